```python
import math
import jax
import jax.numpy as jnp
from jax import lax
import numpy as np

D_MODEL = 1024
BATCH = 16
SEQ = 2048
DEPTH = 2

GRID_W = 64
CTX_LEN = 256
EPS = 1e-6
CHUNK = 64
F32 = jnp.float32

NA_HEAD_DIM = 64
NA_WIDTH = D_MODEL // 2
NA_HEADS = NA_WIDTH // NA_HEAD_DIM
WIN_ROWS = 8
WIN_COLS = 16
COL_BLOCK = 16

HG_WIDTH = D_MODEL // 2
HG_HEAD_DIM = 128
HG_HEADS = HG_WIDTH // HG_HEAD_DIM

EVEN_IN = 3 * NA_WIDTH + 5 * HG_WIDTH
EVEN_MIX = NA_WIDTH + HG_WIDTH

SSD_INNER = 2 * D_MODEL
SSD_HEAD_DIM = 64
SSD_HEADS = SSD_INNER // SSD_HEAD_DIM
SSD_GROUPS = 4
SSD_STATE = 128
SSD_CONV = 3
SSD_XBC = SSD_INNER + 2 * SSD_GROUPS * SSD_STATE
SSD_IN = SSD_INNER + SSD_XBC + 2 * SSD_HEADS

D_FF = ((8 * D_MODEL // 3 + 127) // 128) * 128
FFN_CONV = 3

N_EVEN = (DEPTH + 1) // 2
N_ODD = DEPTH // 2

kernel_name = 'hybrid_natten_hgrn2_ssd_prefix_dit'


def rms_norm(x, gain):
    xf = x.astype(F32)
    y = xf * lax.rsqrt(jnp.mean(xf * xf, axis=-1, keepdims=True) + EPS)
    return (y * gain.astype(F32)).astype(x.dtype)


def modulate(x, gain, shift, scale):
    return rms_norm(x, gain) * (1 + scale[:, None]) + shift[:, None]


def dwconv_centred(x, w, b):
    k = w.shape[0]
    y = lax.conv_general_dilated(x, w[:, None, :].astype(x.dtype), (1,), [(k // 2, k // 2)],
                                 dimension_numbers=('NWC', 'WIO', 'NWC'),
                                 feature_group_count=x.shape[-1])
    return y + b


def conv_ffn(u, w_up, conv_w, conv_b, w_down):
    a, v = jnp.split(u @ w_up, 2, axis=-1)
    return (jax.nn.gelu(dwconv_centred(a, conv_w, conv_b)) * v) @ w_down


def _chunks(a):
    b, t = a.shape[:2]
    return jnp.moveaxis(a.reshape(b, t // CHUNK, CHUNK, *a.shape[2:]), 1, 0)


def _unchunk(a):
    nc, b, cl = a.shape[:3]
    return jnp.moveaxis(a, 0, 1).reshape(b, nc * cl, *a.shape[3:])


def _causal_mask():
    return jnp.tril(jnp.ones((CHUNK, CHUNK), bool))[None, :, :, None, None]


def gla_scan(q, k, v, log_f, s0):
    mask = _causal_mask()

    def step(state, inp):
        qc, kc, vc, gc = inp
        g = jnp.cumsum(gc, axis=1)
        decay = jnp.exp(jnp.where(mask, g[:, :, None] - g[:, None], -jnp.inf))
        att = jnp.einsum('bthd,btshd,bshd->bhts', qc, decay, kc)
        o = (jnp.einsum('bhts,bshv->bthv', att, vc)
             + jnp.einsum('bthd,bhdv->bthv', qc * jnp.exp(g), state))
        g_last = g[:, -1]
        state = (jnp.exp(g_last)[..., None] * state
                 + jnp.einsum('bshd,bshv->bhdv', kc * jnp.exp(g_last[:, None] - g), vc))
        return state, o

    state, o = lax.scan(step, s0, (_chunks(q), _chunks(k), _chunks(v), _chunks(log_f)))
    return _unchunk(o), state


def ssd_scan(x, dt, la, bm, cm, s0):
    mask = _causal_mask()

    def step(state, inp):
        xc, dtc, lac, bc, cc = inp
        cum = jnp.cumsum(lac, axis=1)
        lmat = jnp.exp(jnp.where(mask, cum[:, :, None] - cum[:, None], -jnp.inf))
        cb = jnp.einsum('btgn,bsgn->btsg', cc, bc)
        y = jnp.einsum('btsgh,btsg,bsghp->btghp', lmat, cb, dtc[..., None] * xc)
        y = y + jnp.einsum('btgn,bghpn->btghp', cc, state) * jnp.exp(cum)[..., None]
        w = jnp.exp(cum[:, -1:] - cum) * dtc
        state = (jnp.exp(cum[:, -1])[..., None, None] * state
                 + jnp.einsum('bsgh,bsghp,bsgn->bghpn', w, xc, bc))
        return state, y

    state, y = lax.scan(step, s0, (_chunks(x), _chunks(dt), _chunks(la), _chunks(bm), _chunks(cm)))
    return _unchunk(y), state


def bidir_prefix(scan_fn, lat_f, lat_b, ctx_f, ctx_b, s0):
    flip = lambda t: tuple(jnp.flip(a, axis=1) for a in t)
    yc_f, sc_f = scan_fn(*ctx_f, s0)
    yc_b, sc_b = scan_fn(*flip(ctx_b), s0)
    y_f, _ = scan_fn(*lat_f, sc_f)
    y_b, _ = scan_fn(*flip(lat_b), sc_b)
    return y_f + jnp.flip(y_b, axis=1), yc_f + jnp.flip(yc_b, axis=1)


def neighbourhood_attention(q, k, v, kc, vc, rpb):
    b, n, h, dh = q.shape
    rows = n // GRID_W
    kr = min(WIN_ROWS, rows)
    n_cb = GRID_W // COL_BLOCK
    band = 2 * WIN_COLS
    scale = dh ** -0.5
    grid = lambda a: a.reshape(b, rows, GRID_W, h, dh)
    q, k, v = grid(q), grid(k), grid(v)
    col = jnp.arange(GRID_W)
    band_start = jnp.clip(jnp.arange(n_cb) * COL_BLOCK - WIN_COLS // 2, 0, GRID_W - band)
    band_cols = band_start[:, None] + jnp.arange(band)
    q_cols = col.reshape(n_cb, COL_BLOCK)
    win_start = jnp.clip(q_cols - WIN_COLS // 2, 0, GRID_W - WIN_COLS)
    kcol = band_cols[:, None, :]
    in_win = (kcol >= win_start[..., None]) & (kcol < win_start[..., None] + WIN_COLS)
    dc_idx = jnp.clip(kcol - q_cols[..., None] + WIN_COLS - 1, 0, 2 * WIN_COLS - 2)
    rpb = rpb.astype(F32)

    def row_block(r):
        r0 = jnp.clip(r - kr // 2, 0, rows - kr)
        qb = lax.dynamic_index_in_dim(q, r, axis=1, keepdims=False).reshape(b, n_cb, COL_BLOCK, h, dh)
        kb = lax.dynamic_slice_in_dim(k, r0, kr, axis=1)[:, :, band_cols]
        vb = lax.dynamic_slice_in_dim(v, r0, kr, axis=1)[:, :, band_cols]
        dr_idx = r0 + jnp.arange(kr) - r + WIN_ROWS - 1
        bias = rpb[:, dr_idx][:, :, dc_idx].transpose(0, 2, 3, 1, 4)
        s_loc = jnp.einsum('bjqhd,bijkhd->bhjqik', qb, kb).astype(F32) * scale + bias
        s_loc = jnp.where(in_win[:, :, None, :], s_loc, -jnp.inf)
        s_ctx = jnp.einsum('bjqhd,blhd->bhjql', qb, kc).astype(F32) * scale
        s = jnp.concatenate([s_loc.reshape(*s_loc.shape[:4], kr * band), s_ctx], axis=-1)
        p = jax.nn.softmax(s, axis=-1).astype(v.dtype)
        p_loc = p[..., :kr * band].reshape(s_loc.shape)
        p_ctx = p[..., kr * band:]
        o = (jnp.einsum('bhjqik,bijkhd->bjqhd', p_loc, vb)
             + jnp.einsum('bhjql,blhd->bjqhd', p_ctx, vc))
        return o.reshape(b, GRID_W, h, dh)

    out = lax.map(row_block, jnp.arange(rows))
    return jnp.moveaxis(out, 0, 1).reshape(b, n, h * dh)


def ctx_attention(q, k, v):
    s = jnp.einsum('blhd,bmhd->bhlm', q, k).astype(F32) * q.shape[-1] ** -0.5
    p = jax.nn.softmax(s, axis=-1).astype(v.dtype)
    o = jnp.einsum('bhlm,bmhd->blhd', p, v)
    return o.reshape(o.shape[0], o.shape[1], -1)


def hgrn2_prep(q, f_f, f_b, i, lb_f, lb_b):
    heads = lambda a: a.astype(F32).reshape(a.shape[0], a.shape[1], HG_HEADS, HG_HEAD_DIM)

    def forget(f, lb):
        log_f = jnp.logaddexp(jnp.log(lb), jnp.log1p(-lb) + jax.nn.log_sigmoid(f.astype(F32)))
        return heads(-jnp.expm1(log_f)), heads(log_f)

    k_f, g_f = forget(f_f, lb_f)
    k_b, g_b = forget(f_b, lb_b)
    return heads(jax.nn.silu(q)), k_f, g_f, k_b, g_b, heads(i)


def hybrid_mixer(u, uc, w_in, w_out, q_gain, k_gain, rpb, hg_gain, lb_f, lb_b, need_ctx):
    split_at = [NA_WIDTH, 2 * NA_WIDTH, 3 * NA_WIDTH] + [3 * NA_WIDTH + m * HG_WIDTH for m in range(1, 5)]
    na_heads = lambda a: a.reshape(a.shape[0], a.shape[1], NA_HEADS, NA_HEAD_DIM)

    def project(v_in):
        qa, ka, va, hq, hff, hfb, hi, hg = jnp.split(v_in @ w_in, split_at, axis=-1)
        qa = rms_norm(na_heads(qa), q_gain)
        ka = rms_norm(na_heads(ka), k_gain)
        return (qa, ka, na_heads(va)), hgrn2_prep(hq, hff, hfb, hi, lb_f, lb_b), hg

    (qa, ka, va), (q, kf, gf, kb, gb, vi), g = project(u)
    (qac, kac, vac), (qc, kfc, gfc, kbc, gbc, vic), gc = project(uc)

    o_na = neighbourhood_attention(qa, ka, va, kac, vac, rpb)
    s0 = jnp.zeros((u.shape[0], HG_HEADS, HG_HEAD_DIM, HG_HEAD_DIM), F32)
    o_hg, oc_hg = bidir_prefix(gla_scan, (q, kf, vi, gf), (q, kb, vi, gb),
                               (qc, kfc, vic, gfc), (qc, kbc, vic, gbc), s0)

    def hg_out(o, gate):
        y = rms_norm(o, hg_gain) * jax.nn.silu(gate.astype(F32)).reshape(o.shape)
        return y.reshape(o.shape[0], o.shape[1], HG_WIDTH)

    y = jnp.concatenate([o_na, hg_out(o_hg, g)], axis=-1) @ w_out
    if not need_ctx:
        return y, None
    yc = jnp.concatenate([ctx_attention(qac, kac, vac), hg_out(oc_hg, gc)], axis=-1) @ w_out
    return y, yc


def ssd_mixer(u, uc, w_in, conv_w, conv_b, dt_bias_f, dt_bias_b, a_log_f, a_log_b, d_skip,
              norm_gain, w_out, need_ctx):
    n_hg = SSD_HEADS // SSD_GROUPS
    a_f = -jnp.exp(a_log_f.astype(F32)).reshape(SSD_GROUPS, n_hg)
    a_b = -jnp.exp(a_log_b.astype(F32)).reshape(SSD_GROUPS, n_hg)

    def project(v_in):
        b, t = v_in.shape[:2]
        z, xbc, dt_f, dt_b = jnp.split(v_in @ w_in, [SSD_INNER, SSD_INNER + SSD_XBC,
                                                     SSD_INNER + SSD_XBC + SSD_HEADS], axis=-1)
        xbc = jax.nn.silu(dwconv_centred(xbc, conv_w, conv_b)).astype(F32)
        xs, bm, cm = jnp.split(xbc, [SSD_INNER, SSD_INNER + SSD_GROUPS * SSD_STATE], axis=-1)
        xs = xs.reshape(b, t, SSD_GROUPS, n_hg, SSD_HEAD_DIM)
        bm = bm.reshape(b, t, SSD_GROUPS, SSD_STATE)
        cm = cm.reshape(b, t, SSD_GROUPS, SSD_STATE)

        def direction(dt_raw, dt_bias, a):
            dt = jax.nn.softplus(dt_raw.astype(F32) + dt_bias.astype(F32)).reshape(b, t, SSD_GROUPS, n_hg)
            return (xs, dt, dt * a, bm, cm)

        return z, xs, direction(dt_f, dt_bias_f, a_f), direction(dt_b, dt_bias_b, a_b)

    z, xs, lat_f, lat_b = project(u)
    zc, xsc, ctx_f, ctx_b = project(uc)
    s0 = jnp.zeros((u.shape[0], SSD_GROUPS, n_hg, SSD_HEAD_DIM, SSD_STATE), F32)
    y, yc = bidir_prefix(ssd_scan, lat_f, lat_b, ctx_f, ctx_b, s0)
    d = d_skip.astype(F32).reshape(SSD_GROUPS, n_hg, 1)

    def finish(yy, xx, zz):
        b, t = zz.shape[:2]
        yy = (yy + d * xx).reshape(b, t, SSD_INNER) * jax.nn.silu(zz.astype(F32))
        yy = rms_norm(yy.reshape(b, t, SSD_GROUPS, -1), norm_gain.reshape(SSD_GROUPS, -1))
        return yy.reshape(b, t, SSD_INNER) @ w_out

    y_lat = finish(y, xs, z)
    if not need_ctx:
        return y_lat, None
    return y_lat, finish(yc, xsc, zc)


def setup_inputs(seed: int = 0) -> dict:
    key = jax.random.key(seed)
    keys = iter(jax.random.split(key, 40))
    D = D_MODEL

    def nrm(shape, std):
        return jax.random.normal(next(keys), shape, F32) * std

    def gain(shape):
        return 1.0 + nrm(shape, 0.02)

    def dt_bias(shape):
        dt = jnp.exp(jax.random.uniform(next(keys), shape, F32, math.log(1e-3), math.log(1e-1)))
        return dt + jnp.log(-jnp.expm1(-dt))

    def a_log(shape):
        return jnp.log(jax.random.uniform(next(keys), shape, F32, 1.0, 16.0))

    return {
        'x': nrm((BATCH, SEQ, D), 1.0),
        'c': nrm((BATCH, D), 1.0),
        'ctx': nrm((BATCH, CTX_LEN, D), 1.0),
        'c_ctx': nrm((D,), 1.0),
        'w_mod': nrm((DEPTH, D, 6 * D), 0.5 * D ** -0.5),
        'b_mod': nrm((DEPTH, 6 * D), 0.02),
        'norm_mix': gain((DEPTH, D)),
        'norm_ffn': gain((DEPTH, D)),
        'ffn_w_up': nrm((DEPTH, D, 2 * D_FF), D ** -0.5),
        'ffn_conv_w': nrm((DEPTH, FFN_CONV, D_FF), FFN_CONV ** -0.5),
        'ffn_conv_b': nrm((DEPTH, D_FF), 0.02),
        'ffn_w_down': nrm((DEPTH, D_FF, D), D_FF ** -0.5),
        'hy_w_in': nrm((N_EVEN, D, EVEN_IN), D ** -0.5),
        'hy_w_out': nrm((N_EVEN, EVEN_MIX, D), EVEN_MIX ** -0.5),
        'na_q_gain': gain((N_EVEN, NA_HEAD_DIM)),
        'na_k_gain': gain((N_EVEN, NA_HEAD_DIM)),
        'na_rpb': nrm((N_EVEN, NA_HEADS, 2 * WIN_ROWS - 1, 2 * WIN_COLS - 1), 0.1),
        'hg_out_gain': gain((N_EVEN, HG_HEAD_DIM)),
        'hg_lb_fwd': nrm((DEPTH + 1, HG_WIDTH), 0.5),
        'hg_lb_bwd': nrm((DEPTH + 1, HG_WIDTH), 0.5),
        'ssd_w_in': nrm((N_ODD, D, SSD_IN), D ** -0.5),
        'ssd_conv_w': nrm((N_ODD, SSD_CONV, SSD_XBC), SSD_CONV ** -0.5),
        'ssd_conv_b': nrm((N_ODD, SSD_XBC), 0.02),
        'ssd_dt_bias_fwd': dt_bias((N_ODD, SSD_HEADS)),
        'ssd_dt_bias_bwd': dt_bias((N_ODD, SSD_HEADS)),
        'ssd_a_log_fwd': a_log((N_ODD, SSD_HEADS)),
        'ssd_a_log_bwd': a_log((N_ODD, SSD_HEADS)),
        'ssd_d': 1.0 + nrm((N_ODD, SSD_HEADS), 0.1),
        'ssd_norm_gain': gain((N_ODD, SSD_INNER)),
        'ssd_w_out': nrm((N_ODD, SSD_INNER, D), SSD_INNER ** -0.5),
    }


def reference(x, c, ctx, c_ctx, w_mod, b_mod, norm_mix, norm_ffn, ffn_w_up, ffn_conv_w, ffn_conv_b,
              ffn_w_down, hy_w_in, hy_w_out, na_q_gain, na_k_gain, na_rpb, hg_out_gain, hg_lb_fwd,
              hg_lb_bwd, ssd_w_in, ssd_conv_w, ssd_conv_b, ssd_dt_bias_fwd, ssd_dt_bias_bwd,
              ssd_a_log_fwd, ssd_a_log_bwd, ssd_d, ssd_norm_gain, ssd_w_out):
    lb_f_all = jnp.cumsum(jax.nn.softmax(hg_lb_fwd.astype(F32), axis=0), axis=0)
    lb_b_all = jnp.cumsum(jax.nn.softmax(hg_lb_bwd.astype(F32), axis=0), axis=0)
    s_lat = jax.nn.silu(c)
    s_ctx = jax.nn.silu(c_ctx)[None]
    h, hc = x, ctx
    for l in range(DEPTH):
        need_ctx = l < DEPTH - 1
        j = l // 2
        sh_m, sc_m, g_m, sh_f, sc_f, g_f = jnp.split(s_lat @ w_mod[l] + b_mod[l], 6, axis=-1)
        csh_m, csc_m, cg_m, csh_f, csc_f, cg_f = jnp.split(s_ctx @ w_mod[l] + b_mod[l], 6, axis=-1)
        u = modulate(h, norm_mix[l], sh_m, sc_m)
        uc = modulate(hc, norm_mix[l], csh_m, csc_m)
        if l % 2 == 0:
            y, yc = hybrid_mixer(u, uc, hy_w_in[j], hy_w_out[j], na_q_gain[j], na_k_gain[j], na_rpb[j],
                                 hg_out_gain[j], lb_f_all[l], lb_b_all[l], need_ctx)
        else:
            y, yc = ssd_mixer(u, uc, ssd_w_in[j], ssd_conv_w[j], ssd_conv_b[j], ssd_dt_bias_fwd[j],
                              ssd_dt_bias_bwd[j], ssd_a_log_fwd[j], ssd_a_log_bwd[j], ssd_d[j],
                              ssd_norm_gain[j], ssd_w_out[j], need_ctx)
        h = h + g_m[:, None] * y
        h = h + g_f[:, None] * conv_ffn(modulate(h, norm_ffn[l], sh_f, sc_f), ffn_w_up[l],
                                        ffn_conv_w[l], ffn_conv_b[l], ffn_w_down[l])
        if need_ctx:
            hc = hc + cg_m[:, None] * yc
            hc = hc + cg_f[:, None] * conv_ffn(modulate(hc, norm_ffn[l], csh_f, csc_f), ffn_w_up[l],
                                              ffn_conv_w[l], ffn_conv_b[l], ffn_w_down[l])
    return h
```

```python
import functools
import math

import numpy as np
import jax
import jax.numpy as jnp
from jax import lax
from jax.experimental import pallas as pl
from jax.experimental.pallas import tpu as pltpu

F32 = jnp.float32
BF16 = jnp.bfloat16
EPS = 1e-6

GRID_W = 64
NA_HEAD_DIM = 64
WIN_ROWS = 8
WIN_COLS = 16
HG_HEAD_DIM = 128
SSD_HEAD_DIM = 64
SSD_GROUPS = 4
SSD_STATE = 128
CHUNK = 64
SUB = 16
TM = 256
LANES = 128
MASK_NEG = -1e30
SAFE_RANGE = 60.0

VMEM_LIMIT = 56 * 1024 * 1024


def _cparams(sem):
    return pltpu.CompilerParams(dimension_semantics=sem, vmem_limit_bytes=VMEM_LIMIT)


def _dot(a, b):
    return jnp.dot(a.astype(BF16), b.astype(BF16), preferred_element_type=F32)


def _dot_nt(a, b):
    return lax.dot_general(a.astype(BF16), b.astype(BF16), (((1,), (1,)), ((), ())),
                           preferred_element_type=F32)


def _sigmoid(x):
    return 1.0 / (1.0 + jnp.exp(-x))


def _silu(x):
    return x * _sigmoid(x)


def _softplus(x):
    return jnp.maximum(x, 0.0) + jnp.log1p(jnp.exp(-jnp.abs(x)))


def _gelu_tanh(x):
    c = math.sqrt(2.0 / math.pi)
    return 0.5 * x * (1.0 + jnp.tanh(c * (x + 0.044715 * (x * x * x))))


def _rms_mod(x, gain, shift, scale):
    ms = jnp.mean(x * x, axis=-1, keepdims=True)
    return (x * lax.rsqrt(ms + EPS) * gain) * (1.0 + scale) + shift


def _split3(v):
    p1 = v.astype(BF16)
    r1 = v - p1.astype(F32)
    p2 = r1.astype(BF16)
    r2 = r1 - p2.astype(F32)
    return p1, p2, r2.astype(BF16)


def _tri_cumsum(tri, v):
    p1, p2, p3 = _split3(v)
    d = lambda p: jnp.dot(tri, p, preferred_element_type=F32)
    return d(p1) + d(p2) + d(p3)


def _mod_spec(d, n_lat_tiles):
    return pl.BlockSpec((None, None, 8, d), lambda i, t: (i, t // n_lat_tiles, 0, 0))


def _tri_matrix(n, rev):
    r = lax.broadcasted_iota(jnp.int32, (n, n), 0)
    c = lax.broadcasted_iota(jnp.int32, (n, n), 1)
    return jnp.where((c >= r) if rev else (c <= r), 1.0, 0.0).astype(BF16)


def _mod_kernel(s_ref, w_ref, b_ref, o_ref):
    s = _silu(s_ref[...])
    o_ref[0] = _dot(s, w_ref[0]) + b_ref[0]


def _modulation(cond, w_mod, b_mod):
    depth, d, n = w_mod.shape
    rows = cond.shape[0]
    tn = 1536
    return pl.pallas_call(
        _mod_kernel,
        grid=(depth, n // tn),
        in_specs=[pl.BlockSpec((rows, d), lambda l, j: (0, 0)),
                  pl.BlockSpec((1, d, tn), lambda l, j: (l, 0, j)),
                  pl.BlockSpec((1, 1, tn), lambda l, j: (l, 0, j))],
        out_specs=pl.BlockSpec((1, rows, tn), lambda l, j: (l, 0, j)),
        out_shape=jax.ShapeDtypeStruct((depth, rows, n), F32),
        compiler_params=_cparams(("parallel", "parallel")),
        name="modulation",
    )(cond, w_mod, b_mod.reshape(depth, 1, n))


def _inproj0_kernel(h_ref, m_ref, gain_ref, w_ref, qg_ref, kg_ref, qkv_ref, hraw_ref):
    u = _rms_mod(h_ref[0], gain_ref[...], m_ref[0:1, :], m_ref[1:2, :]).astype(BF16)
    lo = lax.broadcasted_iota(jnp.int32, (1, LANES), 1) < NA_HEAD_DIM
    na_w = qg_ref.shape[1]

    def head_norm(y, g_ref):
        outs = []
        for c in range(na_w // LANES):
            yc = y[:, c * LANES:(c + 1) * LANES]
            sq = yc * yc
            s_lo = jnp.sum(jnp.where(lo, sq, 0.0), axis=-1, keepdims=True)
            s_hi = jnp.sum(jnp.where(lo, 0.0, sq), axis=-1, keepdims=True)
            inv = jnp.where(lo, lax.rsqrt(s_lo / NA_HEAD_DIM + EPS), lax.rsqrt(s_hi / NA_HEAD_DIM + EPS))
            outs.append(yc * inv * g_ref[:, c * LANES:(c + 1) * LANES])
        return jnp.concatenate(outs, axis=-1)

    def proj(lo_col, width):
        return jnp.dot(u, w_ref[:, lo_col:lo_col + width], preferred_element_type=F32)

    qkv_ref[0, :, 0:na_w] = head_norm(proj(0, na_w), qg_ref).astype(BF16)
    qkv_ref[0, :, na_w:2 * na_w] = head_norm(proj(na_w, na_w), kg_ref).astype(BF16)
    qkv_ref[0, :, 2 * na_w:3 * na_w] = proj(2 * na_w, na_w).astype(BF16)
    n_hg = hraw_ref.shape[2]
    step = 512
    for j in range(n_hg // step):
        hraw_ref[0, :, j * step:(j + 1) * step] = proj(3 * na_w + j * step, step)


def _inproj0(hh, mod, gain, w, qg, kg):
    b, tt, d = hh.shape
    n = w.shape[1]
    na_w = qg.shape[1]
    n_hg = n - 3 * na_w
    nt = tt // TM
    n_lat_tiles = nt - 1
    return pl.pallas_call(
        _inproj0_kernel,
        grid=(b, nt),
        in_specs=[pl.BlockSpec((1, TM, d), lambda i, t: (i, t, 0)),
                  _mod_spec(d, n_lat_tiles),
                  pl.BlockSpec((1, d), lambda i, t: (0, 0)),
                  pl.BlockSpec((d, n), lambda i, t: (0, 0)),
                  pl.BlockSpec((1, na_w), lambda i, t: (0, 0)),
                  pl.BlockSpec((1, na_w), lambda i, t: (0, 0))],
        out_specs=[pl.BlockSpec((1, TM, 3 * na_w), lambda i, t: (i, t, 0)),
                   pl.BlockSpec((1, TM, n_hg), lambda i, t: (i, t, 0))],
        out_shape=[jax.ShapeDtypeStruct((b, tt, 3 * na_w), BF16),
                   jax.ShapeDtypeStruct((b, tt, n_hg), F32)],
        compiler_params=_cparams(("parallel", "parallel")),
        name="inproj0",
    )(hh, mod, gain, w, qg, kg)


def _na_kernel(q_ref, k_ref, v_ref, bias_ref, o_ref, *, t_lat, rows):
    tt = q_ref.shape[1]
    lane = lax.broadcasted_iota(jnp.int32, (1, LANES), 1)
    head_mask = [lane < NA_HEAD_DIM, lane >= NA_HEAD_DIM]
    kc = k_ref[0, t_lat:tt, :]
    vc = v_ref[0, t_lat:tt, :]
    kr = min(WIN_ROWS, rows)
    n_loc = kr * GRID_W

    qc = q_ref[0, t_lat:tt, :]
    oc = None
    for hm in head_mask:
        s = _dot_nt(jnp.where(hm, qc, jnp.zeros_like(qc)), kc)
        p = jnp.exp(s - jnp.max(s, axis=-1, keepdims=True))
        o = _dot(p, vc) / jnp.sum(p, axis=-1, keepdims=True)
        oc = o if oc is None else jnp.where(hm, o, oc)
    o_ref[0, t_lat:tt, :] = oc.astype(o_ref.dtype)

    def row_body(r, carry):
        r0 = jnp.clip(r - kr // 2, 0, rows - kr)
        case = r - r0
        q_r = q_ref[0, pl.ds(pl.multiple_of(r * GRID_W, GRID_W), GRID_W), :]
        start = pl.multiple_of(r0 * GRID_W, GRID_W)
        k_loc = k_ref[0, pl.ds(start, n_loc), :]
        v_loc = v_ref[0, pl.ds(start, n_loc), :]
        out = None
        for hi, hm in enumerate(head_mask):
            qh = jnp.where(hm, q_r, jnp.zeros_like(q_r))
            s_loc = _dot_nt(qh, k_loc) + bias_ref[hi, case]
            s_ctx = _dot_nt(qh, kc)
            m = jnp.maximum(jnp.max(s_loc, axis=-1, keepdims=True), jnp.max(s_ctx, axis=-1, keepdims=True))
            p_loc = jnp.exp(s_loc - m)
            p_ctx = jnp.exp(s_ctx - m)
            den = jnp.sum(p_loc, axis=-1, keepdims=True) + jnp.sum(p_ctx, axis=-1, keepdims=True)
            o = (_dot(p_loc, v_loc) + _dot(p_ctx, vc)) / den
            out = o if out is None else jnp.where(hm, o, out)
        o_ref[0, pl.ds(pl.multiple_of(r * GRID_W, GRID_W), GRID_W), :] = out.astype(o_ref.dtype)
        return carry

    lax.fori_loop(0, rows, row_body, 0)


def _na_bias_table(rpb, rows):
    kr = min(WIN_ROWS, rows)
    q = np.arange(GRID_W)
    kcol = np.arange(GRID_W)
    ws = np.clip(q - WIN_COLS // 2, 0, GRID_W - WIN_COLS)
    in_win = (kcol[None, :] >= ws[:, None]) & (kcol[None, :] < ws[:, None] + WIN_COLS)
    dc = np.clip(kcol[None, :] - q[:, None] + WIN_COLS - 1, 0, 2 * WIN_COLS - 2)
    dr = np.arange(kr)[None, :] - np.arange(kr)[:, None] + WIN_ROWS - 1
    tbl = rpb.astype(F32)[:, dr][:, :, :, dc]
    tbl = jnp.where(in_win[None, None, None], tbl, MASK_NEG)
    h = rpb.shape[0]
    return tbl.transpose(0, 1, 3, 2, 4).reshape(h, kr, GRID_W, kr * GRID_W)


def _na_attention(qkv, bias, t_lat):
    b, tt, w3 = qkv.shape
    na_w = w3 // 3
    ncol = na_w // LANES
    rows = t_lat // GRID_W
    kr = bias.shape[1]
    kern = functools.partial(_na_kernel, t_lat=t_lat, rows=rows)
    return pl.pallas_call(
        kern,
        grid=(b, ncol),
        in_specs=[pl.BlockSpec((1, tt, LANES), lambda i, p: (i, 0, p)),
                  pl.BlockSpec((1, tt, LANES), lambda i, p: (i, 0, ncol + p)),
                  pl.BlockSpec((1, tt, LANES), lambda i, p: (i, 0, 2 * ncol + p)),
                  pl.BlockSpec((2, kr, GRID_W, kr * GRID_W), lambda i, p: (p, 0, 0, 0))],
        out_specs=pl.BlockSpec((1, tt, LANES), lambda i, p: (i, 0, p)),
        out_shape=jax.ShapeDtypeStruct((b, tt, na_w), BF16),
        compiler_params=_cparams(("parallel", "parallel")),
        name="na_attention",
    )(qkv, qkv, qkv, bias)


def _scan_block(i, nt, rev):
    if rev:
        return jnp.where(i == 0, nt - 1, nt - 1 - i)
    return jnp.where(i == 0, nt - 1, i - 1)


def _hgrn_kernel(q_ref, f_ref, v_ref, lb_ref, o_ref, st_ref, *, rev):
    n_heads = st_ref.shape[0]
    hd = HG_HEAD_DIM
    n_sub = CHUNK // SUB

    @pl.when(pl.program_id(1) == 0)
    def _():
        st_ref[...] = jnp.zeros_like(st_ref)

    tri = _tri_matrix(CHUNK, rev)
    row = lax.broadcasted_iota(jnp.int32, (CHUNK, CHUNK), 0)
    col = lax.broadcasted_iota(jnp.int32, (CHUNK, CHUNK), 1)
    causal = (col >= row) if rev else (col <= row)
    rowv = lax.broadcasted_iota(jnp.int32, (CHUNK, 1), 0)
    log_lb = lb_ref[0:1, :]
    log_1m_lb = lb_ref[1:2, :]
    one_m_lb = lb_ref[2:3, :]
    last = 0 if rev else CHUNK - 1

    n_chunks = q_ref.shape[1] // CHUNK
    order = range(n_chunks - 1, -1, -1) if rev else range(n_chunks)
    for cc in order:
        rs = slice(cc * CHUNK, (cc + 1) * CHUNK)
        x = f_ref[0, rs, :]
        t = jnp.log1p(jnp.exp(-jnp.abs(x)))
        ls = jnp.minimum(x, 0.0) - t
        lsn = jnp.minimum(-x, 0.0) - t
        bb = log_1m_lb + ls
        log_f = jnp.maximum(log_lb, bb) + jnp.log1p(jnp.exp(-jnp.abs(log_lb - bb)))
        k_all = one_m_lb * jnp.exp(lsn)
        q_all = _silu(q_ref[0, rs, :])
        v_all = v_ref[0, rs, :]
        g_all = _tri_cumsum(tri, log_f)

        rng = None
        for sb in range(n_sub):
            d = jnp.abs(g_all[sb * SUB:sb * SUB + 1, :] - g_all[(sb + 1) * SUB - 1:(sb + 1) * SUB, :])
            rng = d if rng is None else jnp.maximum(rng, d)
        safe = jnp.max(rng) <= SAFE_RANGE

        zero_row = jnp.zeros((1, g_all.shape[1]), F32)
        g_b, g_m = [], []
        for sb in range(n_sub):
            if rev:
                g_b.append(g_all[(sb + 1) * SUB:(sb + 1) * SUB + 1, :] if sb < n_sub - 1 else zero_row)
            else:
                g_b.append(g_all[sb * SUB - 1:sb * SUB, :] if sb > 0 else zero_row)
            g_m.append(g_all[sb * SUB + SUB // 2:sb * SUB + SUB // 2 + 1, :])
        rows_of = lambda rws: jnp.concatenate([jnp.broadcast_to(r, (SUB, r.shape[1])) for r in rws], axis=0)
        gb_rows, gm_rows = rows_of(g_b), rows_of(g_m)
        clip = lambda e: jnp.clip(e, -SAFE_RANGE, SAFE_RANGE)
        q_b = q_all * jnp.exp(g_all - gb_rows)
        q_m = q_all * jnp.exp(clip(g_all - gm_rows))
        k_m = jnp.where(safe, k_all * jnp.exp(clip(gm_rows - g_all)), 0.0)
        k_prev = []
        for sb in range(n_sub):
            prev = (rowv >= (sb + 1) * SUB) if rev else (rowv < sb * SUB)
            k_prev.append(jnp.where(prev, k_all * jnp.exp(jnp.where(prev, g_b[sb] - g_all, 0.0)), 0.0))
        first = n_sub - 1 if rev else 0
        q_g = q_all * jnp.exp(g_all)
        g_last = g_all[last:last + 1, :]
        k_last = k_all * jnp.exp(g_last - g_all)
        e_last = jnp.exp(g_last)

        outs = []
        for h in range(n_heads):
            hs = slice(h * hd, (h + 1) * hd)
            blocks = []
            for sb in range(n_sub):
                bs = slice(sb * SUB, (sb + 1) * SUB)
                k_in = jnp.where(rowv // SUB == sb, k_m[:, hs], 0.0)
                if sb == first:
                    blocks.append(_dot_nt(q_m[bs, hs], k_in))
                else:
                    blocks.append(_dot_nt(jnp.concatenate([q_b[bs, hs], q_m[bs, hs]], axis=-1),
                                          jnp.concatenate([k_prev[sb][:, hs], k_in], axis=-1)))
            a = jnp.where(causal, jnp.concatenate(blocks, axis=0), 0.0)

            def slow_diag(q=q_all[:, hs], k=k_all[:, hs], g=g_all[:, hs]):
                acc = jnp.zeros((CHUNK, CHUNK), F32)
                pos = rowv % SUB
                for dlt in range(SUB):
                    shift = (CHUNK - dlt) % CHUNK if rev else dlt
                    k_d = pltpu.roll(k, shift, 0) if shift else k
                    g_d = pltpu.roll(g, shift, 0) if shift else g
                    valid = (pos + dlt <= SUB - 1) if rev else (pos >= dlt)
                    e = jnp.exp(jnp.where(valid, g - g_d, 0.0))
                    val = jnp.sum(q * k_d * e, axis=-1, keepdims=True)
                    partner = (row + dlt) if rev else (row - dlt)
                    acc = acc + jnp.where((col == partner) & valid, val, 0.0)
                return acc

            a = a + lax.cond(safe, lambda: jnp.zeros((CHUNK, CHUNK), F32), slow_diag)
            st = st_ref[h]
            o = _dot(a, v_all[:, hs]) + _dot_nt(q_g[:, hs], st)
            st_ref[h] = st * e_last[:, hs] + lax.dot_general(
                v_all[:, hs].astype(BF16), k_last[:, hs].astype(BF16), (((0,), (0,)), ((), ())),
                preferred_element_type=F32)
            outs.append(o)
        o_ref[0, rs, :] = jnp.concatenate(outs, axis=-1)


def _hgrn_scan(hraw, lb_rows, f_col, rev):
    b, tt, n = hraw.shape
    w = lb_rows.shape[1]
    nt = tt // TM
    n_heads = w // HG_HEAD_DIM
    blk = lambda i: _scan_block(i, nt, rev)
    return pl.pallas_call(
        functools.partial(_hgrn_kernel, rev=rev),
        grid=(b, nt),
        in_specs=[pl.BlockSpec((1, TM, w), lambda bi, i: (bi, blk(i), 0)),
                  pl.BlockSpec((1, TM, w), lambda bi, i: (bi, blk(i), f_col)),
                  pl.BlockSpec((1, TM, w), lambda bi, i: (bi, blk(i), 3)),
                  pl.BlockSpec((8, w), lambda bi, i: (0, 0))],
        out_specs=pl.BlockSpec((1, TM, w), lambda bi, i: (bi, blk(i), 0)),
        out_shape=jax.ShapeDtypeStruct((b, tt, w), F32),
        scratch_shapes=[pltpu.VMEM((n_heads, HG_HEAD_DIM, HG_HEAD_DIM), F32)],
        compiler_params=_cparams(("parallel", "arbitrary")),
        name="hgrn_bwd" if rev else "hgrn_fwd",
    )(hraw, hraw, hraw, lb_rows)


def _outproj0_kernel(na_ref, of_ref, ob_ref, gate_ref, h_ref, m_ref, gain_ref, w_ref, o_ref):
    na_w = na_ref.shape[2]
    o = of_ref[0] + ob_ref[0]
    gate = _silu(gate_ref[0])
    ys = []
    for h in range(o.shape[1] // HG_HEAD_DIM):
        hs = slice(h * HG_HEAD_DIM, (h + 1) * HG_HEAD_DIM)
        oh = o[:, hs]
        ms = jnp.mean(oh * oh, axis=-1, keepdims=True)
        ys.append(oh * lax.rsqrt(ms + EPS) * gain_ref[:, hs] * gate[:, hs])
    y_hg = jnp.concatenate(ys, axis=-1)
    y = jnp.dot(na_ref[0], w_ref[0:na_w, :], preferred_element_type=F32) + _dot(y_hg, w_ref[na_w:, :])
    o_ref[0] = h_ref[0] + m_ref[2:3, :] * y


def _outproj0(o_na, o_f, o_b, hraw, hh, mod, gain, w):
    b, tt, d = hh.shape
    na_w = o_na.shape[2]
    hw = o_f.shape[2]
    nt = tt // TM
    n_lat_tiles = nt - 1
    tile = lambda wd: pl.BlockSpec((1, TM, wd), lambda i, t: (i, t, 0))
    return pl.pallas_call(
        _outproj0_kernel,
        grid=(b, nt),
        in_specs=[tile(na_w), tile(hw), tile(hw),
                  pl.BlockSpec((1, TM, hw), lambda i, t: (i, t, 4)),
                  tile(d),
                  _mod_spec(d, n_lat_tiles),
                  pl.BlockSpec((1, hw), lambda i, t: (0, 0)),
                  pl.BlockSpec(w.shape, lambda i, t: (0, 0))],
        out_specs=tile(d),
        out_shape=jax.ShapeDtypeStruct((b, tt, d), F32),
        compiler_params=_cparams(("parallel", "parallel")),
        name="outproj0",
    )(o_na, o_f, o_b, hraw, hh, mod, gain, w)


def _ffn_up_kernel(h_ref, m_ref, gain_ref, w_ref, a_ref, v_ref):
    u = _rms_mod(h_ref[0], gain_ref[...], m_ref[3:4, :], m_ref[4:5, :]).astype(BF16)
    dff = a_ref.shape[2]
    step = dff // 2
    for j in range(2):
        a_ref[0, :, j * step:(j + 1) * step] = jnp.dot(
            u, w_ref[:, j * step:(j + 1) * step], preferred_element_type=F32).astype(a_ref.dtype)
        v_ref[0, :, j * step:(j + 1) * step] = jnp.dot(
            u, w_ref[:, dff + j * step:dff + (j + 1) * step], preferred_element_type=F32).astype(v_ref.dtype)


def _ffn_up(hh, mod, gain, w, n_tiles, n_lat_tiles):
    b, _, d = hh.shape
    dff = w.shape[1] // 2
    tt = n_tiles * TM
    return pl.pallas_call(
        _ffn_up_kernel,
        grid=(b, n_tiles),
        in_specs=[pl.BlockSpec((1, TM, d), lambda i, t: (i, t, 0)),
                  _mod_spec(d, n_lat_tiles),
                  pl.BlockSpec((1, d), lambda i, t: (0, 0)),
                  pl.BlockSpec(w.shape, lambda i, t: (0, 0))],
        out_specs=[pl.BlockSpec((1, TM, dff), lambda i, t: (i, t, 0))] * 2,
        out_shape=[jax.ShapeDtypeStruct((b, tt, dff), BF16)] * 2,
        compiler_params=_cparams(("parallel", "parallel")),
        name="ffn_up",
    )(hh, mod, gain, w)


def _conv3_rows(a, prev_row, next_row, w_ref, b_ref):
    n = a.shape[0]
    row = lax.broadcasted_iota(jnp.int32, (n, 1), 0)
    up = jnp.where(row == 0, prev_row, pltpu.roll(a, 1, 0))
    dn = jnp.where(row == n - 1, next_row, pltpu.roll(a, n - 1, 0))
    return w_ref[0:1, :] * up + w_ref[1:2, :] * a + w_ref[2:3, :] * dn + b_ref[...]


def _halo_rows(prev_ref, next_ref, n_lat_tiles):
    t = pl.program_id(1)
    has_prev = jnp.logical_and(t != 0, t != n_lat_tiles)
    has_next = jnp.logical_and(t != n_lat_tiles - 1, t != n_lat_tiles)
    prev_row = jnp.where(has_prev, prev_ref[0, 7:8, :].astype(F32), 0.0)
    next_row = jnp.where(has_next, next_ref[0, 0:1, :].astype(F32), 0.0)
    return prev_row, next_row


def _halo_specs(width, tt):
    r8 = TM // 8
    last = tt // 8 - 1
    return [pl.BlockSpec((1, 8, width), lambda i, t: (i, jnp.maximum(t * r8 - 1, 0), 0)),
            pl.BlockSpec((1, 8, width), lambda i, t: (i, jnp.minimum((t + 1) * r8, last), 0))]


def _ffn_down_kernel(a_ref, ap_ref, an_ref, v_ref, cw_ref, cb_ref, w_ref, h_ref, m_ref, o_ref, *, n_lat_tiles):
    prev_row, next_row = _halo_rows(ap_ref, an_ref, n_lat_tiles)
    c = _conv3_rows(a_ref[0].astype(F32), prev_row, next_row, cw_ref, cb_ref)
    mid = (_gelu_tanh(c) * v_ref[0].astype(F32)).astype(BF16)
    y = jnp.dot(mid, w_ref[...], preferred_element_type=F32)
    o_ref[0] = h_ref[0] + m_ref[5:6, :] * y


def _ffn_down(a, v, conv_w, conv_b, w, hh, mod, n_lat_tiles):
    b, tt, dff = a.shape
    d = hh.shape[2]
    n_tiles = tt // TM
    tile = lambda wd: pl.BlockSpec((1, TM, wd), lambda i, t: (i, t, 0))
    return pl.pallas_call(
        functools.partial(_ffn_down_kernel, n_lat_tiles=n_lat_tiles),
        grid=(b, n_tiles),
        in_specs=[tile(dff)] + _halo_specs(dff, tt) + [
            tile(dff),
            pl.BlockSpec((3, dff), lambda i, t: (0, 0)),
            pl.BlockSpec((1, dff), lambda i, t: (0, 0)),
            pl.BlockSpec(w.shape, lambda i, t: (0, 0)),
            tile(d),
            _mod_spec(d, n_lat_tiles)],
        out_specs=tile(d),
        out_shape=jax.ShapeDtypeStruct((b, tt, d), F32),
        compiler_params=_cparams(("parallel", "parallel")),
        name="ffn_down",
    )(a, a, a, v, conv_w, conv_b, w, hh, mod)


def _inproj1_kernel(h_ref, m_ref, gain_ref, w_ref, dtb_ref, a_ref, z_ref, xbc_ref, dtp_ref):
    u = _rms_mod(h_ref[0], gain_ref[...], m_ref[0:1, :], m_ref[1:2, :]).astype(BF16)
    nz = z_ref.shape[2]
    nx = xbc_ref.shape[2]
    step = 512
    for j in range(nz // step):
        z_ref[0, :, j * step:(j + 1) * step] = jnp.dot(
            u, w_ref[:, j * step:(j + 1) * step], preferred_element_type=F32).astype(z_ref.dtype)
    for j in range(nx // step):
        xbc_ref[0, :, j * step:(j + 1) * step] = jnp.dot(
            u, w_ref[:, nz + j * step:nz + (j + 1) * step], preferred_element_type=F32).astype(xbc_ref.dtype)
    raw = jnp.dot(u, w_ref[:, nz + nx:], preferred_element_type=F32)
    dt = _softplus(raw + dtb_ref[...])
    dtp_ref[0, :, 0:2 * LANES] = dt
    dtp_ref[0, :, 2 * LANES:4 * LANES] = dt * a_ref[...]


def _inproj1(hh, mod, gain, w, dt_bias, a_neg, nz, nx):
    b, tt, d = hh.shape
    nt = tt // TM
    n_lat_tiles = nt - 1
    tile = lambda wd: pl.BlockSpec((1, TM, wd), lambda i, t: (i, t, 0))
    return pl.pallas_call(
        _inproj1_kernel,
        grid=(b, nt),
        in_specs=[tile(d),
                  _mod_spec(d, n_lat_tiles),
                  pl.BlockSpec((1, d), lambda i, t: (0, 0)),
                  pl.BlockSpec(w.shape, lambda i, t: (0, 0)),
                  pl.BlockSpec((1, 2 * LANES), lambda i, t: (0, 0)),
                  pl.BlockSpec((1, 2 * LANES), lambda i, t: (0, 0))],
        out_specs=[tile(nz), tile(nx), tile(4 * LANES)],
        out_shape=[jax.ShapeDtypeStruct((b, tt, nz), BF16),
                   jax.ShapeDtypeStruct((b, tt, nx), BF16),
                   jax.ShapeDtypeStruct((b, tt, 4 * LANES), F32)],
        compiler_params=_cparams(("parallel", "parallel")),
        name="inproj1",
    )(hh, mod, gain, w, dt_bias, a_neg)


def _ssd_conv_kernel(x_ref, xp_ref, xn_ref, cw_ref, cb_ref, xs_ref, bc_ref, *, n_lat_tiles):
    prev_row, next_row = _halo_rows(xp_ref, xn_ref, n_lat_tiles)
    c = _silu(_conv3_rows(x_ref[0].astype(F32), prev_row, next_row, cw_ref, cb_ref))
    nxs = xs_ref.shape[2]
    xs_ref[0] = c[:, :nxs].astype(xs_ref.dtype)
    bc_ref[0] = c[:, nxs:].astype(bc_ref.dtype)


def _ssd_conv(xbc, conv_w, conv_b, nxs):
    b, tt, nx = xbc.shape
    nt = tt // TM
    tile = lambda wd: pl.BlockSpec((1, TM, wd), lambda i, t: (i, t, 0))
    return pl.pallas_call(
        functools.partial(_ssd_conv_kernel, n_lat_tiles=nt - 1),
        grid=(b, nt),
        in_specs=[tile(nx)] + _halo_specs(nx, tt) + [
            pl.BlockSpec((3, nx), lambda i, t: (0, 0)),
            pl.BlockSpec((1, nx), lambda i, t: (0, 0))],
        out_specs=[tile(nxs), tile(nx - nxs)],
        out_shape=[jax.ShapeDtypeStruct((b, tt, nxs), BF16),
                   jax.ShapeDtypeStruct((b, tt, nx - nxs), BF16)],
        compiler_params=_cparams(("parallel", "parallel")),
        name="ssd_conv",
    )(xbc, xbc, xbc, conv_w, conv_b)


def _ssd_kernel(x_ref, bc_ref, dt_ref, la_ref, e_ref, o_ref, s_ref, *, rev):
    n_groups = s_ref.shape[0]
    ns = SSD_STATE
    gw = s_ref.shape[2]
    width = x_ref.shape[2]
    n_heads = width // SSD_HEAD_DIM
    blk = 256

    @pl.when(pl.program_id(1) == 0)
    def _():
        s_ref[...] = jnp.zeros_like(s_ref)

    tri = _tri_matrix(CHUNK, rev)
    lane128 = lax.broadcasted_iota(jnp.int32, (1, LANES), 1)
    rowc = lax.broadcasted_iota(jnp.int32, (CHUNK, width), 0)
    pos = lax.broadcasted_iota(jnp.int32, (CHUNK, width), 1) % SSD_HEAD_DIM
    on_diag = pos == rowc
    causal = (pos >= rowc) if rev else (pos <= rowc)
    blk_head = lax.broadcasted_iota(jnp.int32, (1, blk), 1) // SSD_HEAD_DIM
    last = 0 if rev else CHUNK - 1

    def expand(v):
        p1, p2, p3 = _split3(v)
        zero = jnp.zeros_like(p1)
        st = jnp.where(lane128 < n_heads, p1,
                       jnp.where(lane128 < 2 * n_heads, p2, jnp.where(lane128 < 3 * n_heads, p3, zero)))
        return jnp.dot(st, e_ref[...], preferred_element_type=F32)

    n_chunks = x_ref.shape[1] // CHUNK
    order = range(n_chunks - 1, -1, -1) if rev else range(n_chunks)
    for cc in order:
        rs = slice(cc * CHUNK, (cc + 1) * CHUNK)
        x = x_ref[0, rs, :]
        bc = bc_ref[0, rs, :]
        cum = expand(_tri_cumsum(tri, la_ref[0, rs, :]))
        dt = expand(dt_ref[0, rs, :])
        cum_row = jnp.sum(jnp.where(on_diag, cum, 0.0), axis=0, keepdims=True)
        dt_row = jnp.sum(jnp.where(on_diag, dt, 0.0), axis=0, keepdims=True)
        cum_last = cum[last:last + 1, :]
        decay = jnp.exp(jnp.where(causal, cum - cum_row, 0.0))
        m_all = jnp.where(causal, decay, 0.0) * dt_row
        e_cum = jnp.exp(cum)
        w_all = jnp.exp(cum_last - cum) * dt
        e_last = jnp.exp(cum_last)
        ys = []
        for g in range(n_groups):
            gs = slice(g * gw, (g + 1) * gw)
            b_g = bc[:, g * ns:(g + 1) * ns]
            c_g = bc[:, (n_groups + g) * ns:(n_groups + g + 1) * ns]
            b_rep = jnp.concatenate([b_g] * (gw // CHUNK), axis=0)
            m_g = (m_all[:, gs] * _dot_nt(c_g, b_rep)).astype(BF16)
            x_g = x[:, gs]
            y_parts = []
            for j in range(gw // blk):
                x4 = x_g[:, j * blk:(j + 1) * blk]
                x_bd = jnp.concatenate(
                    [jnp.where(blk_head == hh, x4, jnp.zeros_like(x4)) for hh in range(blk // SSD_HEAD_DIM)], axis=0)
                y_parts.append(jnp.dot(m_g[:, j * blk:(j + 1) * blk], x_bd, preferred_element_type=F32))
            s_g = s_ref[g]
            y_g = jnp.concatenate(y_parts, axis=-1) + _dot(c_g, s_g) * e_cum[:, gs]
            xw = x_g.astype(F32) * w_all[:, gs]
            s_ref[g] = s_g * e_last[:, gs] + lax.dot_general(
                b_g, xw.astype(BF16), (((0,), (0,)), ((), ())), preferred_element_type=F32)
            ys.append(y_g)
        o_ref[0, rs, :] = jnp.concatenate(ys, axis=-1)


def _ssd_scan(xs, bc, dtp, expand_mat, rev):
    b, tt, width = xs.shape
    nt = tt // TM
    nbc = bc.shape[2]
    gw = width // SSD_GROUPS
    blk = lambda i: _scan_block(i, nt, rev)
    d_col = 1 if rev else 0
    return pl.pallas_call(
        functools.partial(_ssd_kernel, rev=rev),
        grid=(b, nt),
        in_specs=[pl.BlockSpec((1, TM, width), lambda bi, i: (bi, blk(i), 0)),
                  pl.BlockSpec((1, TM, nbc), lambda bi, i: (bi, blk(i), 0)),
                  pl.BlockSpec((1, TM, LANES), lambda bi, i: (bi, blk(i), d_col)),
                  pl.BlockSpec((1, TM, LANES), lambda bi, i: (bi, blk(i), 2 + d_col)),
                  pl.BlockSpec(expand_mat.shape, lambda bi, i: (0, 0))],
        out_specs=pl.BlockSpec((1, TM, width), lambda bi, i: (bi, blk(i), 0)),
        out_shape=jax.ShapeDtypeStruct((b, tt, width), F32),
        scratch_shapes=[pltpu.VMEM((SSD_GROUPS, SSD_STATE, gw), F32)],
        compiler_params=_cparams(("parallel", "arbitrary")),
        name="ssd_bwd" if rev else "ssd_fwd",
    )(xs, bc, dtp, dtp, expand_mat)


def _outproj1_kernel(yf_ref, yb_ref, xs_ref, z_ref, h_ref, m_ref, d_ref, gain_ref, w_ref, o_ref):
    yy = (yf_ref[0] + yb_ref[0] + d_ref[...] * xs_ref[0].astype(F32)) * _silu(z_ref[0].astype(F32))
    width = yy.shape[1]
    gw = width // SSD_GROUPS
    acc = None
    for g in range(SSD_GROUPS):
        gs = slice(g * gw, (g + 1) * gw)
        yg = yy[:, gs]
        ms = jnp.mean(yg * yg, axis=-1, keepdims=True)
        part = _dot(yg * lax.rsqrt(ms + EPS) * gain_ref[:, gs], w_ref[gs, :])
        acc = part if acc is None else acc + part
    o_ref[0] = h_ref[0] + m_ref[2:3, :] * acc


def _outproj1(y_f, y_b, xs, z, hh, mod, d_skip, gain, w, n_lat_tiles):
    b, _, d = hh.shape
    width = xs.shape[2]
    tile = lambda wd: pl.BlockSpec((1, TM, wd), lambda i, t: (i, t, 0))
    return pl.pallas_call(
        _outproj1_kernel,
        grid=(b, n_lat_tiles),
        in_specs=[tile(width), tile(width), tile(width), tile(width), tile(d),
                  _mod_spec(d, n_lat_tiles),
                  pl.BlockSpec((1, width), lambda i, t: (0, 0)),
                  pl.BlockSpec((1, width), lambda i, t: (0, 0)),
                  pl.BlockSpec(w.shape, lambda i, t: (0, 0))],
        out_specs=tile(d),
        out_shape=jax.ShapeDtypeStruct((b, n_lat_tiles * TM, d), F32),
        compiler_params=_cparams(("parallel", "parallel")),
        name="outproj1",
    )(y_f, y_b, xs, z, hh, mod, d_skip, gain, w)


def _mod_rows(mods_l, b, d):
    six = mods_l.reshape(mods_l.shape[0], 6, d)
    lat = six[:b]
    ctx = jnp.broadcast_to(six[b:b + 1], (b, 6, d))
    both = jnp.stack([lat, ctx], axis=1)
    return jnp.pad(both, ((0, 0), (0, 0), (0, 2), (0, 0)))


def kernel(x, c, ctx, c_ctx, w_mod, b_mod, norm_mix, norm_ffn, ffn_w_up, ffn_conv_w, ffn_conv_b, ffn_w_down,
           hy_w_in, hy_w_out, na_q_gain, na_k_gain, na_rpb, hg_out_gain, hg_lb_fwd, hg_lb_bwd, ssd_w_in,
           ssd_conv_w, ssd_conv_b, ssd_dt_bias_fwd, ssd_dt_bias_bwd, ssd_a_log_fwd, ssd_a_log_bwd, ssd_d,
           ssd_norm_gain, ssd_w_out):
    b, t_lat, d = x.shape
    l_ctx = ctx.shape[1]
    assert l_ctx == TM and t_lat % TM == 0 and w_mod.shape[0] == 2
    n_lat_tiles = t_lat // TM
    nt = n_lat_tiles + 1
    hh = jnp.concatenate([x, ctx], axis=1)

    rows = ((b + 1 + 7) // 8) * 8
    cond = jnp.concatenate([c, c_ctx[None], jnp.zeros((rows - b - 1, d), F32)], axis=0)
    mods = _modulation(cond, w_mod, b_mod)
    mod0 = _mod_rows(mods[0], b, d)
    mod1 = _mod_rows(mods[1], b, d)
    row = lambda v: v.reshape(1, -1).astype(F32)

    hg_w = hg_lb_fwd.shape[1]
    na_w = hy_w_out.shape[1] - hg_w
    n_na_heads = na_w // NA_HEAD_DIM
    qg = row(jnp.tile(na_q_gain[0], n_na_heads)) * (NA_HEAD_DIM ** -0.5)
    kg = row(jnp.tile(na_k_gain[0], n_na_heads))
    qkv, hraw = _inproj0(hh, mod0, row(norm_mix[0]), hy_w_in[0].astype(BF16), qg, kg)
    bias = _na_bias_table(na_rpb[0], t_lat // GRID_W)
    o_na = _na_attention(qkv, bias, t_lat)

    def lb_rows(lb_param):
        lb = jnp.cumsum(jax.nn.softmax(lb_param.astype(F32), axis=0), axis=0)[0]
        r = jnp.stack([jnp.log(lb), jnp.log1p(-lb), 1.0 - lb], axis=0)
        return jnp.pad(r, ((0, 5), (0, 0)))

    o_f = _hgrn_scan(hraw, lb_rows(hg_lb_fwd), 1, rev=False)
    o_b = _hgrn_scan(hraw, lb_rows(hg_lb_bwd), 2, rev=True)
    hg_gain = row(jnp.tile(hg_out_gain[0], hg_w // HG_HEAD_DIM))
    h1 = _outproj0(o_na, o_f, o_b, hraw, hh, mod0, hg_gain, hy_w_out[0].astype(BF16))

    a0, v0 = _ffn_up(h1, mod0, row(norm_ffn[0]), ffn_w_up[0].astype(BF16), nt, n_lat_tiles)
    h2 = _ffn_down(a0, v0, ffn_conv_w[0], row(ffn_conv_b[0]), ffn_w_down[0].astype(BF16), h1, mod0,
                   n_lat_tiles)

    n_heads = ssd_d.shape[1]
    inner = ssd_w_out.shape[1]
    nxbc = ssd_conv_w.shape[2]
    w1 = ssd_w_in[0]
    rep = LANES // n_heads
    w_dtf = jnp.tile(w1[:, inner + nxbc:inner + nxbc + n_heads], (1, rep))
    w_dtb = jnp.tile(w1[:, inner + nxbc + n_heads:], (1, rep))
    w1p = jnp.concatenate([w1[:, :inner + nxbc], w_dtf, w_dtb], axis=1).astype(BF16)
    dt_bias = row(jnp.concatenate([jnp.tile(ssd_dt_bias_fwd[0], rep), jnp.tile(ssd_dt_bias_bwd[0], rep)]))
    a_neg = row(jnp.concatenate([jnp.tile(-jnp.exp(ssd_a_log_fwd[0].astype(F32)), rep),
                                 jnp.tile(-jnp.exp(ssd_a_log_bwd[0].astype(F32)), rep)]))
    z, xbc, dtp = _inproj1(h2, mod1, row(norm_mix[1]), w1p, dt_bias, a_neg, inner, nxbc)
    xs, bc = _ssd_conv(xbc, ssd_conv_w[0], row(ssd_conv_b[0]), inner)

    lane = np.arange(LANES)[:, None]
    colh = (np.arange(inner) // SSD_HEAD_DIM)[None, :]
    expand_mat = jnp.asarray(((lane % n_heads == colh) & (lane < 3 * n_heads)).astype(np.float32), dtype=BF16)
    y_f = _ssd_scan(xs, bc, dtp, expand_mat, rev=False)
    y_b = _ssd_scan(xs, bc, dtp, expand_mat, rev=True)
    d_skip = row(jnp.repeat(ssd_d[0], SSD_HEAD_DIM))
    h3 = _outproj1(y_f, y_b, xs, z, h2, mod1, d_skip, row(ssd_norm_gain[0]), ssd_w_out[0].astype(BF16),
                   n_lat_tiles)

    a1, v1 = _ffn_up(h3, mod1, row(norm_ffn[1]), ffn_w_up[1].astype(BF16), n_lat_tiles, n_lat_tiles)
    return _ffn_down(a1, v1, ffn_conv_w[1], row(ffn_conv_b[1]), ffn_w_down[1].astype(BF16), h3, mod1,
                     n_lat_tiles)
```

```python
import functools
import math

import numpy as np
import jax
import jax.numpy as jnp
from jax import lax
from jax.experimental import pallas as pl
from jax.experimental.pallas import tpu as pltpu

F32 = jnp.float32
BF16 = jnp.bfloat16
EPS = 1e-6

GRID_W = 64
NA_HEAD_DIM = 64
WIN_ROWS = 8
WIN_COLS = 16
NA_ROW_BLOCK = 4
HG_HEAD_DIM = 128
SSD_HEAD_DIM = 64
SSD_GROUPS = 4
SSD_STATE = 128
CHUNK = 64
SUB = 16
TM = 256
LANES = 128
MASK_NEG = -1e30
SAFE_RANGE = 60.0

VMEM_LIMIT = 56 * 1024 * 1024


def _cparams(sem):
    return pltpu.CompilerParams(dimension_semantics=sem, vmem_limit_bytes=VMEM_LIMIT)


def _dot(a, b):
    return jnp.dot(a.astype(BF16), b.astype(BF16), preferred_element_type=F32)


def _dot_nt(a, b):
    return lax.dot_general(a.astype(BF16), b.astype(BF16), (((1,), (1,)), ((), ())),
                           preferred_element_type=F32)


def _sigmoid(x):
    return 1.0 / (1.0 + jnp.exp(-x))


def _silu(x):
    return x * _sigmoid(x)


def _softplus(x):
    return jnp.maximum(x, 0.0) + jnp.log1p(jnp.exp(-jnp.abs(x)))


def _gelu_tanh(x):
    c = math.sqrt(2.0 / math.pi)
    return 0.5 * x * (1.0 + jnp.tanh(c * (x + 0.044715 * (x * x * x))))


def _rms_mod(x, gain, shift, scale):
    ms = jnp.mean(x * x, axis=-1, keepdims=True)
    return (x * lax.rsqrt(ms + EPS) * gain) * (1.0 + scale) + shift


def _split3(v):
    p1 = v.astype(BF16)
    r1 = v - p1.astype(F32)
    p2 = r1.astype(BF16)
    r2 = r1 - p2.astype(F32)
    return p1, p2, r2.astype(BF16)


def _tri_cumsum(tri, v):
    p1, p2, p3 = _split3(v)
    d = lambda p: jnp.dot(tri, p, preferred_element_type=F32)
    return d(p1) + d(p2) + d(p3)


def _mod_spec(d, n_lat_tiles):
    return pl.BlockSpec((None, None, 8, d), lambda i, t: (i, t // n_lat_tiles, 0, 0))


def _tri_matrix(n, rev):
    r = lax.broadcasted_iota(jnp.int32, (n, n), 0)
    c = lax.broadcasted_iota(jnp.int32, (n, n), 1)
    return jnp.where((c >= r) if rev else (c <= r), 1.0, 0.0).astype(BF16)


def _mod_kernel(s_ref, w_ref, b_ref, o_ref):
    s = _silu(s_ref[...])
    o_ref[0] = _dot(s, w_ref[0]) + b_ref[0]


def _modulation(cond, w_mod, b_mod):
    depth, d, n = w_mod.shape
    rows = cond.shape[0]
    tn = 1536
    return pl.pallas_call(
        _mod_kernel,
        grid=(depth, n // tn),
        in_specs=[pl.BlockSpec((rows, d), lambda l, j: (0, 0)),
                  pl.BlockSpec((1, d, tn), lambda l, j: (l, 0, j)),
                  pl.BlockSpec((1, 1, tn), lambda l, j: (l, 0, j))],
        out_specs=pl.BlockSpec((1, rows, tn), lambda l, j: (l, 0, j)),
        out_shape=jax.ShapeDtypeStruct((depth, rows, n), F32),
        compiler_params=_cparams(("parallel", "parallel")),
        name="modulation",
    )(cond, w_mod, b_mod.reshape(depth, 1, n))


def _inproj0_kernel(h_ref, m_ref, gain_ref, w_ref, qg_ref, kg_ref, qkv_ref, hraw_ref):
    u = _rms_mod(h_ref[0], gain_ref[...], m_ref[0:1, :], m_ref[1:2, :]).astype(BF16)
    lo = lax.broadcasted_iota(jnp.int32, (1, LANES), 1) < NA_HEAD_DIM
    na_w = qg_ref.shape[1]

    def head_norm(y, g_ref):
        outs = []
        for c in range(na_w // LANES):
            yc = y[:, c * LANES:(c + 1) * LANES]
            sq = yc * yc
            s_lo = jnp.sum(jnp.where(lo, sq, 0.0), axis=-1, keepdims=True)
            s_hi = jnp.sum(jnp.where(lo, 0.0, sq), axis=-1, keepdims=True)
            inv = jnp.where(lo, lax.rsqrt(s_lo / NA_HEAD_DIM + EPS), lax.rsqrt(s_hi / NA_HEAD_DIM + EPS))
            outs.append(yc * inv * g_ref[:, c * LANES:(c + 1) * LANES])
        return jnp.concatenate(outs, axis=-1)

    def proj(lo_col, width):
        return jnp.dot(u, w_ref[:, lo_col:lo_col + width], preferred_element_type=F32)

    qkv_ref[0, :, 0:na_w] = head_norm(proj(0, na_w), qg_ref).astype(BF16)
    qkv_ref[0, :, na_w:2 * na_w] = head_norm(proj(na_w, na_w), kg_ref).astype(BF16)
    qkv_ref[0, :, 2 * na_w:3 * na_w] = proj(2 * na_w, na_w).astype(BF16)
    n_hg = hraw_ref.shape[2]
    step = 512
    for j in range(n_hg // step):
        hraw_ref[0, :, j * step:(j + 1) * step] = proj(3 * na_w + j * step, step)


def _inproj0(hh, mod, gain, w, qg, kg):
    b, tt, d = hh.shape
    n = w.shape[1]
    na_w = qg.shape[1]
    n_hg = n - 3 * na_w
    nt = tt // TM
    n_lat_tiles = nt - 1
    return pl.pallas_call(
        _inproj0_kernel,
        grid=(b, nt),
        in_specs=[pl.BlockSpec((1, TM, d), lambda i, t: (i, t, 0)),
                  _mod_spec(d, n_lat_tiles),
                  pl.BlockSpec((1, d), lambda i, t: (0, 0)),
                  pl.BlockSpec((d, n), lambda i, t: (0, 0)),
                  pl.BlockSpec((1, na_w), lambda i, t: (0, 0)),
                  pl.BlockSpec((1, na_w), lambda i, t: (0, 0))],
        out_specs=[pl.BlockSpec((1, TM, 3 * na_w), lambda i, t: (i, t, 0)),
                   pl.BlockSpec((1, TM, n_hg), lambda i, t: (i, t, 0))],
        out_shape=[jax.ShapeDtypeStruct((b, tt, 3 * na_w), BF16),
                   jax.ShapeDtypeStruct((b, tt, n_hg), F32)],
        compiler_params=_cparams(("parallel", "parallel")),
        name="inproj0",
    )(hh, mod, gain, w, qg, kg)


def _na_kernel(q_ref, k_ref, v_ref, bias_ref, o_ref, *, t_lat, rows):
    tt = q_ref.shape[1]
    lane = lax.broadcasted_iota(jnp.int32, (1, LANES), 1)
    head_mask = [lane < NA_HEAD_DIM, lane >= NA_HEAD_DIM]
    kc = k_ref[0, t_lat:tt, :]
    vc = v_ref[0, t_lat:tt, :]
    _, ku, u0s, _, case_of_block = _na_block_plan(rows)
    n_loc = ku * GRID_W
    n_q = NA_ROW_BLOCK * GRID_W

    qc = q_ref[0, t_lat:tt, :]
    oc = None
    for hm in head_mask:
        s = _dot_nt(jnp.where(hm, qc, jnp.zeros_like(qc)), kc)
        p = jnp.exp(s - jnp.max(s, axis=-1, keepdims=True))
        o = _dot(p, vc) / jnp.sum(p, axis=-1, keepdims=True)
        oc = o if oc is None else jnp.where(hm, o, oc)
    o_ref[0, t_lat:tt, :] = oc.astype(o_ref.dtype)

    def block_body(i, carry):
        u0 = u0s[0]
        case = case_of_block[0]
        for bi in range(1, len(u0s)):
            u0 = jnp.where(i == bi, u0s[bi], u0)
            case = jnp.where(i == bi, case_of_block[bi], case)
        q_rows = pl.ds(pl.multiple_of(i * n_q, n_q), n_q)
        q_i = q_ref[0, q_rows, :]
        start = pl.multiple_of(u0 * GRID_W, GRID_W)
        k_loc = k_ref[0, pl.ds(start, n_loc), :]
        v_loc = v_ref[0, pl.ds(start, n_loc), :]
        out = None
        for hi, hm in enumerate(head_mask):
            qh = jnp.where(hm, q_i, jnp.zeros_like(q_i))
            s_loc = _dot_nt(qh, k_loc) + bias_ref[hi, case]
            s_ctx = _dot_nt(qh, kc)
            m = jnp.maximum(jnp.max(s_loc, axis=-1, keepdims=True), jnp.max(s_ctx, axis=-1, keepdims=True))
            p_loc = jnp.exp(s_loc - m)
            p_ctx = jnp.exp(s_ctx - m)
            den = jnp.sum(p_loc, axis=-1, keepdims=True) + jnp.sum(p_ctx, axis=-1, keepdims=True)
            o = (_dot(p_loc, v_loc) + _dot(p_ctx, vc)) / den
            out = o if out is None else jnp.where(hm, o, out)
        o_ref[0, q_rows, :] = out.astype(o_ref.dtype)
        return carry

    lax.fori_loop(0, rows // NA_ROW_BLOCK, block_body, 0)


def _na_block_plan(rows):
    kr = min(WIN_ROWS, rows)
    ku = min(kr + NA_ROW_BLOCK - 1, rows)
    u0s, patterns, case_of_block = [], [], []
    for i in range(rows // NA_ROW_BLOCK):
        u0 = int(np.clip(NA_ROW_BLOCK * i - kr // 2, 0, rows - ku))
        rel = []
        for a in range(NA_ROW_BLOCK):
            r = NA_ROW_BLOCK * i + a
            r0 = int(np.clip(r - kr // 2, 0, rows - kr))
            assert u0 <= r0 and r0 + kr <= u0 + ku
            rel.append((r0 - u0, u0 - r))
        u0s.append(u0)
        if tuple(rel) not in patterns:
            patterns.append(tuple(rel))
        case_of_block.append(patterns.index(tuple(rel)))
    return kr, ku, u0s, patterns, case_of_block


def _na_bias_table(rpb, rows):
    kr, ku, _, patterns, _ = _na_block_plan(rows)
    q = np.arange(GRID_W)
    kcol = np.arange(GRID_W)
    ws = np.clip(q - WIN_COLS // 2, 0, GRID_W - WIN_COLS)
    in_win = (kcol[None, :] >= ws[:, None]) & (kcol[None, :] < ws[:, None] + WIN_COLS)
    dc = np.clip(kcol[None, :] - q[:, None] + WIN_COLS - 1, 0, 2 * WIN_COLS - 2)
    u = np.arange(ku)
    dr = np.zeros((len(patterns), NA_ROW_BLOCK, ku), np.int32)
    row_ok = np.zeros((len(patterns), NA_ROW_BLOCK, ku), bool)
    for c, rel in enumerate(patterns):
        for a, (r0_rel, u0_minus_r) in enumerate(rel):
            row_ok[c, a] = (u >= r0_rel) & (u < r0_rel + kr)
            dr[c, a] = np.clip(u0_minus_r + u + WIN_ROWS - 1, 0, 2 * WIN_ROWS - 2)
    tbl = rpb.astype(F32)[:, dr][..., dc]
    ok = row_ok[None, :, :, :, None, None] & in_win[None, None, None, None]
    tbl = jnp.where(ok, tbl, MASK_NEG)
    h = rpb.shape[0]
    return tbl.transpose(0, 1, 2, 4, 3, 5).reshape(h, len(patterns), NA_ROW_BLOCK * GRID_W, ku * GRID_W)


def _na_attention(qkv, bias, t_lat):
    b, tt, w3 = qkv.shape
    na_w = w3 // 3
    ncol = na_w // LANES
    rows = t_lat // GRID_W
    bias_block = (2,) + bias.shape[1:]
    kern = functools.partial(_na_kernel, t_lat=t_lat, rows=rows)
    return pl.pallas_call(
        kern,
        grid=(b, ncol),
        in_specs=[pl.BlockSpec((1, tt, LANES), lambda i, p: (i, 0, p)),
                  pl.BlockSpec((1, tt, LANES), lambda i, p: (i, 0, ncol + p)),
                  pl.BlockSpec((1, tt, LANES), lambda i, p: (i, 0, 2 * ncol + p)),
                  pl.BlockSpec(bias_block, lambda i, p: (p, 0, 0, 0))],
        out_specs=pl.BlockSpec((1, tt, LANES), lambda i, p: (i, 0, p)),
        out_shape=jax.ShapeDtypeStruct((b, tt, na_w), BF16),
        compiler_params=_cparams(("parallel", "parallel")),
        name="na_attention",
    )(qkv, qkv, qkv, bias)


def _scan_block(i, nt, rev):
    if rev:
        return jnp.where(i == 0, nt - 1, nt - 1 - i)
    return jnp.where(i == 0, nt - 1, i - 1)


def _hgrn_kernel(q_ref, f_ref, v_ref, lb_ref, o_ref, st_ref, *, rev):
    n_heads = st_ref.shape[0]
    hd = HG_HEAD_DIM
    n_sub = CHUNK // SUB

    @pl.when(pl.program_id(1) == 0)
    def _():
        st_ref[...] = jnp.zeros_like(st_ref)

    tri = _tri_matrix(CHUNK, rev)
    row = lax.broadcasted_iota(jnp.int32, (CHUNK, CHUNK), 0)
    col = lax.broadcasted_iota(jnp.int32, (CHUNK, CHUNK), 1)
    causal = (col >= row) if rev else (col <= row)
    rowv = lax.broadcasted_iota(jnp.int32, (CHUNK, 1), 0)
    lb = lb_ref[0:1, :]
    one_m_lb = lb_ref[1:2, :]
    width = lb.shape[1]
    zero_row = jnp.zeros((1, width), F32)
    rows_of = lambda rws: jnp.concatenate([jnp.broadcast_to(r, (SUB, width)) for r in rws], axis=0)
    scan_blocks = list(range(n_sub - 1, -1, -1)) if rev else list(range(n_sub))

    n_chunks = q_ref.shape[1] // CHUNK
    order = range(n_chunks - 1, -1, -1) if rev else range(n_chunks)
    states = [st_ref[h] for h in range(n_heads)]
    for cc in order:
        rs = slice(cc * CHUNK, (cc + 1) * CHUNK)
        x = f_ref[0, rs, :]
        e = jnp.exp(-jnp.abs(x))
        r = 1.0 / (1.0 + e)
        x_pos = x >= 0.0
        log_f = jnp.log(jnp.where(x_pos, 1.0 + lb * e, lb + e) * r)
        k_all = one_m_lb * jnp.where(x_pos, e, 1.0) * r
        q_all = _silu(q_ref[0, rs, :])
        v_all = v_ref[0, rs, :]
        g_all = _tri_cumsum(tri, log_f)

        c = [g_all[j * SUB:j * SUB + 1, :] if rev else g_all[(j + 1) * SUB - 1:(j + 1) * SUB, :]
             for j in range(n_sub)]
        c_prev = [zero_row] * n_sub
        for before, after in zip(scan_blocks[:-1], scan_blocks[1:]):
            c_prev[after] = c[before]
        c_final = c[scan_blocks[-1]]
        rng = None
        for j in range(n_sub):
            d = c_prev[j] - c[j]
            rng = d if rng is None else jnp.maximum(rng, d)
        safe = jnp.max(rng) <= SAFE_RANGE

        q_b = q_all * jnp.exp(g_all - rows_of(c_prev))
        k_end = k_all * jnp.exp(rows_of(c) - g_all)
        q_g = q_b * rows_of([jnp.exp(cp) for cp in c_prev])
        k_last = k_end * rows_of([jnp.exp(c_final - cj) for cj in c])
        e_last = jnp.exp(c_final)
        k_comb = []
        for sb in range(n_sub):
            mult = []
            for j in range(n_sub):
                if j == sb:
                    mult.append(jnp.where(safe, jnp.exp(jnp.minimum(c_prev[sb] - c[sb], SAFE_RANGE)), 0.0))
                elif scan_blocks.index(j) < scan_blocks.index(sb):
                    mult.append(jnp.exp(c_prev[sb] - c[j]))
                else:
                    mult.append(zero_row)
            k_comb.append(k_end * rows_of(mult))

        heads = [slice(h * hd, (h + 1) * hd) for h in range(n_heads)]
        att = []
        for hs in heads:
            blocks = [_dot_nt(q_b[sb * SUB:(sb + 1) * SUB, hs], k_comb[sb][:, hs]) for sb in range(n_sub)]
            att.append(jnp.where(causal, jnp.concatenate(blocks, axis=0), 0.0))

        def slow_diag(q_all=q_all, k_all=k_all, g_all=g_all):
            pos = rowv % SUB
            accs = []
            for hs in heads:
                q, k, g = q_all[:, hs], k_all[:, hs], g_all[:, hs]
                acc = jnp.zeros((CHUNK, CHUNK), F32)
                for dlt in range(SUB):
                    shift = (CHUNK - dlt) % CHUNK if rev else dlt
                    k_d = pltpu.roll(k, shift, 0) if shift else k
                    g_d = pltpu.roll(g, shift, 0) if shift else g
                    valid = (pos + dlt <= SUB - 1) if rev else (pos >= dlt)
                    e = jnp.exp(jnp.where(valid, g - g_d, 0.0))
                    val = jnp.sum(q * k_d * e, axis=-1, keepdims=True)
                    partner = (row + dlt) if rev else (row - dlt)
                    acc = acc + jnp.where((col == partner) & valid, val, 0.0)
                accs.append(acc)
            return jnp.stack(accs, axis=0)

        diag = lax.cond(safe, lambda: jnp.zeros((n_heads, CHUNK, CHUNK), F32), slow_diag)
        outs = []
        for h, hs in enumerate(heads):
            a = att[h] + diag[h]
            st = states[h]
            o = _dot(a, v_all[:, hs]) + _dot_nt(q_g[:, hs], st)
            states[h] = st * e_last[:, hs] + lax.dot_general(
                v_all[:, hs].astype(BF16), k_last[:, hs].astype(BF16), (((0,), (0,)), ((), ())),
                preferred_element_type=F32)
            outs.append(o)
        o_ref[0, rs, :] = jnp.concatenate(outs, axis=-1)
    for h in range(n_heads):
        st_ref[h] = states[h]


def _hgrn_scan(hraw, lb_rows, f_col, rev):
    b, tt, n = hraw.shape
    w = lb_rows.shape[1]
    nt = tt // TM
    n_heads = w // HG_HEAD_DIM
    blk = lambda i: _scan_block(i, nt, rev)
    return pl.pallas_call(
        functools.partial(_hgrn_kernel, rev=rev),
        grid=(b, nt),
        in_specs=[pl.BlockSpec((1, TM, w), lambda bi, i: (bi, blk(i), 0)),
                  pl.BlockSpec((1, TM, w), lambda bi, i: (bi, blk(i), f_col)),
                  pl.BlockSpec((1, TM, w), lambda bi, i: (bi, blk(i), 3)),
                  pl.BlockSpec((8, w), lambda bi, i: (0, 0))],
        out_specs=pl.BlockSpec((1, TM, w), lambda bi, i: (bi, blk(i), 0)),
        out_shape=jax.ShapeDtypeStruct((b, tt, w), F32),
        scratch_shapes=[pltpu.VMEM((n_heads, HG_HEAD_DIM, HG_HEAD_DIM), F32)],
        compiler_params=_cparams(("parallel", "arbitrary")),
        name="hgrn_bwd" if rev else "hgrn_fwd",
    )(hraw, hraw, hraw, lb_rows)


def _outproj0_kernel(na_ref, of_ref, ob_ref, gate_ref, h_ref, m_ref, gain_ref, w_ref, o_ref):
    na_w = na_ref.shape[2]
    o = of_ref[0] + ob_ref[0]
    gate = _silu(gate_ref[0])
    ys = []
    for h in range(o.shape[1] // HG_HEAD_DIM):
        hs = slice(h * HG_HEAD_DIM, (h + 1) * HG_HEAD_DIM)
        oh = o[:, hs]
        ms = jnp.mean(oh * oh, axis=-1, keepdims=True)
        ys.append(oh * lax.rsqrt(ms + EPS) * gain_ref[:, hs] * gate[:, hs])
    y_hg = jnp.concatenate(ys, axis=-1)
    y = jnp.dot(na_ref[0], w_ref[0:na_w, :], preferred_element_type=F32) + _dot(y_hg, w_ref[na_w:, :])
    o_ref[0] = h_ref[0] + m_ref[2:3, :] * y


def _outproj0(o_na, o_f, o_b, hraw, hh, mod, gain, w):
    b, tt, d = hh.shape
    na_w = o_na.shape[2]
    hw = o_f.shape[2]
    nt = tt // TM
    n_lat_tiles = nt - 1
    tile = lambda wd: pl.BlockSpec((1, TM, wd), lambda i, t: (i, t, 0))
    return pl.pallas_call(
        _outproj0_kernel,
        grid=(b, nt),
        in_specs=[tile(na_w), tile(hw), tile(hw),
                  pl.BlockSpec((1, TM, hw), lambda i, t: (i, t, 4)),
                  tile(d),
                  _mod_spec(d, n_lat_tiles),
                  pl.BlockSpec((1, hw), lambda i, t: (0, 0)),
                  pl.BlockSpec(w.shape, lambda i, t: (0, 0))],
        out_specs=tile(d),
        out_shape=jax.ShapeDtypeStruct((b, tt, d), F32),
        compiler_params=_cparams(("parallel", "parallel")),
        name="outproj0",
    )(o_na, o_f, o_b, hraw, hh, mod, gain, w)


HALO = 8


def _halo_specs(width, tt):
    r8 = TM // HALO
    last = tt // HALO - 1
    return [pl.BlockSpec((1, HALO, width), lambda i, t: (i, jnp.maximum(t * r8 - 1, 0), 0)),
            pl.BlockSpec((1, HALO, width), lambda i, t: (i, jnp.minimum((t + 1) * r8, last), 0))]


def _modulated_with_halo(h_ref, hp_ref, hn_ref, gain, shift, scale):
    u = _rms_mod(h_ref[0], gain, shift, scale)
    u_ext = jnp.concatenate([_rms_mod(hp_ref[0], gain, shift, scale), u,
                             _rms_mod(hn_ref[0], gain, shift, scale)], axis=0)
    return u.astype(BF16), u_ext.astype(BF16)


def _conv3_ext(a_ext, cw, cb, n_lat_tiles):
    n = a_ext.shape[0] - 2 * HALO
    t = pl.program_id(1)
    has_prev = jnp.logical_and(t != 0, t != n_lat_tiles)
    has_next = jnp.logical_and(t != n_lat_tiles - 1, t != n_lat_tiles)
    a = a_ext[HALO:HALO + n]
    prev_row = jnp.where(has_prev, a_ext[HALO - 1:HALO], 0.0)
    next_row = jnp.where(has_next, a_ext[HALO + n:HALO + n + 1], 0.0)
    row = lax.broadcasted_iota(jnp.int32, (n, 1), 0)
    up = jnp.where(row == 0, prev_row, pltpu.roll(a, 1, 0))
    dn = jnp.where(row == n - 1, next_row, pltpu.roll(a, n - 1, 0))
    return cw[0:1, :] * up + cw[1:2, :] * a + cw[2:3, :] * dn + cb


def _col_chunks(width, step):
    return [(lo, min(lo + step, width)) for lo in range(0, width, step)]


def _ffn_up_kernel(h_ref, hp_ref, hn_ref, m_ref, gain_ref, w_ref, cw_ref, cb_ref, mid_ref, *, n_lat_tiles):
    u, u_ext = _modulated_with_halo(h_ref, hp_ref, hn_ref, gain_ref[...], m_ref[3:4, :], m_ref[4:5, :])
    dff = mid_ref.shape[2]
    for lo, hi in _col_chunks(dff, 768):
        a_ext = jnp.dot(u_ext, w_ref[:, lo:hi], preferred_element_type=F32)
        v = jnp.dot(u, w_ref[:, dff + lo:dff + hi], preferred_element_type=F32)
        c = _conv3_ext(a_ext, cw_ref[:, lo:hi], cb_ref[:, lo:hi], n_lat_tiles)
        mid_ref[0, :, lo:hi] = (_gelu_tanh(c) * v).astype(mid_ref.dtype)


def _ffn_up(hh, mod, gain, w, conv_w, conv_b, n_tiles, n_lat_tiles):
    b, tt_in, d = hh.shape
    dff = w.shape[1] // 2
    return pl.pallas_call(
        functools.partial(_ffn_up_kernel, n_lat_tiles=n_lat_tiles),
        grid=(b, n_tiles),
        in_specs=[pl.BlockSpec((1, TM, d), lambda i, t: (i, t, 0))] + _halo_specs(d, tt_in) + [
            _mod_spec(d, n_lat_tiles),
            pl.BlockSpec((1, d), lambda i, t: (0, 0)),
            pl.BlockSpec(w.shape, lambda i, t: (0, 0)),
            pl.BlockSpec((3, dff), lambda i, t: (0, 0)),
            pl.BlockSpec((1, dff), lambda i, t: (0, 0))],
        out_specs=pl.BlockSpec((1, TM, dff), lambda i, t: (i, t, 0)),
        out_shape=jax.ShapeDtypeStruct((b, n_tiles * TM, dff), BF16),
        compiler_params=_cparams(("parallel", "parallel")),
        name="ffn_up",
    )(hh, hh, hh, mod, gain, w, conv_w, conv_b)


def _ffn_down_kernel(mid_ref, w_ref, h_ref, m_ref, o_ref):
    y = jnp.dot(mid_ref[0], w_ref[...], preferred_element_type=F32)
    o_ref[0] = h_ref[0] + m_ref[5:6, :] * y


def _ffn_down(mid, w, hh, mod, n_lat_tiles):
    b, tt, dff = mid.shape
    d = hh.shape[2]
    tile = lambda wd: pl.BlockSpec((1, TM, wd), lambda i, t: (i, t, 0))
    return pl.pallas_call(
        _ffn_down_kernel,
        grid=(b, tt // TM),
        in_specs=[tile(dff), pl.BlockSpec(w.shape, lambda i, t: (0, 0)), tile(d), _mod_spec(d, n_lat_tiles)],
        out_specs=tile(d),
        out_shape=jax.ShapeDtypeStruct((b, tt, d), F32),
        compiler_params=_cparams(("parallel", "parallel")),
        name="ffn_down",
    )(mid, w, hh, mod)


def _inproj1_kernel(h_ref, hp_ref, hn_ref, m_ref, gain_ref, w_ref, cw_ref, cb_ref, dtb_ref, a_ref,
                    z_ref, xs_ref, bc_ref, dtp_ref, *, n_lat_tiles):
    u, u_ext = _modulated_with_halo(h_ref, hp_ref, hn_ref, gain_ref[...], m_ref[0:1, :], m_ref[1:2, :])
    nz = z_ref.shape[2]
    nxs = xs_ref.shape[2]
    nx = nxs + bc_ref.shape[2]
    step = 512
    for lo, hi in _col_chunks(nz, step):
        z_ref[0, :, lo:hi] = jnp.dot(u, w_ref[:, lo:hi], preferred_element_type=F32).astype(z_ref.dtype)
    for lo, hi in _col_chunks(nx, step):
        xbc_ext = jnp.dot(u_ext, w_ref[:, nz + lo:nz + hi], preferred_element_type=F32)
        c = _silu(_conv3_ext(xbc_ext, cw_ref[:, lo:hi], cb_ref[:, lo:hi], n_lat_tiles))
        if hi <= nxs:
            xs_ref[0, :, lo:hi] = c.astype(xs_ref.dtype)
        else:
            bc_ref[0, :, lo - nxs:hi - nxs] = c.astype(bc_ref.dtype)
    raw = jnp.dot(u, w_ref[:, nz + nx:], preferred_element_type=F32)
    dt = _softplus(raw + dtb_ref[...])
    dtp_ref[0, :, 0:2 * LANES] = dt
    dtp_ref[0, :, 2 * LANES:4 * LANES] = dt * a_ref[...]


def _inproj1(hh, mod, gain, w, conv_w, conv_b, dt_bias, a_neg, nz, nxs):
    b, tt, d = hh.shape
    nt = tt // TM
    n_lat_tiles = nt - 1
    nx = conv_w.shape[1]
    assert nxs % 512 == 0
    tile = lambda wd: pl.BlockSpec((1, TM, wd), lambda i, t: (i, t, 0))
    return pl.pallas_call(
        functools.partial(_inproj1_kernel, n_lat_tiles=n_lat_tiles),
        grid=(b, nt),
        in_specs=[tile(d)] + _halo_specs(d, tt) + [
            _mod_spec(d, n_lat_tiles),
            pl.BlockSpec((1, d), lambda i, t: (0, 0)),
            pl.BlockSpec(w.shape, lambda i, t: (0, 0)),
            pl.BlockSpec((3, nx), lambda i, t: (0, 0)),
            pl.BlockSpec((1, nx), lambda i, t: (0, 0)),
            pl.BlockSpec((1, 2 * LANES), lambda i, t: (0, 0)),
            pl.BlockSpec((1, 2 * LANES), lambda i, t: (0, 0))],
        out_specs=[tile(nz), tile(nxs), tile(nx - nxs), tile(4 * LANES)],
        out_shape=[jax.ShapeDtypeStruct((b, tt, nz), BF16),
                   jax.ShapeDtypeStruct((b, tt, nxs), BF16),
                   jax.ShapeDtypeStruct((b, tt, nx - nxs), BF16),
                   jax.ShapeDtypeStruct((b, tt, 4 * LANES), F32)],
        compiler_params=_cparams(("parallel", "parallel")),
        name="inproj1",
    )(hh, hh, hh, mod, gain, w, conv_w, conv_b, dt_bias, a_neg)


def _ssd_kernel(x_ref, bc_ref, dt_ref, la_ref, e_ref, o_ref, s_ref, *, rev):
    n_groups = s_ref.shape[0]
    ns = SSD_STATE
    gw = s_ref.shape[2]
    width = x_ref.shape[2]
    n_heads = width // SSD_HEAD_DIM
    blk = 256

    @pl.when(pl.program_id(1) == 0)
    def _():
        s_ref[...] = jnp.zeros_like(s_ref)

    tri = _tri_matrix(CHUNK, rev)
    lane128 = lax.broadcasted_iota(jnp.int32, (1, LANES), 1)
    rowc = lax.broadcasted_iota(jnp.int32, (CHUNK, width), 0)
    pos = lax.broadcasted_iota(jnp.int32, (CHUNK, width), 1) % SSD_HEAD_DIM
    on_diag = pos == rowc
    causal = (pos >= rowc) if rev else (pos <= rowc)
    blk_head = lax.broadcasted_iota(jnp.int32, (1, blk), 1) // SSD_HEAD_DIM
    last = 0 if rev else CHUNK - 1

    def expand(v):
        p1, p2, p3 = _split3(v)
        zero = jnp.zeros_like(p1)
        st = jnp.where(lane128 < n_heads, p1,
                       jnp.where(lane128 < 2 * n_heads, p2, jnp.where(lane128 < 3 * n_heads, p3, zero)))
        return jnp.dot(st, e_ref[...], preferred_element_type=F32)

    n_chunks = x_ref.shape[1] // CHUNK
    order = range(n_chunks - 1, -1, -1) if rev else range(n_chunks)
    states = [s_ref[g] for g in range(n_groups)]
    for cc in order:
        rs = slice(cc * CHUNK, (cc + 1) * CHUNK)
        x = x_ref[0, rs, :].astype(F32)
        bc = bc_ref[0, rs, :]
        cum = expand(_tri_cumsum(tri, la_ref[0, rs, :]))
        dt = expand(dt_ref[0, rs, :])
        cum_row = jnp.sum(jnp.where(on_diag, cum, 0.0), axis=0, keepdims=True)
        cum_last = cum[last:last + 1, :]
        m_all = jnp.exp(jnp.where(causal, cum - cum_row, MASK_NEG))
        e_cum = jnp.exp(cum)
        x_dt = x * dt
        xw_all = (x_dt * jnp.exp(cum_last - cum)).astype(BF16)
        x_dt = x_dt.astype(BF16)
        e_last = jnp.exp(cum_last)
        ys = []
        for g in range(n_groups):
            gs = slice(g * gw, (g + 1) * gw)
            b_g = bc[:, g * ns:(g + 1) * ns]
            c_g = bc[:, (n_groups + g) * ns:(n_groups + g + 1) * ns]
            b_rep = jnp.concatenate([b_g] * (gw // CHUNK), axis=0)
            m_g = (m_all[:, gs] * _dot_nt(c_g, b_rep)).astype(BF16)
            x_g = x_dt[:, gs]
            y_parts = []
            for j in range(gw // blk):
                x4 = x_g[:, j * blk:(j + 1) * blk]
                x_bd = jnp.concatenate(
                    [jnp.where(blk_head == hh, x4, jnp.zeros_like(x4)) for hh in range(blk // SSD_HEAD_DIM)], axis=0)
                y_parts.append(jnp.dot(m_g[:, j * blk:(j + 1) * blk], x_bd, preferred_element_type=F32))
            s_g = states[g]
            y_g = jnp.concatenate(y_parts, axis=-1) + _dot(c_g, s_g) * e_cum[:, gs]
            states[g] = s_g * e_last[:, gs] + lax.dot_general(
                b_g, xw_all[:, gs], (((0,), (0,)), ((), ())), preferred_element_type=F32)
            ys.append(y_g)
        o_ref[0, rs, :] = jnp.concatenate(ys, axis=-1)
    for g in range(n_groups):
        s_ref[g] = states[g]


def _ssd_scan(xs, bc, dtp, expand_mat, rev):
    b, tt, width = xs.shape
    nt = tt // TM
    nbc = bc.shape[2]
    gw = width // SSD_GROUPS
    blk = lambda i: _scan_block(i, nt, rev)
    d_col = 1 if rev else 0
    return pl.pallas_call(
        functools.partial(_ssd_kernel, rev=rev),
        grid=(b, nt),
        in_specs=[pl.BlockSpec((1, TM, width), lambda bi, i: (bi, blk(i), 0)),
                  pl.BlockSpec((1, TM, nbc), lambda bi, i: (bi, blk(i), 0)),
                  pl.BlockSpec((1, TM, LANES), lambda bi, i: (bi, blk(i), d_col)),
                  pl.BlockSpec((1, TM, LANES), lambda bi, i: (bi, blk(i), 2 + d_col)),
                  pl.BlockSpec(expand_mat.shape, lambda bi, i: (0, 0))],
        out_specs=pl.BlockSpec((1, TM, width), lambda bi, i: (bi, blk(i), 0)),
        out_shape=jax.ShapeDtypeStruct((b, tt, width), F32),
        scratch_shapes=[pltpu.VMEM((SSD_GROUPS, SSD_STATE, gw), F32)],
        compiler_params=_cparams(("parallel", "arbitrary")),
        name="ssd_bwd" if rev else "ssd_fwd",
    )(xs, bc, dtp, dtp, expand_mat)


def _outproj1_kernel(yf_ref, yb_ref, xs_ref, z_ref, h_ref, m_ref, d_ref, gain_ref, w_ref, o_ref):
    yy = (yf_ref[0] + yb_ref[0] + d_ref[...] * xs_ref[0].astype(F32)) * _silu(z_ref[0].astype(F32))
    width = yy.shape[1]
    gw = width // SSD_GROUPS
    acc = None
    for g in range(SSD_GROUPS):
        gs = slice(g * gw, (g + 1) * gw)
        yg = yy[:, gs]
        ms = jnp.mean(yg * yg, axis=-1, keepdims=True)
        part = _dot(yg * lax.rsqrt(ms + EPS) * gain_ref[:, gs], w_ref[gs, :])
        acc = part if acc is None else acc + part
    o_ref[0] = h_ref[0] + m_ref[2:3, :] * acc


def _outproj1(y_f, y_b, xs, z, hh, mod, d_skip, gain, w, n_lat_tiles):
    b, _, d = hh.shape
    width = xs.shape[2]
    tile = lambda wd: pl.BlockSpec((1, TM, wd), lambda i, t: (i, t, 0))
    return pl.pallas_call(
        _outproj1_kernel,
        grid=(b, n_lat_tiles),
        in_specs=[tile(width), tile(width), tile(width), tile(width), tile(d),
                  _mod_spec(d, n_lat_tiles),
                  pl.BlockSpec((1, width), lambda i, t: (0, 0)),
                  pl.BlockSpec((1, width), lambda i, t: (0, 0)),
                  pl.BlockSpec(w.shape, lambda i, t: (0, 0))],
        out_specs=tile(d),
        out_shape=jax.ShapeDtypeStruct((b, n_lat_tiles * TM, d), F32),
        compiler_params=_cparams(("parallel", "parallel")),
        name="outproj1",
    )(y_f, y_b, xs, z, hh, mod, d_skip, gain, w)


def _mod_rows(mods_l, b, d):
    six = mods_l.reshape(mods_l.shape[0], 6, d)
    lat = six[:b]
    ctx = jnp.broadcast_to(six[b:b + 1], (b, 6, d))
    both = jnp.stack([lat, ctx], axis=1)
    return jnp.pad(both, ((0, 0), (0, 0), (0, 2), (0, 0)))


def kernel(x, c, ctx, c_ctx, w_mod, b_mod, norm_mix, norm_ffn, ffn_w_up, ffn_conv_w, ffn_conv_b, ffn_w_down,
           hy_w_in, hy_w_out, na_q_gain, na_k_gain, na_rpb, hg_out_gain, hg_lb_fwd, hg_lb_bwd, ssd_w_in,
           ssd_conv_w, ssd_conv_b, ssd_dt_bias_fwd, ssd_dt_bias_bwd, ssd_a_log_fwd, ssd_a_log_bwd, ssd_d,
           ssd_norm_gain, ssd_w_out):
    b, t_lat, d = x.shape
    l_ctx = ctx.shape[1]
    assert l_ctx == TM and t_lat % TM == 0 and w_mod.shape[0] == 2
    n_lat_tiles = t_lat // TM
    nt = n_lat_tiles + 1
    hh = jnp.concatenate([x, ctx], axis=1)

    rows = ((b + 1 + 7) // 8) * 8
    cond = jnp.concatenate([c, c_ctx[None], jnp.zeros((rows - b - 1, d), F32)], axis=0)
    mods = _modulation(cond, w_mod, b_mod)
    mod0 = _mod_rows(mods[0], b, d)
    mod1 = _mod_rows(mods[1], b, d)
    row = lambda v: v.reshape(1, -1).astype(F32)

    hg_w = hg_lb_fwd.shape[1]
    na_w = hy_w_out.shape[1] - hg_w
    n_na_heads = na_w // NA_HEAD_DIM
    qg = row(jnp.tile(na_q_gain[0], n_na_heads)) * (NA_HEAD_DIM ** -0.5)
    kg = row(jnp.tile(na_k_gain[0], n_na_heads))
    qkv, hraw = _inproj0(hh, mod0, row(norm_mix[0]), hy_w_in[0].astype(BF16), qg, kg)
    bias = _na_bias_table(na_rpb[0], t_lat // GRID_W)
    o_na = _na_attention(qkv, bias, t_lat)

    def lb_rows(lb_param):
        lb = jnp.cumsum(jax.nn.softmax(lb_param.astype(F32), axis=0), axis=0)[0]
        return jnp.pad(jnp.stack([lb, 1.0 - lb], axis=0), ((0, 6), (0, 0)))

    o_f = _hgrn_scan(hraw, lb_rows(hg_lb_fwd), 1, rev=False)
    o_b = _hgrn_scan(hraw, lb_rows(hg_lb_bwd), 2, rev=True)
    hg_gain = row(jnp.tile(hg_out_gain[0], hg_w // HG_HEAD_DIM))
    h1 = _outproj0(o_na, o_f, o_b, hraw, hh, mod0, hg_gain, hy_w_out[0].astype(BF16))

    mid0 = _ffn_up(h1, mod0, row(norm_ffn[0]), ffn_w_up[0].astype(BF16), ffn_conv_w[0], row(ffn_conv_b[0]),
                   nt, n_lat_tiles)
    h2 = _ffn_down(mid0, ffn_w_down[0].astype(BF16), h1, mod0, n_lat_tiles)

    n_heads = ssd_d.shape[1]
    inner = ssd_w_out.shape[1]
    nxbc = ssd_conv_w.shape[2]
    w1 = ssd_w_in[0]
    rep = LANES // n_heads
    w_dtf = jnp.tile(w1[:, inner + nxbc:inner + nxbc + n_heads], (1, rep))
    w_dtb = jnp.tile(w1[:, inner + nxbc + n_heads:], (1, rep))
    w1p = jnp.concatenate([w1[:, :inner + nxbc], w_dtf, w_dtb], axis=1).astype(BF16)
    dt_bias = row(jnp.concatenate([jnp.tile(ssd_dt_bias_fwd[0], rep), jnp.tile(ssd_dt_bias_bwd[0], rep)]))
    a_neg = row(jnp.concatenate([jnp.tile(-jnp.exp(ssd_a_log_fwd[0].astype(F32)), rep),
                                 jnp.tile(-jnp.exp(ssd_a_log_bwd[0].astype(F32)), rep)]))
    z, xs, bc, dtp = _inproj1(h2, mod1, row(norm_mix[1]), w1p, ssd_conv_w[0], row(ssd_conv_b[0]), dt_bias,
                              a_neg, inner, inner)

    lane = np.arange(LANES)[:, None]
    colh = (np.arange(inner) // SSD_HEAD_DIM)[None, :]
    expand_mat = jnp.asarray(((lane % n_heads == colh) & (lane < 3 * n_heads)).astype(np.float32), dtype=BF16)
    y_f = _ssd_scan(xs, bc, dtp, expand_mat, rev=False)
    y_b = _ssd_scan(xs, bc, dtp, expand_mat, rev=True)
    d_skip = row(jnp.repeat(ssd_d[0], SSD_HEAD_DIM))
    h3 = _outproj1(y_f, y_b, xs, z, h2, mod1, d_skip, row(ssd_norm_gain[0]), ssd_w_out[0].astype(BF16),
                   n_lat_tiles)

    mid1 = _ffn_up(h3, mod1, row(norm_ffn[1]), ffn_w_up[1].astype(BF16), ffn_conv_w[1], row(ffn_conv_b[1]),
                   n_lat_tiles, n_lat_tiles)
    return _ffn_down(mid1, ffn_w_down[1].astype(BF16), h3, mod1, n_lat_tiles)
```

```python
import functools
import math

import numpy as np
import jax
import jax.numpy as jnp
from jax import lax
from jax.experimental import pallas as pl
from jax.experimental.pallas import tpu as pltpu

F32 = jnp.float32
BF16 = jnp.bfloat16
EPS = 1e-6

GRID_W = 64
NA_HEAD_DIM = 64
WIN_ROWS = 8
WIN_COLS = 16
NA_ROW_BLOCK = 4
HG_HEAD_DIM = 128
SSD_HEAD_DIM = 64
SSD_GROUPS = 4
SSD_STATE = 128
CHUNK = 64
SUB = 16
TM = 256
BB = 2
LANES = 128
MASK_NEG = -1e30
SAFE_RANGE = 60.0

VMEM_LIMIT = 56 * 1024 * 1024


def _cparams(sem):
    return pltpu.CompilerParams(dimension_semantics=sem, vmem_limit_bytes=VMEM_LIMIT)


def _dot(a, b):
    return jnp.dot(a.astype(BF16), b.astype(BF16), preferred_element_type=F32)


def _dot_nt(a, b):
    return lax.dot_general(a.astype(BF16), b.astype(BF16), (((1,), (1,)), ((), ())),
                           preferred_element_type=F32)


def _sigmoid(x):
    return 1.0 / (1.0 + jnp.exp(-x))


def _silu(x):
    return x * _sigmoid(x)


def _softplus(x):
    return jnp.maximum(x, 0.0) + jnp.log1p(jnp.exp(-jnp.abs(x)))


def _gelu_tanh(x):
    c = math.sqrt(2.0 / math.pi)
    return 0.5 * x * (1.0 + jnp.tanh(c * (x + 0.044715 * (x * x * x))))


def _rms_mod(x, gain, shift, scale):
    ms = jnp.mean(x * x, axis=-1, keepdims=True)
    return (x * lax.rsqrt(ms + EPS) * gain) * (1.0 + scale) + shift


def _split3(v):
    p1 = v.astype(BF16)
    r1 = v - p1.astype(F32)
    p2 = r1.astype(BF16)
    r2 = r1 - p2.astype(F32)
    return p1, p2, r2.astype(BF16)


def _tri_cumsum(tri, v):
    p1, p2, p3 = _split3(v)
    d = lambda p: jnp.dot(tri, p, preferred_element_type=F32)
    return d(p1) + d(p2) + d(p3)


def _mod_spec(d, n_lat_tiles):
    return pl.BlockSpec((BB, None, 8, d), lambda i, t: (i, t // n_lat_tiles, 0, 0))


def _tile_spec(width):
    return pl.BlockSpec((BB, TM, width), lambda i, t: (i, t, 0))


def _const_spec(shape):
    return pl.BlockSpec(shape, lambda i, t: (0,) * len(shape))


def _stream0_specs(d, n_lat_tiles):
    return [pl.BlockSpec((BB, TM, d), lambda i, t: (i, jnp.minimum(t, n_lat_tiles - 1), 0)),
            pl.BlockSpec((BB, TM, d), lambda i, t: (i, 0, 0))]


def _stream0_rows(x_ref, c_ref, bi, n_lat_tiles):
    return jnp.where(pl.program_id(1) == n_lat_tiles, c_ref[bi], x_ref[bi])


def _col_chunks(width, step):
    return [(lo, min(lo + step, width)) for lo in range(0, width, step)]


def _tri_matrix(n, rev):
    r = lax.broadcasted_iota(jnp.int32, (n, n), 0)
    c = lax.broadcasted_iota(jnp.int32, (n, n), 1)
    return jnp.where((c >= r) if rev else (c <= r), 1.0, 0.0).astype(BF16)


def _mod_kernel(s_ref, w_ref, b_ref, o_ref):
    s = _silu(s_ref[...])
    o_ref[0] = _dot(s, w_ref[0]) + b_ref[0]


def _modulation(cond, w_mod, b_mod):
    depth, d, n = w_mod.shape
    rows = cond.shape[0]
    tn = 1536
    return pl.pallas_call(
        _mod_kernel,
        grid=(depth, n // tn),
        in_specs=[pl.BlockSpec((rows, d), lambda l, j: (0, 0)),
                  pl.BlockSpec((1, d, tn), lambda l, j: (l, 0, j)),
                  pl.BlockSpec((1, 1, tn), lambda l, j: (l, 0, j))],
        out_specs=pl.BlockSpec((1, rows, tn), lambda l, j: (l, 0, j)),
        out_shape=jax.ShapeDtypeStruct((depth, rows, n), F32),
        compiler_params=_cparams(("parallel", "parallel")),
        name="modulation",
    )(cond, w_mod, b_mod.reshape(depth, 1, n))


def _inproj0_kernel(x_ref, c_ref, m_ref, gain_ref, w_ref, qg_ref, kg_ref, qkv_ref, hraw_ref, *, n_lat_tiles):
    u = jnp.concatenate(
        [_rms_mod(_stream0_rows(x_ref, c_ref, bi, n_lat_tiles), gain_ref[...], m_ref[bi, 0:1, :],
                  m_ref[bi, 1:2, :]).astype(BF16) for bi in range(BB)], axis=0)
    lo = lax.broadcasted_iota(jnp.int32, (1, LANES), 1) < NA_HEAD_DIM
    na_w = qg_ref.shape[1]

    def head_norm(y, g_ref):
        outs = []
        for c in range(na_w // LANES):
            yc = y[:, c * LANES:(c + 1) * LANES]
            sq = yc * yc
            s_lo = jnp.sum(jnp.where(lo, sq, 0.0), axis=-1, keepdims=True)
            s_hi = jnp.sum(jnp.where(lo, 0.0, sq), axis=-1, keepdims=True)
            inv = jnp.where(lo, lax.rsqrt(s_lo / NA_HEAD_DIM + EPS), lax.rsqrt(s_hi / NA_HEAD_DIM + EPS))
            outs.append(yc * inv * g_ref[:, c * LANES:(c + 1) * LANES])
        return jnp.concatenate(outs, axis=-1)

    def proj(lo_col, width):
        return jnp.dot(u, w_ref[:, lo_col:lo_col + width], preferred_element_type=F32)

    def store(ref, lo_col, y):
        for bi in range(BB):
            ref[bi, :, lo_col:lo_col + y.shape[1]] = y[bi * TM:(bi + 1) * TM].astype(ref.dtype)

    store(qkv_ref, 0, head_norm(proj(0, na_w), qg_ref))
    store(qkv_ref, na_w, head_norm(proj(na_w, na_w), kg_ref))
    store(qkv_ref, 2 * na_w, proj(2 * na_w, na_w))
    for lo_col, hi_col in _col_chunks(hraw_ref.shape[2], 512):
        store(hraw_ref, lo_col, proj(3 * na_w + lo_col, hi_col - lo_col))


def _inproj0(x, ctx, mod, gain, w, qg, kg):
    b, t_lat, d = x.shape
    n = w.shape[1]
    na_w = qg.shape[1]
    n_hg = n - 3 * na_w
    n_lat_tiles = t_lat // TM
    tt = t_lat + ctx.shape[1]
    return pl.pallas_call(
        functools.partial(_inproj0_kernel, n_lat_tiles=n_lat_tiles),
        grid=(b // BB, n_lat_tiles + 1),
        in_specs=_stream0_specs(d, n_lat_tiles) + [
            _mod_spec(d, n_lat_tiles), _const_spec((1, d)), _const_spec((d, n)),
            _const_spec((1, na_w)), _const_spec((1, na_w))],
        out_specs=[_tile_spec(3 * na_w), _tile_spec(n_hg)],
        out_shape=[jax.ShapeDtypeStruct((b, tt, 3 * na_w), BF16),
                   jax.ShapeDtypeStruct((b, tt, n_hg), F32)],
        compiler_params=_cparams(("parallel", "parallel")),
        name="inproj0",
    )(x, ctx, mod, gain, w, qg, kg)


def _na_kernel(q_ref, k_ref, v_ref, bias_ref, o_ref, *, t_lat, rows):
    tt = q_ref.shape[1]
    lane = lax.broadcasted_iota(jnp.int32, (1, LANES), 1)
    head_mask = [lane < NA_HEAD_DIM, lane >= NA_HEAD_DIM]
    kc = k_ref[0, t_lat:tt, :]
    vc = v_ref[0, t_lat:tt, :]
    _, ku, u0s, _, case_of_block = _na_block_plan(rows)
    n_loc = ku * GRID_W
    n_q = NA_ROW_BLOCK * GRID_W

    qc = q_ref[0, t_lat:tt, :]
    oc = None
    for hm in head_mask:
        s = _dot_nt(jnp.where(hm, qc, jnp.zeros_like(qc)), kc)
        p = jnp.exp(s - jnp.max(s, axis=-1, keepdims=True))
        o = _dot(p, vc) / jnp.sum(p, axis=-1, keepdims=True)
        oc = o if oc is None else jnp.where(hm, o, oc)
    o_ref[0, t_lat:tt, :] = oc.astype(o_ref.dtype)

    def block_body(i, carry):
        u0 = u0s[0]
        case = case_of_block[0]
        for bi in range(1, len(u0s)):
            u0 = jnp.where(i == bi, u0s[bi], u0)
            case = jnp.where(i == bi, case_of_block[bi], case)
        q_rows = pl.ds(pl.multiple_of(i * n_q, n_q), n_q)
        q_i = q_ref[0, q_rows, :]
        start = pl.multiple_of(u0 * GRID_W, GRID_W)
        k_loc = k_ref[0, pl.ds(start, n_loc), :]
        v_loc = v_ref[0, pl.ds(start, n_loc), :]
        out = None
        for hi, hm in enumerate(head_mask):
            qh = jnp.where(hm, q_i, jnp.zeros_like(q_i))
            s_loc = _dot_nt(qh, k_loc) + bias_ref[hi, case]
            s_ctx = _dot_nt(qh, kc)
            m = jnp.maximum(jnp.max(s_loc, axis=-1, keepdims=True), jnp.max(s_ctx, axis=-1, keepdims=True))
            p_loc = jnp.exp(s_loc - m)
            p_ctx = jnp.exp(s_ctx - m)
            den = jnp.sum(p_loc, axis=-1, keepdims=True) + jnp.sum(p_ctx, axis=-1, keepdims=True)
            o = (_dot(p_loc, v_loc) + _dot(p_ctx, vc)) / den
            out = o if out is None else jnp.where(hm, o, out)
        o_ref[0, q_rows, :] = out.astype(o_ref.dtype)
        return carry

    lax.fori_loop(0, rows // NA_ROW_BLOCK, block_body, 0, unroll=2)


def _na_block_plan(rows):
    kr = min(WIN_ROWS, rows)
    ku = min(kr + NA_ROW_BLOCK - 1, rows)
    u0s, patterns, case_of_block = [], [], []
    for i in range(rows // NA_ROW_BLOCK):
        u0 = int(np.clip(NA_ROW_BLOCK * i - kr // 2, 0, rows - ku))
        rel = []
        for a in range(NA_ROW_BLOCK):
            r = NA_ROW_BLOCK * i + a
            r0 = int(np.clip(r - kr // 2, 0, rows - kr))
            assert u0 <= r0 and r0 + kr <= u0 + ku
            rel.append((r0 - u0, u0 - r))
        u0s.append(u0)
        if tuple(rel) not in patterns:
            patterns.append(tuple(rel))
        case_of_block.append(patterns.index(tuple(rel)))
    return kr, ku, u0s, patterns, case_of_block


def _na_bias_table(rpb, rows):
    kr, ku, _, patterns, _ = _na_block_plan(rows)
    q = np.arange(GRID_W)
    kcol = np.arange(GRID_W)
    ws = np.clip(q - WIN_COLS // 2, 0, GRID_W - WIN_COLS)
    in_win = (kcol[None, :] >= ws[:, None]) & (kcol[None, :] < ws[:, None] + WIN_COLS)
    dc = np.clip(kcol[None, :] - q[:, None] + WIN_COLS - 1, 0, 2 * WIN_COLS - 2)
    u = np.arange(ku)
    n_dr = 2 * WIN_ROWS - 1
    dr = np.full((len(patterns), NA_ROW_BLOCK, ku), n_dr, np.int32)
    for c, rel in enumerate(patterns):
        for a, (r0_rel, u0_minus_r) in enumerate(rel):
            row_ok = (u >= r0_rel) & (u < r0_rel + kr)
            dr[c, a] = np.where(row_ok, u0_minus_r + u + WIN_ROWS - 1, n_dr)
    h = rpb.shape[0]
    tiles = jnp.where(in_win[None, None], rpb.astype(F32)[:, :, dc], MASK_NEG)
    tiles = jnp.concatenate([tiles, jnp.full((h, 1, GRID_W, GRID_W), MASK_NEG, F32)], axis=1)
    tbl = jnp.take(tiles, jnp.asarray(dr.reshape(-1)), axis=1)
    tbl = tbl.reshape(h, len(patterns), NA_ROW_BLOCK, ku, GRID_W, GRID_W)
    return tbl.transpose(0, 1, 2, 4, 3, 5).reshape(h, len(patterns), NA_ROW_BLOCK * GRID_W, ku * GRID_W)


def _na_attention(qkv, bias, t_lat):
    b, tt, w3 = qkv.shape
    na_w = w3 // 3
    ncol = na_w // LANES
    rows = t_lat // GRID_W
    bias_block = (2,) + bias.shape[1:]
    kern = functools.partial(_na_kernel, t_lat=t_lat, rows=rows)
    return pl.pallas_call(
        kern,
        grid=(b, ncol),
        in_specs=[pl.BlockSpec((1, tt, LANES), lambda i, p: (i, 0, p)),
                  pl.BlockSpec((1, tt, LANES), lambda i, p: (i, 0, ncol + p)),
                  pl.BlockSpec((1, tt, LANES), lambda i, p: (i, 0, 2 * ncol + p)),
                  pl.BlockSpec(bias_block, lambda i, p: (p, 0, 0, 0))],
        out_specs=pl.BlockSpec((1, tt, LANES), lambda i, p: (i, 0, p)),
        out_shape=jax.ShapeDtypeStruct((b, tt, na_w), BF16),
        compiler_params=_cparams(("parallel", "parallel")),
        name="na_attention",
    )(qkv, qkv, qkv, bias)


def _scan_block(i, nt, rev):
    if rev:
        return jnp.where(i == 0, nt - 1, nt - 1 - i)
    return jnp.where(i == 0, nt - 1, i - 1)


def _hgrn_kernel(q_ref, f_ref, v_ref, lb_ref, o_ref, st_ref, *, rev):
    n_heads = st_ref.shape[0]
    hd = HG_HEAD_DIM
    n_sub = CHUNK // SUB

    @pl.when(pl.program_id(1) == 0)
    def _():
        st_ref[...] = jnp.zeros_like(st_ref)

    tri = _tri_matrix(CHUNK, rev)
    row = lax.broadcasted_iota(jnp.int32, (CHUNK, CHUNK), 0)
    col = lax.broadcasted_iota(jnp.int32, (CHUNK, CHUNK), 1)
    causal = (col >= row) if rev else (col <= row)
    rowv = lax.broadcasted_iota(jnp.int32, (CHUNK, 1), 0)
    lb = lb_ref[0:1, :]
    one_m_lb = lb_ref[1:2, :]
    width = lb.shape[1]
    zero_row = jnp.zeros((1, width), F32)
    rows_of = lambda rws: jnp.concatenate([jnp.broadcast_to(r, (SUB, width)) for r in rws], axis=0)
    scan_blocks = list(range(n_sub - 1, -1, -1)) if rev else list(range(n_sub))

    n_chunks = q_ref.shape[1] // CHUNK
    order = range(n_chunks - 1, -1, -1) if rev else range(n_chunks)
    states = [st_ref[h] for h in range(n_heads)]
    for cc in order:
        rs = slice(cc * CHUNK, (cc + 1) * CHUNK)
        x = f_ref[0, rs, :]
        e = jnp.exp(-jnp.abs(x))
        r = 1.0 / (1.0 + e)
        x_pos = x >= 0.0
        f = jnp.where(x_pos, 1.0 + lb * e, lb + e) * r
        log_f = jnp.where(f > 0.0, jnp.log(f), x)
        k_all = one_m_lb * jnp.where(x_pos, e, 1.0) * r
        q_all = _silu(q_ref[0, rs, :])
        v_all = v_ref[0, rs, :]
        g_all = _tri_cumsum(tri, log_f)

        c = [g_all[j * SUB:j * SUB + 1, :] if rev else g_all[(j + 1) * SUB - 1:(j + 1) * SUB, :]
             for j in range(n_sub)]
        c_prev = [zero_row] * n_sub
        for before, after in zip(scan_blocks[:-1], scan_blocks[1:]):
            c_prev[after] = c[before]
        c_final = c[scan_blocks[-1]]
        rng = None
        for j in range(n_sub):
            d = c_prev[j] - c[j]
            rng = d if rng is None else jnp.maximum(rng, d)
        safe = jnp.max(rng) <= SAFE_RANGE

        q_b = q_all * jnp.exp(g_all - rows_of(c_prev))
        k_end = k_all * jnp.exp(rows_of(c) - g_all)
        q_g = q_b * rows_of([jnp.exp(cp) for cp in c_prev])
        k_last = k_end * rows_of([jnp.exp(c_final - cj) for cj in c])
        e_last = jnp.exp(c_final)
        k_comb = []
        for sb in range(n_sub):
            mult = []
            for j in range(n_sub):
                if j == sb:
                    mult.append(jnp.where(safe, jnp.exp(jnp.minimum(c_prev[sb] - c[sb], SAFE_RANGE)), 0.0))
                elif scan_blocks.index(j) < scan_blocks.index(sb):
                    mult.append(jnp.exp(c_prev[sb] - c[j]))
                else:
                    mult.append(zero_row)
            k_comb.append(k_end * rows_of(mult))

        heads = [slice(h * hd, (h + 1) * hd) for h in range(n_heads)]
        att = []
        for hs in heads:
            blocks = [_dot_nt(q_b[sb * SUB:(sb + 1) * SUB, hs], k_comb[sb][:, hs]) for sb in range(n_sub)]
            att.append(jnp.where(causal, jnp.concatenate(blocks, axis=0), 0.0))

        def slow_diag(q_all=q_all, k_all=k_all, g_all=g_all):
            pos = rowv % SUB
            accs = []
            for hs in heads:
                q, k, g = q_all[:, hs], k_all[:, hs], g_all[:, hs]
                acc = jnp.zeros((CHUNK, CHUNK), F32)
                for dlt in range(SUB):
                    shift = (CHUNK - dlt) % CHUNK if rev else dlt
                    k_d = pltpu.roll(k, shift, 0) if shift else k
                    g_d = pltpu.roll(g, shift, 0) if shift else g
                    valid = (pos + dlt <= SUB - 1) if rev else (pos >= dlt)
                    e = jnp.exp(jnp.where(valid, g - g_d, 0.0))
                    val = jnp.sum(q * k_d * e, axis=-1, keepdims=True)
                    partner = (row + dlt) if rev else (row - dlt)
                    acc = acc + jnp.where((col == partner) & valid, val, 0.0)
                accs.append(acc)
            return jnp.stack(accs, axis=0)

        diag = lax.cond(safe, lambda: jnp.zeros((n_heads, CHUNK, CHUNK), F32), slow_diag)
        outs = []
        for h, hs in enumerate(heads):
            a = att[h] + diag[h]
            st = states[h]
            o = _dot(a, v_all[:, hs]) + _dot_nt(q_g[:, hs], st)
            states[h] = st * e_last[:, hs] + lax.dot_general(
                v_all[:, hs].astype(BF16), k_last[:, hs].astype(BF16), (((0,), (0,)), ((), ())),
                preferred_element_type=F32)
            outs.append(o)
        o_ref[0, rs, :] = jnp.concatenate(outs, axis=-1).astype(o_ref.dtype)
    for h in range(n_heads):
        st_ref[h] = states[h]


def _hgrn_scan(hraw, lb_rows, f_col, rev):
    b, tt, n = hraw.shape
    w = lb_rows.shape[1]
    nt = tt // TM
    n_heads = w // HG_HEAD_DIM
    blk = lambda i: _scan_block(i, nt, rev)
    return pl.pallas_call(
        functools.partial(_hgrn_kernel, rev=rev),
        grid=(b, nt),
        in_specs=[pl.BlockSpec((1, TM, w), lambda bi, i: (bi, blk(i), 0)),
                  pl.BlockSpec((1, TM, w), lambda bi, i: (bi, blk(i), f_col)),
                  pl.BlockSpec((1, TM, w), lambda bi, i: (bi, blk(i), 3)),
                  pl.BlockSpec((8, w), lambda bi, i: (0, 0))],
        out_specs=pl.BlockSpec((1, TM, w), lambda bi, i: (bi, blk(i), 0)),
        out_shape=jax.ShapeDtypeStruct((b, tt, w), BF16),
        scratch_shapes=[pltpu.VMEM((n_heads, HG_HEAD_DIM, HG_HEAD_DIM), F32)],
        compiler_params=_cparams(("parallel", "arbitrary")),
        name="hgrn_bwd" if rev else "hgrn_fwd",
    )(hraw, hraw, hraw, lb_rows)


def _outproj0_kernel(na_ref, of_ref, ob_ref, gate_ref, x_ref, c_ref, m_ref, gain_ref, w_ref, o_ref, *, n_lat_tiles):
    na_w = na_ref.shape[2]
    y_na, y_hg = [], []
    for bi in range(BB):
        o = of_ref[bi].astype(F32) + ob_ref[bi].astype(F32)
        gate = _silu(gate_ref[bi])
        ys = []
        for h in range(o.shape[1] // HG_HEAD_DIM):
            hs = slice(h * HG_HEAD_DIM, (h + 1) * HG_HEAD_DIM)
            oh = o[:, hs]
            ms = jnp.mean(oh * oh, axis=-1, keepdims=True)
            ys.append(oh * lax.rsqrt(ms + EPS) * gain_ref[:, hs] * gate[:, hs])
        y_hg.append(jnp.concatenate(ys, axis=-1).astype(BF16))
        y_na.append(na_ref[bi])
    y = (jnp.dot(jnp.concatenate(y_na, axis=0), w_ref[0:na_w, :], preferred_element_type=F32)
         + jnp.dot(jnp.concatenate(y_hg, axis=0), w_ref[na_w:, :], preferred_element_type=F32))
    for bi in range(BB):
        o_ref[bi] = _stream0_rows(x_ref, c_ref, bi, n_lat_tiles) + m_ref[bi, 2:3, :] * y[bi * TM:(bi + 1) * TM]


def _outproj0(o_na, o_f, o_b, hraw, x, ctx, mod, gain, w):
    b, t_lat, d = x.shape
    tt = t_lat + ctx.shape[1]
    na_w = o_na.shape[2]
    hw = o_f.shape[2]
    n_lat_tiles = t_lat // TM
    return pl.pallas_call(
        functools.partial(_outproj0_kernel, n_lat_tiles=n_lat_tiles),
        grid=(b // BB, n_lat_tiles + 1),
        in_specs=[_tile_spec(na_w), _tile_spec(hw), _tile_spec(hw),
                  pl.BlockSpec((BB, TM, hw), lambda i, t: (i, t, 4))] + _stream0_specs(d, n_lat_tiles) + [
            _mod_spec(d, n_lat_tiles), _const_spec((1, hw)), _const_spec(w.shape)],
        out_specs=_tile_spec(d),
        out_shape=jax.ShapeDtypeStruct((b, tt, d), F32),
        compiler_params=_cparams(("parallel", "parallel")),
        name="outproj0",
    )(o_na, o_f, o_b, hraw, x, ctx, mod, gain, w)


HALO = 8
TM_EXT = TM + 2 * HALO


def _halo_specs(width, tt):
    r8 = TM // HALO
    last = tt // HALO - 1
    return [pl.BlockSpec((BB, HALO, width), lambda i, t: (i, jnp.maximum(t * r8 - 1, 0), 0)),
            pl.BlockSpec((BB, HALO, width), lambda i, t: (i, jnp.minimum((t + 1) * r8, last), 0))]


def _modulated_with_halo(h_ref, hp_ref, hn_ref, gain, m_ref, shift_row, scale_row):
    us, exts = [], []
    for bi in range(BB):
        shift = m_ref[bi, shift_row:shift_row + 1, :]
        scale = m_ref[bi, scale_row:scale_row + 1, :]
        u = _rms_mod(h_ref[bi], gain, shift, scale)
        us.append(u.astype(BF16))
        exts += [_rms_mod(hp_ref[bi], gain, shift, scale), u, _rms_mod(hn_ref[bi], gain, shift, scale)]
    return jnp.concatenate(us, axis=0), jnp.concatenate(exts, axis=0).astype(BF16)


def _conv3_ext(a_ext, cw, cb, n_lat_tiles):
    n = a_ext.shape[0] - 2 * HALO
    t = pl.program_id(1)
    has_prev = jnp.logical_and(t != 0, t != n_lat_tiles)
    has_next = jnp.logical_and(t != n_lat_tiles - 1, t != n_lat_tiles)
    a = a_ext[HALO:HALO + n]
    prev_row = jnp.where(has_prev, a_ext[HALO - 1:HALO], 0.0)
    next_row = jnp.where(has_next, a_ext[HALO + n:HALO + n + 1], 0.0)
    row = lax.broadcasted_iota(jnp.int32, (n, 1), 0)
    up = jnp.where(row == 0, prev_row, pltpu.roll(a, 1, 0))
    dn = jnp.where(row == n - 1, next_row, pltpu.roll(a, n - 1, 0))
    return cw[0:1, :] * up + cw[1:2, :] * a + cw[2:3, :] * dn + cb


def _ffn_up_kernel(h_ref, hp_ref, hn_ref, m_ref, gain_ref, w_ref, cw_ref, cb_ref, mid_ref, *, n_lat_tiles):
    u, u_ext = _modulated_with_halo(h_ref, hp_ref, hn_ref, gain_ref[...], m_ref, 3, 4)
    dff = mid_ref.shape[2]
    for lo, hi in _col_chunks(dff, 768):
        a_ext = jnp.dot(u_ext, w_ref[:, lo:hi], preferred_element_type=F32)
        v = jnp.dot(u, w_ref[:, dff + lo:dff + hi], preferred_element_type=F32)
        for bi in range(BB):
            c = _conv3_ext(a_ext[bi * TM_EXT:(bi + 1) * TM_EXT], cw_ref[:, lo:hi], cb_ref[:, lo:hi], n_lat_tiles)
            mid_ref[bi, :, lo:hi] = (_gelu_tanh(c) * v[bi * TM:(bi + 1) * TM]).astype(mid_ref.dtype)


def _ffn_up(hh, mod, gain, w, conv_w, conv_b, n_tiles, n_lat_tiles):
    b, tt_in, d = hh.shape
    dff = w.shape[1] // 2
    return pl.pallas_call(
        functools.partial(_ffn_up_kernel, n_lat_tiles=n_lat_tiles),
        grid=(b // BB, n_tiles),
        in_specs=[_tile_spec(d)] + _halo_specs(d, tt_in) + [
            _mod_spec(d, n_lat_tiles), _const_spec((1, d)), _const_spec(w.shape),
            _const_spec((3, dff)), _const_spec((1, dff))],
        out_specs=_tile_spec(dff),
        out_shape=jax.ShapeDtypeStruct((b, n_tiles * TM, dff), BF16),
        compiler_params=_cparams(("parallel", "parallel")),
        name="ffn_up",
    )(hh, hh, hh, mod, gain, w, conv_w, conv_b)


def _ffn_down_kernel(mid_ref, w_ref, h_ref, m_ref, o_ref):
    y = jnp.dot(jnp.concatenate([mid_ref[bi] for bi in range(BB)], axis=0), w_ref[...],
                preferred_element_type=F32)
    for bi in range(BB):
        o_ref[bi] = h_ref[bi] + m_ref[bi, 5:6, :] * y[bi * TM:(bi + 1) * TM]


def _ffn_down(mid, w, hh, mod, n_lat_tiles):
    b, tt, dff = mid.shape
    d = hh.shape[2]
    return pl.pallas_call(
        _ffn_down_kernel,
        grid=(b // BB, tt // TM),
        in_specs=[_tile_spec(dff), _const_spec(w.shape), _tile_spec(d), _mod_spec(d, n_lat_tiles)],
        out_specs=_tile_spec(d),
        out_shape=jax.ShapeDtypeStruct((b, tt, d), F32),
        compiler_params=_cparams(("parallel", "parallel")),
        name="ffn_down",
    )(mid, w, hh, mod)


def _inproj1_kernel(h_ref, hp_ref, hn_ref, m_ref, gain_ref, w_ref, cw_ref, cb_ref, dtb_ref, a_ref,
                    z_ref, xs_ref, bc_ref, dtp_ref, *, n_lat_tiles):
    u, u_ext = _modulated_with_halo(h_ref, hp_ref, hn_ref, gain_ref[...], m_ref, 0, 1)
    nz = z_ref.shape[2]
    nxs = xs_ref.shape[2]
    nx = nxs + bc_ref.shape[2]
    step = 512
    for lo, hi in _col_chunks(nz, step):
        z = _silu(jnp.dot(u, w_ref[:, lo:hi], preferred_element_type=F32))
        for bi in range(BB):
            z_ref[bi, :, lo:hi] = z[bi * TM:(bi + 1) * TM].astype(z_ref.dtype)
    for lo, hi in _col_chunks(nx, step):
        xbc_ext = jnp.dot(u_ext, w_ref[:, nz + lo:nz + hi], preferred_element_type=F32)
        for bi in range(BB):
            c = _silu(_conv3_ext(xbc_ext[bi * TM_EXT:(bi + 1) * TM_EXT], cw_ref[:, lo:hi], cb_ref[:, lo:hi],
                                 n_lat_tiles))
            if hi <= nxs:
                xs_ref[bi, :, lo:hi] = c.astype(xs_ref.dtype)
            else:
                bc_ref[bi, :, lo - nxs:hi - nxs] = c.astype(bc_ref.dtype)
    raw = jnp.dot(u, w_ref[:, nz + nx:], preferred_element_type=F32)
    dt = _softplus(raw + dtb_ref[...])
    la = dt * a_ref[...]
    for bi in range(BB):
        dtp_ref[bi, :, 0:2 * LANES] = dt[bi * TM:(bi + 1) * TM]
        dtp_ref[bi, :, 2 * LANES:4 * LANES] = la[bi * TM:(bi + 1) * TM]


def _inproj1(hh, mod, gain, w, conv_w, conv_b, dt_bias, a_neg, nz, nxs):
    b, tt, d = hh.shape
    nt = tt // TM
    n_lat_tiles = nt - 1
    nx = conv_w.shape[1]
    assert nxs % 512 == 0
    tile = _tile_spec
    return pl.pallas_call(
        functools.partial(_inproj1_kernel, n_lat_tiles=n_lat_tiles),
        grid=(b // BB, nt),
        in_specs=[tile(d)] + _halo_specs(d, tt) + [
            _mod_spec(d, n_lat_tiles), _const_spec((1, d)), _const_spec(w.shape),
            _const_spec((3, nx)), _const_spec((1, nx)),
            _const_spec((1, 2 * LANES)), _const_spec((1, 2 * LANES))],
        out_specs=[tile(nz), tile(nxs), tile(nx - nxs), tile(4 * LANES)],
        out_shape=[jax.ShapeDtypeStruct((b, tt, nz), BF16),
                   jax.ShapeDtypeStruct((b, tt, nxs), BF16),
                   jax.ShapeDtypeStruct((b, tt, nx - nxs), BF16),
                   jax.ShapeDtypeStruct((b, tt, 4 * LANES), F32)],
        compiler_params=_cparams(("parallel", "parallel")),
        name="inproj1",
    )(hh, hh, hh, mod, gain, w, conv_w, conv_b, dt_bias, a_neg)


def _ssd_kernel(x_ref, bc_ref, dt_ref, la_ref, e_ref, o_ref, s_ref, *, rev):
    n_groups = s_ref.shape[0]
    ns = SSD_STATE
    gw = s_ref.shape[2]
    width = x_ref.shape[2]
    n_heads = width // SSD_HEAD_DIM
    blk = 256

    @pl.when(pl.program_id(1) == 0)
    def _():
        s_ref[...] = jnp.zeros_like(s_ref)

    tri = _tri_matrix(CHUNK, rev)
    lane128 = lax.broadcasted_iota(jnp.int32, (1, LANES), 1)
    rowc = lax.broadcasted_iota(jnp.int32, (CHUNK, width), 0)
    pos = lax.broadcasted_iota(jnp.int32, (CHUNK, width), 1) % SSD_HEAD_DIM
    on_diag = pos == rowc
    causal = (pos >= rowc) if rev else (pos <= rowc)
    blk_head = lax.broadcasted_iota(jnp.int32, (1, blk), 1) // SSD_HEAD_DIM
    last = 0 if rev else CHUNK - 1

    def expand(v):
        p1, p2, p3 = _split3(v)
        zero = jnp.zeros_like(p1)
        st = jnp.where(lane128 < n_heads, p1,
                       jnp.where(lane128 < 2 * n_heads, p2, jnp.where(lane128 < 3 * n_heads, p3, zero)))
        return jnp.dot(st, e_ref[...], preferred_element_type=F32)

    n_chunks = x_ref.shape[1] // CHUNK
    order = range(n_chunks - 1, -1, -1) if rev else range(n_chunks)
    states = [s_ref[g] for g in range(n_groups)]
    for cc in order:
        rs = slice(cc * CHUNK, (cc + 1) * CHUNK)
        x = x_ref[0, rs, :].astype(F32)
        bc = bc_ref[0, rs, :]
        cum = expand(_tri_cumsum(tri, la_ref[0, rs, :]))
        dt = expand(dt_ref[0, rs, :])
        cum_row = jnp.sum(jnp.where(on_diag, cum, 0.0), axis=0, keepdims=True)
        cum_last = cum[last:last + 1, :]
        m_all = jnp.exp(jnp.where(causal, cum - cum_row, MASK_NEG))
        e_cum = jnp.exp(cum)
        x_dt = x * dt
        xw_all = (x_dt * jnp.exp(cum_last - cum)).astype(BF16)
        x_dt = x_dt.astype(BF16)
        e_last = jnp.exp(cum_last)
        ys = []
        for g in range(n_groups):
            gs = slice(g * gw, (g + 1) * gw)
            b_g = bc[:, g * ns:(g + 1) * ns]
            c_g = bc[:, (n_groups + g) * ns:(n_groups + g + 1) * ns]
            b_rep = jnp.concatenate([b_g] * (gw // CHUNK), axis=0)
            m_g = (m_all[:, gs] * _dot_nt(c_g, b_rep)).astype(BF16)
            x_g = x_dt[:, gs]
            y_parts = []
            for j in range(gw // blk):
                x4 = x_g[:, j * blk:(j + 1) * blk]
                x_bd = jnp.concatenate(
                    [jnp.where(blk_head == hh, x4, jnp.zeros_like(x4)) for hh in range(blk // SSD_HEAD_DIM)], axis=0)
                y_parts.append(jnp.dot(m_g[:, j * blk:(j + 1) * blk], x_bd, preferred_element_type=F32))
            s_g = states[g]
            y_g = jnp.concatenate(y_parts, axis=-1) + _dot(c_g, s_g) * e_cum[:, gs]
            states[g] = s_g * e_last[:, gs] + lax.dot_general(
                b_g, xw_all[:, gs], (((0,), (0,)), ((), ())), preferred_element_type=F32)
            ys.append(y_g)
        o_ref[0, rs, :] = jnp.concatenate(ys, axis=-1).astype(o_ref.dtype)
    for g in range(n_groups):
        s_ref[g] = states[g]


def _ssd_scan(xs, bc, dtp, expand_mat, rev):
    b, tt, width = xs.shape
    nt = tt // TM
    nbc = bc.shape[2]
    gw = width // SSD_GROUPS
    blk = lambda i: _scan_block(i, nt, rev)
    d_col = 1 if rev else 0
    return pl.pallas_call(
        functools.partial(_ssd_kernel, rev=rev),
        grid=(b, nt),
        in_specs=[pl.BlockSpec((1, TM, width), lambda bi, i: (bi, blk(i), 0)),
                  pl.BlockSpec((1, TM, nbc), lambda bi, i: (bi, blk(i), 0)),
                  pl.BlockSpec((1, TM, LANES), lambda bi, i: (bi, blk(i), d_col)),
                  pl.BlockSpec((1, TM, LANES), lambda bi, i: (bi, blk(i), 2 + d_col)),
                  pl.BlockSpec(expand_mat.shape, lambda bi, i: (0, 0))],
        out_specs=pl.BlockSpec((1, TM, width), lambda bi, i: (bi, blk(i), 0)),
        out_shape=jax.ShapeDtypeStruct((b, tt, width), BF16),
        scratch_shapes=[pltpu.VMEM((SSD_GROUPS, SSD_STATE, gw), F32)],
        compiler_params=_cparams(("parallel", "arbitrary")),
        name="ssd_bwd" if rev else "ssd_fwd",
    )(xs, bc, dtp, dtp, expand_mat)


def _outproj1_kernel(yf_ref, yb_ref, xs_ref, sz_ref, h_ref, m_ref, d_ref, gain_ref, w_ref, o_ref):
    width = xs_ref.shape[2]
    gw = width // SSD_GROUPS
    normed = []
    for bi in range(BB):
        yy = ((yf_ref[bi].astype(F32) + yb_ref[bi].astype(F32) + d_ref[...] * xs_ref[bi].astype(F32))
              * sz_ref[bi].astype(F32))
        parts = []
        for g in range(SSD_GROUPS):
            gs = slice(g * gw, (g + 1) * gw)
            yg = yy[:, gs]
            ms = jnp.mean(yg * yg, axis=-1, keepdims=True)
            parts.append((yg * lax.rsqrt(ms + EPS) * gain_ref[:, gs]).astype(BF16))
        normed.append(jnp.concatenate(parts, axis=-1))
    y = jnp.dot(jnp.concatenate(normed, axis=0), w_ref[...], preferred_element_type=F32)
    for bi in range(BB):
        o_ref[bi] = h_ref[bi] + m_ref[bi, 2:3, :] * y[bi * TM:(bi + 1) * TM]


def _outproj1(y_f, y_b, xs, sz, hh, mod, d_skip, gain, w, n_lat_tiles):
    b, _, d = hh.shape
    width = xs.shape[2]
    tile = _tile_spec
    return pl.pallas_call(
        _outproj1_kernel,
        grid=(b // BB, n_lat_tiles),
        in_specs=[tile(width), tile(width), tile(width), tile(width), tile(d),
                  _mod_spec(d, n_lat_tiles), _const_spec((1, width)), _const_spec((1, width)),
                  _const_spec(w.shape)],
        out_specs=tile(d),
        out_shape=jax.ShapeDtypeStruct((b, n_lat_tiles * TM, d), F32),
        compiler_params=_cparams(("parallel", "parallel")),
        name="outproj1",
    )(y_f, y_b, xs, sz, hh, mod, d_skip, gain, w)


def _mod_rows(mods_l, b, d):
    six = mods_l.reshape(mods_l.shape[0], 6, d)
    lat = six[:b]
    ctx = jnp.broadcast_to(six[b:b + 1], (b, 6, d))
    both = jnp.stack([lat, ctx], axis=1)
    return jnp.pad(both, ((0, 0), (0, 0), (0, 2), (0, 0)))


def kernel(x, c, ctx, c_ctx, w_mod, b_mod, norm_mix, norm_ffn, ffn_w_up, ffn_conv_w, ffn_conv_b, ffn_w_down,
           hy_w_in, hy_w_out, na_q_gain, na_k_gain, na_rpb, hg_out_gain, hg_lb_fwd, hg_lb_bwd, ssd_w_in,
           ssd_conv_w, ssd_conv_b, ssd_dt_bias_fwd, ssd_dt_bias_bwd, ssd_a_log_fwd, ssd_a_log_bwd, ssd_d,
           ssd_norm_gain, ssd_w_out):
    b, t_lat, d = x.shape
    l_ctx = ctx.shape[1]
    assert l_ctx == TM and t_lat % TM == 0 and w_mod.shape[0] == 2 and b % BB == 0
    n_lat_tiles = t_lat // TM
    nt = n_lat_tiles + 1

    rows = ((b + 1 + 7) // 8) * 8
    cond = jnp.concatenate([c, c_ctx[None], jnp.zeros((rows - b - 1, d), F32)], axis=0)
    mods = _modulation(cond, w_mod, b_mod)
    mod0 = _mod_rows(mods[0], b, d)
    mod1 = _mod_rows(mods[1], b, d)
    row = lambda v: v.reshape(1, -1).astype(F32)

    hg_w = hg_lb_fwd.shape[1]
    na_w = hy_w_out.shape[1] - hg_w
    n_na_heads = na_w // NA_HEAD_DIM
    qg = row(jnp.tile(na_q_gain[0], n_na_heads)) * (NA_HEAD_DIM ** -0.5)
    kg = row(jnp.tile(na_k_gain[0], n_na_heads))
    qkv, hraw = _inproj0(x, ctx, mod0, row(norm_mix[0]), hy_w_in[0].astype(BF16), qg, kg)
    bias = _na_bias_table(na_rpb[0], t_lat // GRID_W)
    o_na = _na_attention(qkv, bias, t_lat)

    def lb_rows(lb_param):
        lb = jnp.cumsum(jax.nn.softmax(lb_param.astype(F32), axis=0), axis=0)[0]
        return jnp.pad(jnp.stack([lb, 1.0 - lb], axis=0), ((0, 6), (0, 0)))

    o_f = _hgrn_scan(hraw, lb_rows(hg_lb_fwd), 1, rev=False)
    o_b = _hgrn_scan(hraw, lb_rows(hg_lb_bwd), 2, rev=True)
    hg_gain = row(jnp.tile(hg_out_gain[0], hg_w // HG_HEAD_DIM))
    h1 = _outproj0(o_na, o_f, o_b, hraw, x, ctx, mod0, hg_gain, hy_w_out[0].astype(BF16))

    mid0 = _ffn_up(h1, mod0, row(norm_ffn[0]), ffn_w_up[0].astype(BF16), ffn_conv_w[0], row(ffn_conv_b[0]),
                   nt, n_lat_tiles)
    h2 = _ffn_down(mid0, ffn_w_down[0].astype(BF16), h1, mod0, n_lat_tiles)

    n_heads = ssd_d.shape[1]
    inner = ssd_w_out.shape[1]
    nxbc = ssd_conv_w.shape[2]
    w1 = ssd_w_in[0]
    rep = LANES // n_heads
    w_dtf = jnp.tile(w1[:, inner + nxbc:inner + nxbc + n_heads], (1, rep))
    w_dtb = jnp.tile(w1[:, inner + nxbc + n_heads:], (1, rep))
    w1p = jnp.concatenate([w1[:, :inner + nxbc], w_dtf, w_dtb], axis=1).astype(BF16)
    dt_bias = row(jnp.concatenate([jnp.tile(ssd_dt_bias_fwd[0], rep), jnp.tile(ssd_dt_bias_bwd[0], rep)]))
    a_neg = row(jnp.concatenate([jnp.tile(-jnp.exp(ssd_a_log_fwd[0].astype(F32)), rep),
                                 jnp.tile(-jnp.exp(ssd_a_log_bwd[0].astype(F32)), rep)]))
    z, xs, bc, dtp = _inproj1(h2, mod1, row(norm_mix[1]), w1p, ssd_conv_w[0], row(ssd_conv_b[0]), dt_bias,
                              a_neg, inner, inner)

    lane = np.arange(LANES)[:, None]
    colh = (np.arange(inner) // SSD_HEAD_DIM)[None, :]
    expand_mat = jnp.asarray(((lane % n_heads == colh) & (lane < 3 * n_heads)).astype(np.float32), dtype=BF16)
    y_f = _ssd_scan(xs, bc, dtp, expand_mat, rev=False)
    y_b = _ssd_scan(xs, bc, dtp, expand_mat, rev=True)
    d_skip = row(jnp.repeat(ssd_d[0], SSD_HEAD_DIM))
    h3 = _outproj1(y_f, y_b, xs, z, h2, mod1, d_skip, row(ssd_norm_gain[0]), ssd_w_out[0].astype(BF16),
                   n_lat_tiles)

    mid1 = _ffn_up(h3, mod1, row(norm_ffn[1]), ffn_w_up[1].astype(BF16), ffn_conv_w[1], row(ffn_conv_b[1]),
                   n_lat_tiles, n_lat_tiles)
    return _ffn_down(mid1, ffn_w_down[1].astype(BF16), h3, mod1, n_lat_tiles)
```

```python
import functools
import math

import numpy as np
import jax
import jax.numpy as jnp
from jax import lax
from jax.experimental import pallas as pl
from jax.experimental.pallas import tpu as pltpu

F32 = jnp.float32
BF16 = jnp.bfloat16
EPS = 1e-6

GRID_W = 64
NA_HEAD_DIM = 64
WIN_ROWS = 8
WIN_COLS = 16
NA_ROW_BLOCK = 4
HG_HEAD_DIM = 128
SSD_HEAD_DIM = 64
SSD_GROUPS = 4
SSD_STATE = 128
CHUNK = 64
SUB = 16
TM = 256
BB = 2
LANES = 128
MASK_NEG = -1e30
SAFE_RANGE = 60.0

VMEM_LIMIT = 56 * 1024 * 1024


def _cparams(sem):
    return pltpu.CompilerParams(dimension_semantics=sem, vmem_limit_bytes=VMEM_LIMIT)


def _dot(a, b):
    return jnp.dot(a.astype(BF16), b.astype(BF16), preferred_element_type=F32)


def _dot_nt(a, b):
    return lax.dot_general(a.astype(BF16), b.astype(BF16), (((1,), (1,)), ((), ())),
                           preferred_element_type=F32)


def _sigmoid(x):
    return 1.0 / (1.0 + jnp.exp(-x))


def _silu(x):
    return x * _sigmoid(x)


def _softplus(x):
    return jnp.maximum(x, 0.0) + jnp.log1p(jnp.exp(-jnp.abs(x)))


def _gelu_tanh(x):
    c = math.sqrt(2.0 / math.pi)
    return 0.5 * x * (1.0 + jnp.tanh(c * (x + 0.044715 * (x * x * x))))


def _rms_mod(x, gain, shift, scale):
    ms = jnp.mean(x * x, axis=-1, keepdims=True)
    return (x * lax.rsqrt(ms + EPS) * gain) * (1.0 + scale) + shift


def _split3(v):
    p1 = v.astype(BF16)
    r1 = v - p1.astype(F32)
    p2 = r1.astype(BF16)
    r2 = r1 - p2.astype(F32)
    return p1, p2, r2.astype(BF16)


def _tri_cumsum(tri, v):
    p1, p2, p3 = _split3(v)
    d = lambda p: jnp.dot(tri, p, preferred_element_type=F32)
    return d(p1) + d(p2) + d(p3)


def _mod_spec(d, n_lat_tiles):
    return pl.BlockSpec((BB, None, 8, d), lambda i, t: (i, t // n_lat_tiles, 0, 0))


def _tile_spec(width):
    return pl.BlockSpec((BB, TM, width), lambda i, t: (i, t, 0))


def _const_spec(shape):
    return pl.BlockSpec(shape, lambda i, t: (0,) * len(shape))


def _stream0_specs(d, n_lat_tiles):
    return [pl.BlockSpec((BB, TM, d), lambda i, t: (i, jnp.minimum(t, n_lat_tiles - 1), 0)),
            pl.BlockSpec((BB, TM, d), lambda i, t: (i, 0, 0))]


def _stream0_rows(x_ref, c_ref, bi, n_lat_tiles):
    return jnp.where(pl.program_id(1) == n_lat_tiles, c_ref[bi], x_ref[bi])


def _col_chunks(width, step):
    return [(lo, min(lo + step, width)) for lo in range(0, width, step)]


def _tri_matrix(n, rev):
    r = lax.broadcasted_iota(jnp.int32, (n, n), 0)
    c = lax.broadcasted_iota(jnp.int32, (n, n), 1)
    return jnp.where((c >= r) if rev else (c <= r), 1.0, 0.0).astype(BF16)


def _mod_kernel(s_ref, w_ref, b_ref, o_ref):
    s = _silu(s_ref[...])
    o_ref[0] = _dot(s, w_ref[0]) + b_ref[0]


def _modulation(cond, w_mod, b_mod):
    depth, d, n = w_mod.shape
    rows = cond.shape[0]
    tn = 1536
    return pl.pallas_call(
        _mod_kernel,
        grid=(depth, n // tn),
        in_specs=[pl.BlockSpec((rows, d), lambda l, j: (0, 0)),
                  pl.BlockSpec((1, d, tn), lambda l, j: (l, 0, j)),
                  pl.BlockSpec((1, 1, tn), lambda l, j: (l, 0, j))],
        out_specs=pl.BlockSpec((1, rows, tn), lambda l, j: (l, 0, j)),
        out_shape=jax.ShapeDtypeStruct((depth, rows, n), F32),
        compiler_params=_cparams(("parallel", "parallel")),
        name="modulation",
    )(cond, w_mod, b_mod.reshape(depth, 1, n))


def _inproj0_kernel(x_ref, c_ref, m_ref, gain_ref, w_ref, qg_ref, kg_ref, qkv_ref, hraw_ref, *, n_lat_tiles):
    u = jnp.concatenate(
        [_rms_mod(_stream0_rows(x_ref, c_ref, bi, n_lat_tiles), gain_ref[...], m_ref[bi, 0:1, :],
                  m_ref[bi, 1:2, :]).astype(BF16) for bi in range(BB)], axis=0)
    lo = lax.broadcasted_iota(jnp.int32, (1, LANES), 1) < NA_HEAD_DIM
    na_w = qg_ref.shape[1]

    def head_norm(y, g_ref):
        outs = []
        for c in range(na_w // LANES):
            yc = y[:, c * LANES:(c + 1) * LANES]
            sq = yc * yc
            s_lo = jnp.sum(jnp.where(lo, sq, 0.0), axis=-1, keepdims=True)
            s_hi = jnp.sum(jnp.where(lo, 0.0, sq), axis=-1, keepdims=True)
            inv = jnp.where(lo, lax.rsqrt(s_lo / NA_HEAD_DIM + EPS), lax.rsqrt(s_hi / NA_HEAD_DIM + EPS))
            outs.append(yc * inv * g_ref[:, c * LANES:(c + 1) * LANES])
        return jnp.concatenate(outs, axis=-1)

    def proj(lo_col, width):
        return jnp.dot(u, w_ref[:, lo_col:lo_col + width], preferred_element_type=F32)

    def store(ref, lo_col, y):
        for bi in range(BB):
            ref[bi, :, lo_col:lo_col + y.shape[1]] = y[bi * TM:(bi + 1) * TM].astype(ref.dtype)

    store(qkv_ref, 0, head_norm(proj(0, na_w), qg_ref))
    store(qkv_ref, na_w, head_norm(proj(na_w, na_w), kg_ref))
    store(qkv_ref, 2 * na_w, proj(2 * na_w, na_w))
    for lo_col, hi_col in _col_chunks(hraw_ref.shape[2], 512):
        store(hraw_ref, lo_col, proj(3 * na_w + lo_col, hi_col - lo_col))


def _inproj0(x, ctx, mod, gain, w, qg, kg):
    b, t_lat, d = x.shape
    n = w.shape[1]
    na_w = qg.shape[1]
    n_hg = n - 3 * na_w
    n_lat_tiles = t_lat // TM
    tt = t_lat + ctx.shape[1]
    return pl.pallas_call(
        functools.partial(_inproj0_kernel, n_lat_tiles=n_lat_tiles),
        grid=(b // BB, n_lat_tiles + 1),
        in_specs=_stream0_specs(d, n_lat_tiles) + [
            _mod_spec(d, n_lat_tiles), _const_spec((1, d)), _const_spec((d, n)),
            _const_spec((1, na_w)), _const_spec((1, na_w))],
        out_specs=[_tile_spec(3 * na_w), _tile_spec(n_hg)],
        out_shape=[jax.ShapeDtypeStruct((b, tt, 3 * na_w), BF16),
                   jax.ShapeDtypeStruct((b, tt, n_hg), F32)],
        compiler_params=_cparams(("parallel", "parallel")),
        name="inproj0",
    )(x, ctx, mod, gain, w, qg, kg)


def _na_kernel(q_ref, k_ref, v_ref, bias_ref, o_ref, *, t_lat, rows):
    tt = q_ref.shape[1]
    lane = lax.broadcasted_iota(jnp.int32, (1, LANES), 1)
    head_mask = [lane < NA_HEAD_DIM, lane >= NA_HEAD_DIM]
    kc = k_ref[0, t_lat:tt, :]
    vc = v_ref[0, t_lat:tt, :]
    _, ku, u0s, _, case_of_block = _na_block_plan(rows)
    n_loc = ku * GRID_W
    n_q = NA_ROW_BLOCK * GRID_W

    qc = q_ref[0, t_lat:tt, :]
    oc = None
    for hm in head_mask:
        s = _dot_nt(jnp.where(hm, qc, jnp.zeros_like(qc)), kc)
        p = jnp.exp(s - jnp.max(s, axis=-1, keepdims=True))
        o = _dot(p, vc) / jnp.sum(p, axis=-1, keepdims=True)
        oc = o if oc is None else jnp.where(hm, o, oc)
    o_ref[0, t_lat:tt, :] = oc.astype(o_ref.dtype)

    def block_body(i, carry):
        u0 = u0s[0]
        case = case_of_block[0]
        for bi in range(1, len(u0s)):
            u0 = jnp.where(i == bi, u0s[bi], u0)
            case = jnp.where(i == bi, case_of_block[bi], case)
        q_rows = pl.ds(pl.multiple_of(i * n_q, n_q), n_q)
        q_i = q_ref[0, q_rows, :]
        start = pl.multiple_of(u0 * GRID_W, GRID_W)
        k_loc = k_ref[0, pl.ds(start, n_loc), :]
        v_loc = v_ref[0, pl.ds(start, n_loc), :]
        out = None
        for hi, hm in enumerate(head_mask):
            qh = jnp.where(hm, q_i, jnp.zeros_like(q_i))
            s_loc = _dot_nt(qh, k_loc) + bias_ref[hi, case]
            s_ctx = _dot_nt(qh, kc)
            m = jnp.maximum(jnp.max(s_loc, axis=-1, keepdims=True), jnp.max(s_ctx, axis=-1, keepdims=True))
            p_loc = jnp.exp(s_loc - m)
            p_ctx = jnp.exp(s_ctx - m)
            den = jnp.sum(p_loc, axis=-1, keepdims=True) + jnp.sum(p_ctx, axis=-1, keepdims=True)
            o = (_dot(p_loc, v_loc) + _dot(p_ctx, vc)) / den
            out = o if out is None else jnp.where(hm, o, out)
        o_ref[0, q_rows, :] = out.astype(o_ref.dtype)
        return carry

    lax.fori_loop(0, rows // NA_ROW_BLOCK, block_body, 0, unroll=2)


def _na_block_plan(rows):
    kr = min(WIN_ROWS, rows)
    ku = min(kr + NA_ROW_BLOCK - 1, rows)
    u0s, patterns, case_of_block = [], [], []
    for i in range(rows // NA_ROW_BLOCK):
        u0 = int(np.clip(NA_ROW_BLOCK * i - kr // 2, 0, rows - ku))
        rel = []
        for a in range(NA_ROW_BLOCK):
            r = NA_ROW_BLOCK * i + a
            r0 = int(np.clip(r - kr // 2, 0, rows - kr))
            assert u0 <= r0 and r0 + kr <= u0 + ku
            rel.append((r0 - u0, u0 - r))
        u0s.append(u0)
        if tuple(rel) not in patterns:
            patterns.append(tuple(rel))
        case_of_block.append(patterns.index(tuple(rel)))
    return kr, ku, u0s, patterns, case_of_block


def _na_bias_table(rpb, rows):
    kr, ku, _, patterns, _ = _na_block_plan(rows)
    q = np.arange(GRID_W)
    kcol = np.arange(GRID_W)
    ws = np.clip(q - WIN_COLS // 2, 0, GRID_W - WIN_COLS)
    in_win = (kcol[None, :] >= ws[:, None]) & (kcol[None, :] < ws[:, None] + WIN_COLS)
    dc = np.clip(kcol[None, :] - q[:, None] + WIN_COLS - 1, 0, 2 * WIN_COLS - 2)
    u = np.arange(ku)
    n_dr = 2 * WIN_ROWS - 1
    dr = np.full((len(patterns), NA_ROW_BLOCK, ku), n_dr, np.int32)
    for c, rel in enumerate(patterns):
        for a, (r0_rel, u0_minus_r) in enumerate(rel):
            row_ok = (u >= r0_rel) & (u < r0_rel + kr)
            dr[c, a] = np.where(row_ok, u0_minus_r + u + WIN_ROWS - 1, n_dr)
    h = rpb.shape[0]
    tiles = jnp.where(in_win[None, None], rpb.astype(F32)[:, :, dc], MASK_NEG)
    tiles = jnp.concatenate([tiles, jnp.full((h, 1, GRID_W, GRID_W), MASK_NEG, F32)], axis=1)
    tbl = jnp.take(tiles, jnp.asarray(dr.reshape(-1)), axis=1)
    tbl = tbl.reshape(h, len(patterns), NA_ROW_BLOCK, ku, GRID_W, GRID_W)
    return tbl.transpose(0, 1, 2, 4, 3, 5).reshape(h, len(patterns), NA_ROW_BLOCK * GRID_W, ku * GRID_W)


def _na_attention(qkv, bias, t_lat):
    b, tt, w3 = qkv.shape
    na_w = w3 // 3
    ncol = na_w // LANES
    rows = t_lat // GRID_W
    bias_block = (2,) + bias.shape[1:]
    kern = functools.partial(_na_kernel, t_lat=t_lat, rows=rows)
    return pl.pallas_call(
        kern,
        grid=(b, ncol),
        in_specs=[pl.BlockSpec((1, tt, LANES), lambda i, p: (i, 0, p)),
                  pl.BlockSpec((1, tt, LANES), lambda i, p: (i, 0, ncol + p)),
                  pl.BlockSpec((1, tt, LANES), lambda i, p: (i, 0, 2 * ncol + p)),
                  pl.BlockSpec(bias_block, lambda i, p: (p, 0, 0, 0))],
        out_specs=pl.BlockSpec((1, tt, LANES), lambda i, p: (i, 0, p)),
        out_shape=jax.ShapeDtypeStruct((b, tt, na_w), BF16),
        compiler_params=_cparams(("parallel", "parallel")),
        name="na_attention",
    )(qkv, qkv, qkv, bias)


def _scan_block(i, nt, rev):
    if rev:
        return jnp.where(i == 0, nt - 1, nt - 1 - i)
    return jnp.where(i == 0, nt - 1, i - 1)


class _HgrnDir:
    def __init__(self, rev, q_ref, f_ref, v_ref, lb_ref, o_ref, st_ref):
        self.rev, self.q_ref, self.f_ref, self.v_ref, self.o_ref, self.st_ref = rev, q_ref, f_ref, v_ref, o_ref, st_ref
        self.n_heads = st_ref.shape[0]
        self.lb = lb_ref[0:1, :]
        self.one_m_lb = lb_ref[1:2, :]
        self.n_chunks = q_ref.shape[1] // CHUNK
        self.order = list(range(self.n_chunks - 1, -1, -1)) if rev else list(range(self.n_chunks))
        self.heads = [slice(h * HG_HEAD_DIM, (h + 1) * HG_HEAD_DIM) for h in range(self.n_heads)]
        self.states = [st_ref[h] for h in range(self.n_heads)]
        self.gated = {}

    def gate(self, cc):
        rev, lb, one_m_lb, q_ref, f_ref = self.rev, self.lb, self.one_m_lb, self.q_ref, self.f_ref
        n_sub = CHUNK // SUB
        tri = _tri_matrix(CHUNK, rev)
        zero_row = jnp.zeros((1, lb.shape[1]), F32)
        scan_blocks = list(range(n_sub - 1, -1, -1)) if rev else list(range(n_sub))
        rng = None
        rs = slice(cc * CHUNK, (cc + 1) * CHUNK)
        x = f_ref[0, rs, :]
        e = jnp.exp(-jnp.abs(x))
        r = 1.0 / (1.0 + e)
        x_pos = x >= 0.0
        f = jnp.where(x_pos, 1.0 + lb * e, lb + e) * r
        log_f = jnp.where(f > 0.0, jnp.log(f), x)
        k_all = one_m_lb * jnp.where(x_pos, e, 1.0) * r
        q_all = _silu(q_ref[0, rs, :])
        g_all = _tri_cumsum(tri, log_f)
        c = [g_all[j * SUB:j * SUB + 1, :] if rev else g_all[(j + 1) * SUB - 1:(j + 1) * SUB, :]
             for j in range(n_sub)]
        c_prev = [zero_row] * n_sub
        for before, after in zip(scan_blocks[:-1], scan_blocks[1:]):
            c_prev[after] = c[before]
        for j in range(n_sub):
            d = c_prev[j] - c[j]
            rng = d if rng is None else jnp.maximum(rng, d)
        self.gated[cc] = (q_all, k_all, g_all, c, c_prev)
        return rng

    def slow_diag(self):
        rev = self.rev
        row = lax.broadcasted_iota(jnp.int32, (CHUNK, CHUNK), 0)
        col = lax.broadcasted_iota(jnp.int32, (CHUNK, CHUNK), 1)
        rowv = lax.broadcasted_iota(jnp.int32, (CHUNK, 1), 0)
        pos = rowv % SUB
        accs = []
        for cc in self.order:
            q_all, k_all, g_all = self.gated[cc][:3]
            for hs in self.heads:
                q, k, g = q_all[:, hs], k_all[:, hs], g_all[:, hs]
                acc = jnp.zeros((CHUNK, CHUNK), F32)
                for dlt in range(SUB):
                    shift = (CHUNK - dlt) % CHUNK if rev else dlt
                    k_d = pltpu.roll(k, shift, 0) if shift else k
                    g_d = pltpu.roll(g, shift, 0) if shift else g
                    valid = (pos + dlt <= SUB - 1) if rev else (pos >= dlt)
                    e = jnp.exp(jnp.where(valid, g - g_d, 0.0))
                    val = jnp.sum(q * k_d * e, axis=-1, keepdims=True)
                    partner = (row + dlt) if rev else (row - dlt)
                    acc = acc + jnp.where((col == partner) & valid, val, 0.0)
                accs.append(acc)
        return jnp.stack(accs, axis=0)

    def prepare(self, ci, safe):
        rev = self.rev
        n_sub = CHUNK // SUB
        cc = self.order[ci]
        q_all, k_all, g_all, c, c_prev = self.gated[cc]
        width = g_all.shape[1]
        zero_row = jnp.zeros((1, width), F32)
        rows_of = lambda rws: jnp.concatenate([jnp.broadcast_to(r, (SUB, width)) for r in rws], axis=0)
        scan_blocks = list(range(n_sub - 1, -1, -1)) if rev else list(range(n_sub))
        c_final = c[scan_blocks[-1]]
        q_b = q_all * jnp.exp(g_all - rows_of(c_prev))
        k_end = k_all * jnp.exp(rows_of(c) - g_all)
        k_comb = []
        for sb in range(n_sub):
            mult = []
            for j in range(n_sub):
                if j == sb:
                    mult.append(jnp.where(safe, jnp.exp(jnp.minimum(c_prev[sb] - c[sb], SAFE_RANGE)), 0.0))
                elif scan_blocks.index(j) < scan_blocks.index(sb):
                    mult.append(jnp.exp(c_prev[sb] - c[j]))
                else:
                    mult.append(zero_row)
            k_comb.append(k_end * rows_of(mult))
        return dict(
            rows=slice(cc * CHUNK, (cc + 1) * CHUNK), q_b=q_b, k_comb=k_comb,
            q_g=q_b * rows_of([jnp.exp(cp) for cp in c_prev]),
            k_last=k_end * rows_of([jnp.exp(c_final - cj) for cj in c]),
            e_last=jnp.exp(c_final), v=self.v_ref[0, slice(cc * CHUNK, (cc + 1) * CHUNK), :], outs=[])

    def head(self, h, p, diag_block):
        rev = self.rev
        n_sub = CHUNK // SUB
        hs = self.heads[h]
        row = lax.broadcasted_iota(jnp.int32, (CHUNK, CHUNK), 0)
        col = lax.broadcasted_iota(jnp.int32, (CHUNK, CHUNK), 1)
        causal = (col >= row) if rev else (col <= row)
        blocks = [_dot_nt(p["q_b"][sb * SUB:(sb + 1) * SUB, hs], p["k_comb"][sb][:, hs]) for sb in range(n_sub)]
        a = jnp.where(causal, jnp.concatenate(blocks, axis=0), 0.0) + diag_block
        st = self.states[h]
        p["outs"].append(_dot(a, p["v"][:, hs]) + _dot_nt(p["q_g"][:, hs], st))
        self.states[h] = st * p["e_last"][:, hs] + lax.dot_general(
            p["v"][:, hs].astype(BF16), p["k_last"][:, hs].astype(BF16), (((0,), (0,)), ((), ())),
            preferred_element_type=F32)

    def store(self, p):
        self.o_ref[0, p["rows"], :] = jnp.concatenate(p["outs"], axis=-1).astype(self.o_ref.dtype)

    def finish(self):
        for h in range(self.n_heads):
            self.st_ref[h] = self.states[h]


def _hgrn_kernel(qf_ref, ff_ref, vf_ref, lbf_ref, qb_ref, fb_ref, vb_ref, lbb_ref, of_ref, ob_ref, sf_ref, sb_ref):
    @pl.when(pl.program_id(1) == 0)
    def _():
        sf_ref[...] = jnp.zeros_like(sf_ref)
        sb_ref[...] = jnp.zeros_like(sb_ref)

    dirs = [_HgrnDir(False, qf_ref, ff_ref, vf_ref, lbf_ref, of_ref, sf_ref),
            _HgrnDir(True, qb_ref, fb_ref, vb_ref, lbb_ref, ob_ref, sb_ref)]
    n_chunks, n_heads = dirs[0].n_chunks, dirs[0].n_heads
    rng = None
    for ci in range(n_chunks):
        for d in dirs:
            r = d.gate(d.order[ci])
            rng = r if rng is None else jnp.maximum(rng, r)
    safe = jnp.max(rng) <= SAFE_RANGE
    per_dir = n_chunks * n_heads
    diag = lax.cond(safe, lambda: jnp.zeros((2 * per_dir, CHUNK, CHUNK), F32),
                    lambda: jnp.concatenate([d.slow_diag() for d in dirs], axis=0))
    for ci in range(n_chunks):
        preps = [d.prepare(ci, safe) for d in dirs]
        for h in range(n_heads):
            for di, d in enumerate(dirs):
                d.head(h, preps[di], diag[di * per_dir + ci * n_heads + h])
        for di, d in enumerate(dirs):
            d.store(preps[di])
    for d in dirs:
        d.finish()


def _hgrn_scan(hraw, lb_rows_f, lb_rows_b):
    b, tt, n = hraw.shape
    w = lb_rows_f.shape[1]
    nt = tt // TM
    n_heads = w // HG_HEAD_DIM

    def dir_specs(rev):
        blk = lambda i: _scan_block(i, nt, rev)
        f_col = 2 if rev else 1
        return [pl.BlockSpec((1, TM, w), lambda bi, i: (bi, blk(i), 0)),
                pl.BlockSpec((1, TM, w), lambda bi, i: (bi, blk(i), f_col)),
                pl.BlockSpec((1, TM, w), lambda bi, i: (bi, blk(i), 3)),
                pl.BlockSpec((8, w), lambda bi, i: (0, 0))]

    out_spec = lambda rev: pl.BlockSpec((1, TM, w), lambda bi, i: (bi, _scan_block(i, nt, rev), 0))
    return pl.pallas_call(
        _hgrn_kernel,
        grid=(b, nt),
        in_specs=dir_specs(False) + dir_specs(True),
        out_specs=[out_spec(False), out_spec(True)],
        out_shape=[jax.ShapeDtypeStruct((b, tt, w), BF16)] * 2,
        scratch_shapes=[pltpu.VMEM((n_heads, HG_HEAD_DIM, HG_HEAD_DIM), F32)] * 2,
        compiler_params=_cparams(("parallel", "arbitrary")),
        name="hgrn_scan",
    )(hraw, hraw, hraw, lb_rows_f, hraw, hraw, hraw, lb_rows_b)


def _outproj0_kernel(na_ref, of_ref, ob_ref, gate_ref, x_ref, c_ref, m_ref, gain_ref, w_ref, o_ref, *, n_lat_tiles):
    na_w = na_ref.shape[2]
    y_na, y_hg = [], []
    for bi in range(BB):
        o = of_ref[bi].astype(F32) + ob_ref[bi].astype(F32)
        gate = _silu(gate_ref[bi])
        ys = []
        for h in range(o.shape[1] // HG_HEAD_DIM):
            hs = slice(h * HG_HEAD_DIM, (h + 1) * HG_HEAD_DIM)
            oh = o[:, hs]
            ms = jnp.mean(oh * oh, axis=-1, keepdims=True)
            ys.append(oh * lax.rsqrt(ms + EPS) * gain_ref[:, hs] * gate[:, hs])
        y_hg.append(jnp.concatenate(ys, axis=-1).astype(BF16))
        y_na.append(na_ref[bi])
    y = (jnp.dot(jnp.concatenate(y_na, axis=0), w_ref[0:na_w, :], preferred_element_type=F32)
         + jnp.dot(jnp.concatenate(y_hg, axis=0), w_ref[na_w:, :], preferred_element_type=F32))
    for bi in range(BB):
        o_ref[bi] = _stream0_rows(x_ref, c_ref, bi, n_lat_tiles) + m_ref[bi, 2:3, :] * y[bi * TM:(bi + 1) * TM]


def _outproj0(o_na, o_f, o_b, hraw, x, ctx, mod, gain, w):
    b, t_lat, d = x.shape
    tt = t_lat + ctx.shape[1]
    na_w = o_na.shape[2]
    hw = o_f.shape[2]
    n_lat_tiles = t_lat // TM
    return pl.pallas_call(
        functools.partial(_outproj0_kernel, n_lat_tiles=n_lat_tiles),
        grid=(b // BB, n_lat_tiles + 1),
        in_specs=[_tile_spec(na_w), _tile_spec(hw), _tile_spec(hw),
                  pl.BlockSpec((BB, TM, hw), lambda i, t: (i, t, 4))] + _stream0_specs(d, n_lat_tiles) + [
            _mod_spec(d, n_lat_tiles), _const_spec((1, hw)), _const_spec(w.shape)],
        out_specs=_tile_spec(d),
        out_shape=jax.ShapeDtypeStruct((b, tt, d), F32),
        compiler_params=_cparams(("parallel", "parallel")),
        name="outproj0",
    )(o_na, o_f, o_b, hraw, x, ctx, mod, gain, w)


HALO = 8
TM_EXT = TM + 2 * HALO


def _halo_specs(width, tt):
    r8 = TM // HALO
    last = tt // HALO - 1
    return [pl.BlockSpec((BB, HALO, width), lambda i, t: (i, jnp.maximum(t * r8 - 1, 0), 0)),
            pl.BlockSpec((BB, HALO, width), lambda i, t: (i, jnp.minimum((t + 1) * r8, last), 0))]


def _modulated_with_halo(h_ref, hp_ref, hn_ref, gain, m_ref, shift_row, scale_row):
    us, exts = [], []
    for bi in range(BB):
        shift = m_ref[bi, shift_row:shift_row + 1, :]
        scale = m_ref[bi, scale_row:scale_row + 1, :]
        u = _rms_mod(h_ref[bi], gain, shift, scale)
        us.append(u.astype(BF16))
        exts += [_rms_mod(hp_ref[bi], gain, shift, scale), u, _rms_mod(hn_ref[bi], gain, shift, scale)]
    return jnp.concatenate(us, axis=0), jnp.concatenate(exts, axis=0).astype(BF16)


def _conv3_ext(a_ext, cw, cb, n_lat_tiles):
    n = a_ext.shape[0] - 2 * HALO
    t = pl.program_id(1)
    has_prev = jnp.logical_and(t != 0, t != n_lat_tiles)
    has_next = jnp.logical_and(t != n_lat_tiles - 1, t != n_lat_tiles)
    a = a_ext[HALO:HALO + n]
    prev_row = jnp.where(has_prev, a_ext[HALO - 1:HALO], 0.0)
    next_row = jnp.where(has_next, a_ext[HALO + n:HALO + n + 1], 0.0)
    row = lax.broadcasted_iota(jnp.int32, (n, 1), 0)
    up = jnp.where(row == 0, prev_row, pltpu.roll(a, 1, 0))
    dn = jnp.where(row == n - 1, next_row, pltpu.roll(a, n - 1, 0))
    return cw[0:1, :] * up + cw[1:2, :] * a + cw[2:3, :] * dn + cb


def _ffn_up_kernel(h_ref, hp_ref, hn_ref, m_ref, gain_ref, w_ref, cw_ref, cb_ref, mid_ref, *, n_lat_tiles):
    u, u_ext = _modulated_with_halo(h_ref, hp_ref, hn_ref, gain_ref[...], m_ref, 3, 4)
    dff = mid_ref.shape[2]
    for lo, hi in _col_chunks(dff, 768):
        a_ext = jnp.dot(u_ext, w_ref[:, lo:hi], preferred_element_type=F32)
        v = jnp.dot(u, w_ref[:, dff + lo:dff + hi], preferred_element_type=F32)
        for bi in range(BB):
            c = _conv3_ext(a_ext[bi * TM_EXT:(bi + 1) * TM_EXT], cw_ref[:, lo:hi], cb_ref[:, lo:hi], n_lat_tiles)
            mid_ref[bi, :, lo:hi] = (_gelu_tanh(c) * v[bi * TM:(bi + 1) * TM]).astype(mid_ref.dtype)


def _ffn_up(hh, mod, gain, w, conv_w, conv_b, n_tiles, n_lat_tiles):
    b, tt_in, d = hh.shape
    dff = w.shape[1] // 2
    return pl.pallas_call(
        functools.partial(_ffn_up_kernel, n_lat_tiles=n_lat_tiles),
        grid=(b // BB, n_tiles),
        in_specs=[_tile_spec(d)] + _halo_specs(d, tt_in) + [
            _mod_spec(d, n_lat_tiles), _const_spec((1, d)), _const_spec(w.shape),
            _const_spec((3, dff)), _const_spec((1, dff))],
        out_specs=_tile_spec(dff),
        out_shape=jax.ShapeDtypeStruct((b, n_tiles * TM, dff), BF16),
        compiler_params=_cparams(("parallel", "parallel")),
        name="ffn_up",
    )(hh, hh, hh, mod, gain, w, conv_w, conv_b)


def _ffn_down_kernel(mid_ref, w_ref, h_ref, m_ref, o_ref):
    y = jnp.dot(jnp.concatenate([mid_ref[bi] for bi in range(BB)], axis=0), w_ref[...],
                preferred_element_type=F32)
    for bi in range(BB):
        o_ref[bi] = h_ref[bi] + m_ref[bi, 5:6, :] * y[bi * TM:(bi + 1) * TM]


def _ffn_down(mid, w, hh, mod, n_lat_tiles):
    b, tt, dff = mid.shape
    d = hh.shape[2]
    return pl.pallas_call(
        _ffn_down_kernel,
        grid=(b // BB, tt // TM),
        in_specs=[_tile_spec(dff), _const_spec(w.shape), _tile_spec(d), _mod_spec(d, n_lat_tiles)],
        out_specs=_tile_spec(d),
        out_shape=jax.ShapeDtypeStruct((b, tt, d), F32),
        compiler_params=_cparams(("parallel", "parallel")),
        name="ffn_down",
    )(mid, w, hh, mod)


def _inproj1_kernel(h_ref, hp_ref, hn_ref, m_ref, gain_ref, w_ref, cw_ref, cb_ref, dtb_ref, a_ref,
                    z_ref, xs_ref, bc_ref, dtp_ref, *, n_lat_tiles):
    u, u_ext = _modulated_with_halo(h_ref, hp_ref, hn_ref, gain_ref[...], m_ref, 0, 1)
    nz = z_ref.shape[2]
    nxs = xs_ref.shape[2]
    nx = nxs + bc_ref.shape[2]
    step = 512
    for lo, hi in _col_chunks(nz, step):
        z = _silu(jnp.dot(u, w_ref[:, lo:hi], preferred_element_type=F32))
        for bi in range(BB):
            z_ref[bi, :, lo:hi] = z[bi * TM:(bi + 1) * TM].astype(z_ref.dtype)
    for lo, hi in _col_chunks(nx, step):
        xbc_ext = jnp.dot(u_ext, w_ref[:, nz + lo:nz + hi], preferred_element_type=F32)
        for bi in range(BB):
            c = _silu(_conv3_ext(xbc_ext[bi * TM_EXT:(bi + 1) * TM_EXT], cw_ref[:, lo:hi], cb_ref[:, lo:hi],
                                 n_lat_tiles))
            if hi <= nxs:
                xs_ref[bi, :, lo:hi] = c.astype(xs_ref.dtype)
            else:
                bc_ref[bi, :, lo - nxs:hi - nxs] = c.astype(bc_ref.dtype)
    raw = jnp.dot(u, w_ref[:, nz + nx:], preferred_element_type=F32)
    dt = _softplus(raw + dtb_ref[...])
    la = dt * a_ref[...]
    for bi in range(BB):
        dtp_ref[bi, :, 0:2 * LANES] = dt[bi * TM:(bi + 1) * TM]
        dtp_ref[bi, :, 2 * LANES:4 * LANES] = la[bi * TM:(bi + 1) * TM]


def _inproj1(hh, mod, gain, w, conv_w, conv_b, dt_bias, a_neg, nz, nxs):
    b, tt, d = hh.shape
    nt = tt // TM
    n_lat_tiles = nt - 1
    nx = conv_w.shape[1]
    assert nxs % 512 == 0
    tile = _tile_spec
    return pl.pallas_call(
        functools.partial(_inproj1_kernel, n_lat_tiles=n_lat_tiles),
        grid=(b // BB, nt),
        in_specs=[tile(d)] + _halo_specs(d, tt) + [
            _mod_spec(d, n_lat_tiles), _const_spec((1, d)), _const_spec(w.shape),
            _const_spec((3, nx)), _const_spec((1, nx)),
            _const_spec((1, 2 * LANES)), _const_spec((1, 2 * LANES))],
        out_specs=[tile(nz), tile(nxs), tile(nx - nxs), tile(4 * LANES)],
        out_shape=[jax.ShapeDtypeStruct((b, tt, nz), BF16),
                   jax.ShapeDtypeStruct((b, tt, nxs), BF16),
                   jax.ShapeDtypeStruct((b, tt, nx - nxs), BF16),
                   jax.ShapeDtypeStruct((b, tt, 4 * LANES), F32)],
        compiler_params=_cparams(("parallel", "parallel")),
        name="inproj1",
    )(hh, hh, hh, mod, gain, w, conv_w, conv_b, dt_bias, a_neg)


class _SsdDir:
    def __init__(self, rev, x_ref, bc_ref, dt_ref, la_ref, e_ref, o_ref, s_ref):
        self.rev, self.x_ref, self.bc_ref, self.dt_ref, self.la_ref = rev, x_ref, bc_ref, dt_ref, la_ref
        self.e_ref, self.o_ref, self.s_ref = e_ref, o_ref, s_ref
        self.n_groups = s_ref.shape[0]
        self.gw = s_ref.shape[2]
        self.n_heads = x_ref.shape[2] // SSD_HEAD_DIM
        self.n_chunks = x_ref.shape[1] // CHUNK
        self.states = [s_ref[g] for g in range(self.n_groups)]
        self.tri = _tri_matrix(CHUNK, rev)
        rowc = lax.broadcasted_iota(jnp.int32, (CHUNK, self.gw), 0)
        pos = lax.broadcasted_iota(jnp.int32, (CHUNK, self.gw), 1) % SSD_HEAD_DIM
        self.on_diag = pos == rowc
        self.causal = (pos >= rowc) if rev else (pos <= rowc)

    def chunk(self, k):
        return self.n_chunks - 1 - k if self.rev else k

    def _stacked(self, v):
        lane = lax.broadcasted_iota(jnp.int32, (1, LANES), 1)
        p1, p2, p3 = _split3(v)
        return jnp.where(lane < self.n_heads, p1, jnp.where(lane < 2 * self.n_heads, p2,
                                                            jnp.where(lane < 3 * self.n_heads, p3, jnp.zeros_like(p1))))

    def scalars(self, cc):
        rs = slice(cc * CHUNK, (cc + 1) * CHUNK)
        return (self._stacked(_tri_cumsum(self.tri, self.la_ref[0, rs, :])), self._stacked(self.dt_ref[0, rs, :]))

    def prep(self, cc, g, scalars):
        cum_st, dt_st = scalars
        rs = slice(cc * CHUNK, (cc + 1) * CHUNK)
        gs = slice(g * self.gw, (g + 1) * self.gw)
        last = 0 if self.rev else CHUNK - 1
        cum = jnp.dot(cum_st, self.e_ref[:, gs], preferred_element_type=F32)
        dt = jnp.dot(dt_st, self.e_ref[:, gs], preferred_element_type=F32)
        cum_row = jnp.sum(jnp.where(self.on_diag, cum, 0.0), axis=0, keepdims=True)
        cum_last = cum[last:last + 1, :]
        x_dt = self.x_ref[0, rs, gs].astype(F32) * dt
        return dict(
            xw=(x_dt * jnp.exp(cum_last - cum)).astype(BF16),
            x_dt=x_dt.astype(BF16),
            decay=jnp.exp(jnp.where(self.causal, cum - cum_row, MASK_NEG)),
            e_cum=jnp.exp(cum),
            e_last=jnp.exp(cum_last))

    def matmuls(self, cc, g, p):
        blk = 256
        blk_head = lax.broadcasted_iota(jnp.int32, (1, blk), 1) // SSD_HEAD_DIM
        rs = slice(cc * CHUNK, (cc + 1) * CHUNK)
        gs = slice(g * self.gw, (g + 1) * self.gw)
        b_g = self.bc_ref[0, rs, g * SSD_STATE:(g + 1) * SSD_STATE]
        c_g = self.bc_ref[0, rs, (self.n_groups + g) * SSD_STATE:(self.n_groups + g + 1) * SSD_STATE]
        b_rep = jnp.concatenate([b_g] * (self.gw // CHUNK), axis=0)
        m_g = (p["decay"] * _dot_nt(c_g, b_rep)).astype(BF16)
        y_parts = []
        for j in range(self.gw // blk):
            x4 = p["x_dt"][:, j * blk:(j + 1) * blk]
            x_bd = jnp.concatenate(
                [jnp.where(blk_head == hh, x4, jnp.zeros_like(x4)) for hh in range(blk // SSD_HEAD_DIM)], axis=0)
            y_parts.append(jnp.dot(m_g[:, j * blk:(j + 1) * blk], x_bd, preferred_element_type=F32))
        s_g = self.states[g]
        y_g = jnp.concatenate(y_parts, axis=-1) + _dot(c_g, s_g) * p["e_cum"]
        self.states[g] = s_g * p["e_last"] + lax.dot_general(
            b_g, p["xw"], (((0,), (0,)), ((), ())), preferred_element_type=F32)
        self.o_ref[0, rs, gs] = y_g.astype(self.o_ref.dtype)

    def finish(self):
        for g in range(self.n_groups):
            self.s_ref[g] = self.states[g]


def _ssd_kernel(xf_ref, bcf_ref, dtf_ref, laf_ref, xb_ref, bcb_ref, dtb_ref, lab_ref, e_ref,
                of_ref, ob_ref, sf_ref, sb_ref):
    @pl.when(pl.program_id(1) == 0)
    def _():
        sf_ref[...] = jnp.zeros_like(sf_ref)
        sb_ref[...] = jnp.zeros_like(sb_ref)

    fwd = _SsdDir(False, xf_ref, bcf_ref, dtf_ref, laf_ref, e_ref, of_ref, sf_ref)
    bwd = _SsdDir(True, xb_ref, bcb_ref, dtb_ref, lab_ref, e_ref, ob_ref, sb_ref)
    groups = range(fwd.n_groups)
    steps = [(d, k) for k in range(fwd.n_chunks) for d in (fwd, bwd)]
    d0, k0 = steps[0]
    sc = d0.scalars(d0.chunk(k0))
    ready = [d0.prep(d0.chunk(k0), g, sc) for g in groups]
    for idx, (d, k) in enumerate(steps):
        nxt = steps[idx + 1] if idx + 1 < len(steps) else None
        if nxt is not None:
            nd, nk = nxt
            nsc = nd.scalars(nd.chunk(nk))
        following = []
        for g in groups:
            d.matmuls(d.chunk(k), g, ready[g])
            if nxt is not None:
                following.append(nd.prep(nd.chunk(nk), g, nsc))
        ready = following
    fwd.finish()
    bwd.finish()


def _ssd_scan(xs, bc, dtp, expand_mat):
    b, tt, width = xs.shape
    nt = tt // TM
    nbc = bc.shape[2]
    gw = width // SSD_GROUPS

    def dir_specs(rev):
        blk = lambda i: _scan_block(i, nt, rev)
        d_col = 1 if rev else 0
        return [pl.BlockSpec((1, TM, width), lambda bi, i: (bi, blk(i), 0)),
                pl.BlockSpec((1, TM, nbc), lambda bi, i: (bi, blk(i), 0)),
                pl.BlockSpec((1, TM, LANES), lambda bi, i: (bi, blk(i), d_col)),
                pl.BlockSpec((1, TM, LANES), lambda bi, i: (bi, blk(i), 2 + d_col))]

    out_spec = lambda rev: pl.BlockSpec((1, TM, width), lambda bi, i: (bi, _scan_block(i, nt, rev), 0))
    return pl.pallas_call(
        _ssd_kernel,
        grid=(b, nt),
        in_specs=dir_specs(False) + dir_specs(True) + [pl.BlockSpec(expand_mat.shape, lambda bi, i: (0, 0))],
        out_specs=[out_spec(False), out_spec(True)],
        out_shape=[jax.ShapeDtypeStruct((b, tt, width), BF16)] * 2,
        scratch_shapes=[pltpu.VMEM((SSD_GROUPS, SSD_STATE, gw), F32)] * 2,
        compiler_params=_cparams(("parallel", "arbitrary")),
        name="ssd_scan",
    )(xs, bc, dtp, dtp, xs, bc, dtp, dtp, expand_mat)


def _outproj1_kernel(yf_ref, yb_ref, xs_ref, sz_ref, h_ref, m_ref, d_ref, gain_ref, w_ref, o_ref):
    width = xs_ref.shape[2]
    gw = width // SSD_GROUPS
    normed = []
    for bi in range(BB):
        yy = ((yf_ref[bi].astype(F32) + yb_ref[bi].astype(F32) + d_ref[...] * xs_ref[bi].astype(F32))
              * sz_ref[bi].astype(F32))
        parts = []
        for g in range(SSD_GROUPS):
            gs = slice(g * gw, (g + 1) * gw)
            yg = yy[:, gs]
            ms = jnp.mean(yg * yg, axis=-1, keepdims=True)
            parts.append((yg * lax.rsqrt(ms + EPS) * gain_ref[:, gs]).astype(BF16))
        normed.append(jnp.concatenate(parts, axis=-1))
    y = jnp.dot(jnp.concatenate(normed, axis=0), w_ref[...], preferred_element_type=F32)
    for bi in range(BB):
        o_ref[bi] = h_ref[bi] + m_ref[bi, 2:3, :] * y[bi * TM:(bi + 1) * TM]


def _outproj1(y_f, y_b, xs, sz, hh, mod, d_skip, gain, w, n_lat_tiles):
    b, _, d = hh.shape
    width = xs.shape[2]
    tile = _tile_spec
    return pl.pallas_call(
        _outproj1_kernel,
        grid=(b // BB, n_lat_tiles),
        in_specs=[tile(width), tile(width), tile(width), tile(width), tile(d),
                  _mod_spec(d, n_lat_tiles), _const_spec((1, width)), _const_spec((1, width)),
                  _const_spec(w.shape)],
        out_specs=tile(d),
        out_shape=jax.ShapeDtypeStruct((b, n_lat_tiles * TM, d), F32),
        compiler_params=_cparams(("parallel", "parallel")),
        name="outproj1",
    )(y_f, y_b, xs, sz, hh, mod, d_skip, gain, w)


def _mod_rows(mods_l, b, d):
    six = mods_l.reshape(mods_l.shape[0], 6, d)
    lat = six[:b]
    ctx = jnp.broadcast_to(six[b:b + 1], (b, 6, d))
    both = jnp.stack([lat, ctx], axis=1)
    return jnp.pad(both, ((0, 0), (0, 0), (0, 2), (0, 0)))


def kernel(x, c, ctx, c_ctx, w_mod, b_mod, norm_mix, norm_ffn, ffn_w_up, ffn_conv_w, ffn_conv_b, ffn_w_down,
           hy_w_in, hy_w_out, na_q_gain, na_k_gain, na_rpb, hg_out_gain, hg_lb_fwd, hg_lb_bwd, ssd_w_in,
           ssd_conv_w, ssd_conv_b, ssd_dt_bias_fwd, ssd_dt_bias_bwd, ssd_a_log_fwd, ssd_a_log_bwd, ssd_d,
           ssd_norm_gain, ssd_w_out):
    b, t_lat, d = x.shape
    l_ctx = ctx.shape[1]
    assert l_ctx == TM and t_lat % TM == 0 and w_mod.shape[0] == 2 and b % BB == 0
    n_lat_tiles = t_lat // TM
    nt = n_lat_tiles + 1

    rows = ((b + 1 + 7) // 8) * 8
    cond = jnp.concatenate([c, c_ctx[None], jnp.zeros((rows - b - 1, d), F32)], axis=0)
    mods = _modulation(cond, w_mod, b_mod)
    mod0 = _mod_rows(mods[0], b, d)
    mod1 = _mod_rows(mods[1], b, d)
    row = lambda v: v.reshape(1, -1).astype(F32)

    hg_w = hg_lb_fwd.shape[1]
    na_w = hy_w_out.shape[1] - hg_w
    n_na_heads = na_w // NA_HEAD_DIM
    qg = row(jnp.tile(na_q_gain[0], n_na_heads)) * (NA_HEAD_DIM ** -0.5)
    kg = row(jnp.tile(na_k_gain[0], n_na_heads))
    qkv, hraw = _inproj0(x, ctx, mod0, row(norm_mix[0]), hy_w_in[0].astype(BF16), qg, kg)
    bias = _na_bias_table(na_rpb[0], t_lat // GRID_W)
    o_na = _na_attention(qkv, bias, t_lat)

    def lb_rows(lb_param):
        lb = jnp.cumsum(jax.nn.softmax(lb_param.astype(F32), axis=0), axis=0)[0]
        return jnp.pad(jnp.stack([lb, 1.0 - lb], axis=0), ((0, 6), (0, 0)))

    o_f, o_b = _hgrn_scan(hraw, lb_rows(hg_lb_fwd), lb_rows(hg_lb_bwd))
    hg_gain = row(jnp.tile(hg_out_gain[0], hg_w // HG_HEAD_DIM))
    h1 = _outproj0(o_na, o_f, o_b, hraw, x, ctx, mod0, hg_gain, hy_w_out[0].astype(BF16))

    mid0 = _ffn_up(h1, mod0, row(norm_ffn[0]), ffn_w_up[0].astype(BF16), ffn_conv_w[0], row(ffn_conv_b[0]),
                   nt, n_lat_tiles)
    h2 = _ffn_down(mid0, ffn_w_down[0].astype(BF16), h1, mod0, n_lat_tiles)

    n_heads = ssd_d.shape[1]
    inner = ssd_w_out.shape[1]
    nxbc = ssd_conv_w.shape[2]
    w1 = ssd_w_in[0]
    rep = LANES // n_heads
    w_dtf = jnp.tile(w1[:, inner + nxbc:inner + nxbc + n_heads], (1, rep))
    w_dtb = jnp.tile(w1[:, inner + nxbc + n_heads:], (1, rep))
    w1p = jnp.concatenate([w1[:, :inner + nxbc], w_dtf, w_dtb], axis=1).astype(BF16)
    dt_bias = row(jnp.concatenate([jnp.tile(ssd_dt_bias_fwd[0], rep), jnp.tile(ssd_dt_bias_bwd[0], rep)]))
    a_neg = row(jnp.concatenate([jnp.tile(-jnp.exp(ssd_a_log_fwd[0].astype(F32)), rep),
                                 jnp.tile(-jnp.exp(ssd_a_log_bwd[0].astype(F32)), rep)]))
    z, xs, bc, dtp = _inproj1(h2, mod1, row(norm_mix[1]), w1p, ssd_conv_w[0], row(ssd_conv_b[0]), dt_bias,
                              a_neg, inner, inner)

    lane = np.arange(LANES)[:, None]
    colh = (np.arange(inner) // SSD_HEAD_DIM)[None, :]
    expand_mat = jnp.asarray(((lane % n_heads == colh) & (lane < 3 * n_heads)).astype(np.float32), dtype=BF16)
    y_f, y_b = _ssd_scan(xs, bc, dtp, expand_mat)
    d_skip = row(jnp.repeat(ssd_d[0], SSD_HEAD_DIM))
    h3 = _outproj1(y_f, y_b, xs, z, h2, mod1, d_skip, row(ssd_norm_gain[0]), ssd_w_out[0].astype(BF16),
                   n_lat_tiles)

    mid1 = _ffn_up(h3, mod1, row(norm_ffn[1]), ffn_w_up[1].astype(BF16), ffn_conv_w[1], row(ffn_conv_b[1]),
                   n_lat_tiles, n_lat_tiles)
    return _ffn_down(mid1, ffn_w_down[1].astype(BF16), h3, mod1, n_lat_tiles)
```

```python
import functools
import math

import numpy as np
import jax
import jax.numpy as jnp
from jax import lax
from jax.experimental import pallas as pl
from jax.experimental.pallas import tpu as pltpu

F32 = jnp.float32
BF16 = jnp.bfloat16
EPS = 1e-6

GRID_W = 64
NA_HEAD_DIM = 64
WIN_ROWS = 8
WIN_COLS = 16
NA_ROW_BLOCK = 4
HG_HEAD_DIM = 128
SSD_HEAD_DIM = 64
SSD_GROUPS = 4
SSD_STATE = 128
CHUNK = 64
SUB = 16
TM = 256
BB = 2
LANES = 128
MASK_NEG = -1e30
SAFE_RANGE = 60.0

VMEM_LIMIT = 56 * 1024 * 1024


def _cparams(sem):
    return pltpu.CompilerParams(dimension_semantics=sem, vmem_limit_bytes=VMEM_LIMIT)


def _dot(a, b):
    return jnp.dot(a.astype(BF16), b.astype(BF16), preferred_element_type=F32)


def _dot_nt(a, b):
    return lax.dot_general(a.astype(BF16), b.astype(BF16), (((1,), (1,)), ((), ())),
                           preferred_element_type=F32)


def _sigmoid(x):
    return 1.0 / (1.0 + jnp.exp(-x))


def _silu(x):
    h = 0.5 * x
    return h + h * jnp.tanh(h)


def _softplus(x):
    return jnp.maximum(x, 0.0) + jnp.log1p(jnp.exp(-jnp.abs(x)))


def _gelu_tanh(x):
    c = math.sqrt(2.0 / math.pi)
    return 0.5 * x * (1.0 + jnp.tanh(c * (x + 0.044715 * (x * x * x))))


def _rms_mod(x, gain, shift, scale):
    ms = jnp.mean(x * x, axis=-1, keepdims=True)
    return (x * lax.rsqrt(ms + EPS) * gain) * (1.0 + scale) + shift


def _split3(v):
    p1 = v.astype(BF16)
    r1 = v - p1.astype(F32)
    p2 = r1.astype(BF16)
    r2 = r1 - p2.astype(F32)
    return p1, p2, r2.astype(BF16)


def _tri_cumsum(tri, v):
    p1, p2, p3 = _split3(v)
    d = lambda p: jnp.dot(tri, p, preferred_element_type=F32)
    return d(p1) + d(p2) + d(p3)


def _mod_spec(d, n_lat_tiles):
    return pl.BlockSpec((BB, None, 8, d), lambda i, t: (i, t // n_lat_tiles, 0, 0))


def _tile_spec(width):
    return pl.BlockSpec((BB, TM, width), lambda i, t: (i, t, 0))


def _const_spec(shape):
    return pl.BlockSpec(shape, lambda i, t: (0,) * len(shape))


def _stream0_specs(d, n_lat_tiles):
    return [pl.BlockSpec((BB, TM, d), lambda i, t: (i, jnp.minimum(t, n_lat_tiles - 1), 0)),
            pl.BlockSpec((BB, TM, d), lambda i, t: (i, 0, 0))]


def _stream0_rows(x_ref, c_ref, bi, n_lat_tiles):
    return jnp.where(pl.program_id(1) == n_lat_tiles, c_ref[bi], x_ref[bi])


def _col_chunks(width, step):
    return [(lo, min(lo + step, width)) for lo in range(0, width, step)]


def _tri_matrix(n, rev):
    r = lax.broadcasted_iota(jnp.int32, (n, n), 0)
    c = lax.broadcasted_iota(jnp.int32, (n, n), 1)
    return jnp.where((c >= r) if rev else (c <= r), 1.0, 0.0).astype(BF16)


def _mod_kernel(s_ref, w_ref, b_ref, o_ref):
    s = _silu(s_ref[...])
    o_ref[0] = _dot(s, w_ref[0]) + b_ref[0]


def _modulation(cond, w_mod, b_mod):
    depth, d, n = w_mod.shape
    rows = cond.shape[0]
    tn = 1536
    return pl.pallas_call(
        _mod_kernel,
        grid=(depth, n // tn),
        in_specs=[pl.BlockSpec((rows, d), lambda l, j: (0, 0)),
                  pl.BlockSpec((1, d, tn), lambda l, j: (l, 0, j)),
                  pl.BlockSpec((1, 1, tn), lambda l, j: (l, 0, j))],
        out_specs=pl.BlockSpec((1, rows, tn), lambda l, j: (l, 0, j)),
        out_shape=jax.ShapeDtypeStruct((depth, rows, n), F32),
        compiler_params=_cparams(("parallel", "parallel")),
        name="modulation",
    )(cond, w_mod, b_mod.reshape(depth, 1, n))


def _inproj0_kernel(x_ref, c_ref, m_ref, gain_ref, w_ref, qg_ref, kg_ref, qkv_ref, hraw_ref, *, n_lat_tiles):
    u = jnp.concatenate(
        [_rms_mod(_stream0_rows(x_ref, c_ref, bi, n_lat_tiles), gain_ref[...], m_ref[bi, 0:1, :],
                  m_ref[bi, 1:2, :]).astype(BF16) for bi in range(BB)], axis=0)
    lo = lax.broadcasted_iota(jnp.int32, (1, LANES), 1) < NA_HEAD_DIM
    na_w = qg_ref.shape[1]

    def head_norm(y, g_ref):
        outs = []
        for c in range(na_w // LANES):
            yc = y[:, c * LANES:(c + 1) * LANES]
            sq = yc * yc
            s_lo = jnp.sum(jnp.where(lo, sq, 0.0), axis=-1, keepdims=True)
            s_hi = jnp.sum(jnp.where(lo, 0.0, sq), axis=-1, keepdims=True)
            inv = jnp.where(lo, lax.rsqrt(s_lo / NA_HEAD_DIM + EPS), lax.rsqrt(s_hi / NA_HEAD_DIM + EPS))
            outs.append(yc * inv * g_ref[:, c * LANES:(c + 1) * LANES])
        return jnp.concatenate(outs, axis=-1)

    def proj(lo_col, width):
        return jnp.dot(u, w_ref[:, lo_col:lo_col + width], preferred_element_type=F32)

    def store(ref, lo_col, y):
        for bi in range(BB):
            ref[bi, :, lo_col:lo_col + y.shape[1]] = y[bi * TM:(bi + 1) * TM].astype(ref.dtype)

    store(qkv_ref, 0, head_norm(proj(0, na_w), qg_ref))
    store(qkv_ref, na_w, head_norm(proj(na_w, na_w), kg_ref))
    store(qkv_ref, 2 * na_w, proj(2 * na_w, na_w))
    for lo_col, hi_col in _col_chunks(hraw_ref.shape[2], 512):
        store(hraw_ref, lo_col, proj(3 * na_w + lo_col, hi_col - lo_col))


def _inproj0(x, ctx, mod, gain, w, qg, kg):
    b, t_lat, d = x.shape
    n = w.shape[1]
    na_w = qg.shape[1]
    n_hg = n - 3 * na_w
    n_lat_tiles = t_lat // TM
    tt = t_lat + ctx.shape[1]
    return pl.pallas_call(
        functools.partial(_inproj0_kernel, n_lat_tiles=n_lat_tiles),
        grid=(b // BB, n_lat_tiles + 1),
        in_specs=_stream0_specs(d, n_lat_tiles) + [
            _mod_spec(d, n_lat_tiles), _const_spec((1, d)), _const_spec((d, n)),
            _const_spec((1, na_w)), _const_spec((1, na_w))],
        out_specs=[_tile_spec(3 * na_w), _tile_spec(n_hg)],
        out_shape=[jax.ShapeDtypeStruct((b, tt, 3 * na_w), BF16),
                   jax.ShapeDtypeStruct((b, tt, n_hg), F32)],
        compiler_params=_cparams(("parallel", "parallel")),
        name="inproj0",
    )(x, ctx, mod, gain, w, qg, kg)


def _na_kernel(q_ref, k_ref, v_ref, bias_ref, o_ref, *, t_lat, rows):
    tt = q_ref.shape[1]
    lane = lax.broadcasted_iota(jnp.int32, (1, LANES), 1)
    head_mask = [lane < NA_HEAD_DIM, lane >= NA_HEAD_DIM]
    kc = k_ref[0, t_lat:tt, :]
    vc = v_ref[0, t_lat:tt, :]
    _, ku, u0s, _, case_of_block = _na_block_plan(rows)
    n_loc = ku * GRID_W
    n_q = NA_ROW_BLOCK * GRID_W

    qc = q_ref[0, t_lat:tt, :]
    oc = None
    for hm in head_mask:
        s = _dot_nt(jnp.where(hm, qc, jnp.zeros_like(qc)), kc)
        p = jnp.exp(s - jnp.max(s, axis=-1, keepdims=True))
        o = _dot(p, vc) / jnp.sum(p, axis=-1, keepdims=True)
        oc = o if oc is None else jnp.where(hm, o, oc)
    o_ref[0, t_lat:tt, :] = oc.astype(o_ref.dtype)

    n_blocks = rows // NA_ROW_BLOCK
    group = 2 if n_blocks % 2 == 0 else 1

    def blocks_body(it, carry):
        loaded, scores = [], []
        for j in range(group):
            i = it * group + j
            u0 = u0s[0]
            case = case_of_block[0]
            for bi in range(1, len(u0s)):
                u0 = jnp.where(i == bi, u0s[bi], u0)
                case = jnp.where(i == bi, case_of_block[bi], case)
            q_rows = pl.ds(pl.multiple_of(i * n_q, n_q), n_q)
            q_i = q_ref[0, q_rows, :]
            start = pl.multiple_of(u0 * GRID_W, GRID_W)
            k_loc = k_ref[0, pl.ds(start, n_loc), :]
            loaded.append((q_rows, v_ref[0, pl.ds(start, n_loc), :]))
            for hi, hm in enumerate(head_mask):
                qh = jnp.where(hm, q_i, jnp.zeros_like(q_i))
                scores.append((_dot_nt(qh, k_loc) + bias_ref[hi, case], _dot_nt(qh, kc)))
        probs = []
        for s_loc, s_ctx in scores:
            m = jnp.maximum(jnp.max(s_loc, axis=-1, keepdims=True), jnp.max(s_ctx, axis=-1, keepdims=True))
            p_loc = jnp.exp(s_loc - m)
            p_ctx = jnp.exp(s_ctx - m)
            den = jnp.sum(p_loc, axis=-1, keepdims=True) + jnp.sum(p_ctx, axis=-1, keepdims=True)
            probs.append((p_loc.astype(BF16), p_ctx.astype(BF16), den))
        for j, (q_rows, v_loc) in enumerate(loaded):
            out = None
            for hi, hm in enumerate(head_mask):
                p_loc, p_ctx, den = probs[j * len(head_mask) + hi]
                o = (jnp.dot(p_loc, v_loc, preferred_element_type=F32)
                     + jnp.dot(p_ctx, vc, preferred_element_type=F32)) / den
                out = o if out is None else jnp.where(hm, o, out)
            o_ref[0, q_rows, :] = out.astype(o_ref.dtype)
        return carry

    lax.fori_loop(0, n_blocks // group, blocks_body, 0)


def _na_block_plan(rows):
    kr = min(WIN_ROWS, rows)
    ku = min(kr + NA_ROW_BLOCK - 1, rows)
    u0s, patterns, case_of_block = [], [], []
    for i in range(rows // NA_ROW_BLOCK):
        u0 = int(np.clip(NA_ROW_BLOCK * i - kr // 2, 0, rows - ku))
        rel = []
        for a in range(NA_ROW_BLOCK):
            r = NA_ROW_BLOCK * i + a
            r0 = int(np.clip(r - kr // 2, 0, rows - kr))
            assert u0 <= r0 and r0 + kr <= u0 + ku
            rel.append((r0 - u0, u0 - r))
        u0s.append(u0)
        if tuple(rel) not in patterns:
            patterns.append(tuple(rel))
        case_of_block.append(patterns.index(tuple(rel)))
    return kr, ku, u0s, patterns, case_of_block


def _na_bias_table(rpb, rows):
    kr, ku, _, patterns, _ = _na_block_plan(rows)
    q = np.arange(GRID_W)
    kcol = np.arange(GRID_W)
    ws = np.clip(q - WIN_COLS // 2, 0, GRID_W - WIN_COLS)
    in_win = (kcol[None, :] >= ws[:, None]) & (kcol[None, :] < ws[:, None] + WIN_COLS)
    dc = np.clip(kcol[None, :] - q[:, None] + WIN_COLS - 1, 0, 2 * WIN_COLS - 2)
    u = np.arange(ku)
    n_dr = 2 * WIN_ROWS - 1
    dr = np.full((len(patterns), NA_ROW_BLOCK, ku), n_dr, np.int32)
    for c, rel in enumerate(patterns):
        for a, (r0_rel, u0_minus_r) in enumerate(rel):
            row_ok = (u >= r0_rel) & (u < r0_rel + kr)
            dr[c, a] = np.where(row_ok, u0_minus_r + u + WIN_ROWS - 1, n_dr)
    h = rpb.shape[0]
    tiles = jnp.where(in_win[None, None], rpb.astype(F32)[:, :, dc], MASK_NEG)
    tiles = jnp.concatenate([tiles, jnp.full((h, 1, GRID_W, GRID_W), MASK_NEG, F32)], axis=1)
    tbl = jnp.take(tiles, jnp.asarray(dr.reshape(-1)), axis=1)
    tbl = tbl.reshape(h, len(patterns), NA_ROW_BLOCK, ku, GRID_W, GRID_W)
    return tbl.transpose(0, 1, 2, 4, 3, 5).reshape(h, len(patterns), NA_ROW_BLOCK * GRID_W, ku * GRID_W)


def _na_attention(qkv, bias, t_lat):
    b, tt, w3 = qkv.shape
    na_w = w3 // 3
    ncol = na_w // LANES
    rows = t_lat // GRID_W
    bias_block = (2,) + bias.shape[1:]
    kern = functools.partial(_na_kernel, t_lat=t_lat, rows=rows)
    return pl.pallas_call(
        kern,
        grid=(b, ncol),
        in_specs=[pl.BlockSpec((1, tt, LANES), lambda i, p: (i, 0, p)),
                  pl.BlockSpec((1, tt, LANES), lambda i, p: (i, 0, ncol + p)),
                  pl.BlockSpec((1, tt, LANES), lambda i, p: (i, 0, 2 * ncol + p)),
                  pl.BlockSpec(bias_block, lambda i, p: (p, 0, 0, 0))],
        out_specs=pl.BlockSpec((1, tt, LANES), lambda i, p: (i, 0, p)),
        out_shape=jax.ShapeDtypeStruct((b, tt, na_w), BF16),
        compiler_params=_cparams(("parallel", "parallel")),
        name="na_attention",
    )(qkv, qkv, qkv, bias)


def _scan_block(i, nt, rev):
    if rev:
        return jnp.where(i == 0, nt - 1, nt - 1 - i)
    return jnp.where(i == 0, nt - 1, i - 1)


class _HgrnDir:
    def __init__(self, rev, q_ref, f_ref, v_ref, lb_ref, o_ref, st_ref):
        self.rev, self.q_ref, self.f_ref, self.v_ref, self.o_ref, self.st_ref = rev, q_ref, f_ref, v_ref, o_ref, st_ref
        self.n_heads = st_ref.shape[0]
        self.lb = lb_ref[0:1, :]
        self.one_m_lb = lb_ref[1:2, :]
        self.n_chunks = q_ref.shape[1] // CHUNK
        self.order = list(range(self.n_chunks - 1, -1, -1)) if rev else list(range(self.n_chunks))
        self.heads = [slice(h * HG_HEAD_DIM, (h + 1) * HG_HEAD_DIM) for h in range(self.n_heads)]
        self.states = [st_ref[h] for h in range(self.n_heads)]
        self.gated = {}

    def gate(self, cc):
        rev, lb, one_m_lb, q_ref, f_ref = self.rev, self.lb, self.one_m_lb, self.q_ref, self.f_ref
        n_sub = CHUNK // SUB
        tri = _tri_matrix(CHUNK, rev)
        zero_row = jnp.zeros((1, lb.shape[1]), F32)
        scan_blocks = list(range(n_sub - 1, -1, -1)) if rev else list(range(n_sub))
        rng = None
        rs = slice(cc * CHUNK, (cc + 1) * CHUNK)
        x = f_ref[0, rs, :]
        e = jnp.exp(-jnp.abs(x))
        r = 1.0 / (1.0 + e)
        x_pos = x >= 0.0
        f = jnp.where(x_pos, 1.0 + lb * e, lb + e) * r
        log_f = jnp.where(f > 0.0, jnp.log(f), x)
        k_all = one_m_lb * jnp.where(x_pos, e, 1.0) * r
        q_all = _silu(q_ref[0, rs, :])
        g_all = _tri_cumsum(tri, log_f)
        c = [g_all[j * SUB:j * SUB + 1, :] if rev else g_all[(j + 1) * SUB - 1:(j + 1) * SUB, :]
             for j in range(n_sub)]
        c_prev = [zero_row] * n_sub
        for before, after in zip(scan_blocks[:-1], scan_blocks[1:]):
            c_prev[after] = c[before]
        for j in range(n_sub):
            d = c_prev[j] - c[j]
            rng = d if rng is None else jnp.maximum(rng, d)
        self.gated[cc] = (q_all, k_all, g_all, c, c_prev)
        return rng

    def slow_diag(self):
        rev = self.rev
        row = lax.broadcasted_iota(jnp.int32, (CHUNK, CHUNK), 0)
        col = lax.broadcasted_iota(jnp.int32, (CHUNK, CHUNK), 1)
        rowv = lax.broadcasted_iota(jnp.int32, (CHUNK, 1), 0)
        pos = rowv % SUB
        accs = []
        for cc in self.order:
            q_all, k_all, g_all = self.gated[cc][:3]
            for hs in self.heads:
                q, k, g = q_all[:, hs], k_all[:, hs], g_all[:, hs]
                acc = jnp.zeros((CHUNK, CHUNK), F32)
                for dlt in range(SUB):
                    shift = (CHUNK - dlt) % CHUNK if rev else dlt
                    k_d = pltpu.roll(k, shift, 0) if shift else k
                    g_d = pltpu.roll(g, shift, 0) if shift else g
                    valid = (pos + dlt <= SUB - 1) if rev else (pos >= dlt)
                    e = jnp.exp(jnp.where(valid, g - g_d, 0.0))
                    val = jnp.sum(q * k_d * e, axis=-1, keepdims=True)
                    partner = (row + dlt) if rev else (row - dlt)
                    acc = acc + jnp.where((col == partner) & valid, val, 0.0)
                accs.append(acc)
        return jnp.stack(accs, axis=0)

    def prepare(self, ci, safe):
        rev = self.rev
        n_sub = CHUNK // SUB
        cc = self.order[ci]
        q_all, k_all, g_all, c, c_prev = self.gated[cc]
        width = g_all.shape[1]
        zero_row = jnp.zeros((1, width), F32)
        rows_of = lambda rws: jnp.concatenate([jnp.broadcast_to(r, (SUB, width)) for r in rws], axis=0)
        scan_blocks = list(range(n_sub - 1, -1, -1)) if rev else list(range(n_sub))
        c_final = c[scan_blocks[-1]]
        q_b = q_all * jnp.exp(g_all - rows_of(c_prev))
        k_end = k_all * jnp.exp(rows_of(c) - g_all)
        k_comb = []
        for sb in range(n_sub):
            mult = []
            for j in range(n_sub):
                if j == sb:
                    mult.append(jnp.where(safe, jnp.exp(jnp.minimum(c_prev[sb] - c[sb], SAFE_RANGE)), 0.0))
                elif scan_blocks.index(j) < scan_blocks.index(sb):
                    mult.append(jnp.exp(c_prev[sb] - c[j]))
                else:
                    mult.append(zero_row)
            k_comb.append(k_end * rows_of(mult))
        return dict(
            rows=slice(cc * CHUNK, (cc + 1) * CHUNK), q_b=q_b, k_comb=k_comb,
            q_g=q_b * rows_of([jnp.exp(cp) for cp in c_prev]),
            k_last=k_end * rows_of([jnp.exp(c_final - cj) for cj in c]),
            e_last=jnp.exp(c_final), v=self.v_ref[0, slice(cc * CHUNK, (cc + 1) * CHUNK), :], outs=[])

    def scores(self, h, p, diag_block):
        rev = self.rev
        n_sub = CHUNK // SUB
        hs = self.heads[h]
        row = lax.broadcasted_iota(jnp.int32, (CHUNK, CHUNK), 0)
        col = lax.broadcasted_iota(jnp.int32, (CHUNK, CHUNK), 1)
        causal = (col >= row) if rev else (col <= row)
        blocks = [_dot_nt(p["q_b"][sb * SUB:(sb + 1) * SUB, hs], p["k_comb"][sb][:, hs]) for sb in range(n_sub)]
        return jnp.where(causal, jnp.concatenate(blocks, axis=0), 0.0) + diag_block

    def values(self, h, p, a):
        hs = self.heads[h]
        st = self.states[h]
        p["outs"].append(_dot(a, p["v"][:, hs]) + _dot_nt(p["q_g"][:, hs], st))
        self.states[h] = st * p["e_last"][:, hs] + lax.dot_general(
            p["v"][:, hs].astype(BF16), p["k_last"][:, hs].astype(BF16), (((0,), (0,)), ((), ())),
            preferred_element_type=F32)

    def store(self, p):
        self.o_ref[0, p["rows"], :] = jnp.concatenate(p["outs"], axis=-1).astype(self.o_ref.dtype)

    def finish(self):
        for h in range(self.n_heads):
            self.st_ref[h] = self.states[h]


def _hgrn_kernel(qf_ref, ff_ref, vf_ref, lbf_ref, qb_ref, fb_ref, vb_ref, lbb_ref, of_ref, ob_ref, sf_ref, sb_ref):
    @pl.when(pl.program_id(1) == 0)
    def _():
        sf_ref[...] = jnp.zeros_like(sf_ref)
        sb_ref[...] = jnp.zeros_like(sb_ref)

    dirs = [_HgrnDir(False, qf_ref, ff_ref, vf_ref, lbf_ref, of_ref, sf_ref),
            _HgrnDir(True, qb_ref, fb_ref, vb_ref, lbb_ref, ob_ref, sb_ref)]
    n_chunks, n_heads = dirs[0].n_chunks, dirs[0].n_heads
    rng = None
    for ci in range(n_chunks):
        for d in dirs:
            r = d.gate(d.order[ci])
            rng = r if rng is None else jnp.maximum(rng, r)
    safe = jnp.max(rng) <= SAFE_RANGE
    per_dir = n_chunks * n_heads
    diag = lax.cond(safe, lambda: jnp.zeros((2 * per_dir, CHUNK, CHUNK), F32),
                    lambda: jnp.concatenate([d.slow_diag() for d in dirs], axis=0))
    for ci in range(n_chunks):
        preps = [d.prepare(ci, safe) for d in dirs]
        chains = [(h, di, d) for h in range(n_heads) for di, d in enumerate(dirs)]
        att = [d.scores(h, preps[di], diag[di * per_dir + ci * n_heads + h]) for h, di, d in chains]
        for (h, di, d), a in zip(chains, att):
            d.values(h, preps[di], a)
        for di, d in enumerate(dirs):
            d.store(preps[di])
    for d in dirs:
        d.finish()


def _hgrn_scan(hraw, lb_rows_f, lb_rows_b):
    b, tt, n = hraw.shape
    w = lb_rows_f.shape[1]
    nt = tt // TM
    n_heads = w // HG_HEAD_DIM

    def dir_specs(rev):
        blk = lambda i: _scan_block(i, nt, rev)
        f_col = 2 if rev else 1
        return [pl.BlockSpec((1, TM, w), lambda bi, i: (bi, blk(i), 0)),
                pl.BlockSpec((1, TM, w), lambda bi, i: (bi, blk(i), f_col)),
                pl.BlockSpec((1, TM, w), lambda bi, i: (bi, blk(i), 3)),
                pl.BlockSpec((8, w), lambda bi, i: (0, 0))]

    out_spec = lambda rev: pl.BlockSpec((1, TM, w), lambda bi, i: (bi, _scan_block(i, nt, rev), 0))
    return pl.pallas_call(
        _hgrn_kernel,
        grid=(b, nt),
        in_specs=dir_specs(False) + dir_specs(True),
        out_specs=[out_spec(False), out_spec(True)],
        out_shape=[jax.ShapeDtypeStruct((b, tt, w), BF16)] * 2,
        scratch_shapes=[pltpu.VMEM((n_heads, HG_HEAD_DIM, HG_HEAD_DIM), F32)] * 2,
        compiler_params=_cparams(("parallel", "arbitrary")),
        name="hgrn_scan",
    )(hraw, hraw, hraw, lb_rows_f, hraw, hraw, hraw, lb_rows_b)


def _outproj0_kernel(na_ref, of_ref, ob_ref, gate_ref, x_ref, c_ref, m_ref, gain_ref, w_ref, o_ref, *, n_lat_tiles):
    na_w = na_ref.shape[2]
    y_na, y_hg = [], []
    for bi in range(BB):
        o = of_ref[bi].astype(F32) + ob_ref[bi].astype(F32)
        gate = _silu(gate_ref[bi])
        ys = []
        for h in range(o.shape[1] // HG_HEAD_DIM):
            hs = slice(h * HG_HEAD_DIM, (h + 1) * HG_HEAD_DIM)
            oh = o[:, hs]
            ms = jnp.mean(oh * oh, axis=-1, keepdims=True)
            ys.append(oh * lax.rsqrt(ms + EPS) * gain_ref[:, hs] * gate[:, hs])
        y_hg.append(jnp.concatenate(ys, axis=-1).astype(BF16))
        y_na.append(na_ref[bi])
    y = (jnp.dot(jnp.concatenate(y_na, axis=0), w_ref[0:na_w, :], preferred_element_type=F32)
         + jnp.dot(jnp.concatenate(y_hg, axis=0), w_ref[na_w:, :], preferred_element_type=F32))
    for bi in range(BB):
        o_ref[bi] = _stream0_rows(x_ref, c_ref, bi, n_lat_tiles) + m_ref[bi, 2:3, :] * y[bi * TM:(bi + 1) * TM]


def _outproj0(o_na, o_f, o_b, hraw, x, ctx, mod, gain, w):
    b, t_lat, d = x.shape
    tt = t_lat + ctx.shape[1]
    na_w = o_na.shape[2]
    hw = o_f.shape[2]
    n_lat_tiles = t_lat // TM
    return pl.pallas_call(
        functools.partial(_outproj0_kernel, n_lat_tiles=n_lat_tiles),
        grid=(b // BB, n_lat_tiles + 1),
        in_specs=[_tile_spec(na_w), _tile_spec(hw), _tile_spec(hw),
                  pl.BlockSpec((BB, TM, hw), lambda i, t: (i, t, 4))] + _stream0_specs(d, n_lat_tiles) + [
            _mod_spec(d, n_lat_tiles), _const_spec((1, hw)), _const_spec(w.shape)],
        out_specs=_tile_spec(d),
        out_shape=jax.ShapeDtypeStruct((b, tt, d), F32),
        compiler_params=_cparams(("parallel", "parallel")),
        name="outproj0",
    )(o_na, o_f, o_b, hraw, x, ctx, mod, gain, w)


HALO = 8
TM_EXT = TM + 2 * HALO


def _halo_specs(width, tt):
    r8 = TM // HALO
    last = tt // HALO - 1
    return [pl.BlockSpec((BB, HALO, width), lambda i, t: (i, jnp.maximum(t * r8 - 1, 0), 0)),
            pl.BlockSpec((BB, HALO, width), lambda i, t: (i, jnp.minimum((t + 1) * r8, last), 0))]


def _modulated_with_halo(h_ref, hp_ref, hn_ref, gain, m_ref, shift_row, scale_row):
    us, exts = [], []
    for bi in range(BB):
        shift = m_ref[bi, shift_row:shift_row + 1, :]
        scale = m_ref[bi, scale_row:scale_row + 1, :]
        u = _rms_mod(h_ref[bi], gain, shift, scale)
        us.append(u.astype(BF16))
        exts += [_rms_mod(hp_ref[bi], gain, shift, scale), u, _rms_mod(hn_ref[bi], gain, shift, scale)]
    return jnp.concatenate(us, axis=0), jnp.concatenate(exts, axis=0).astype(BF16)


def _conv3_ext(a_ext, cw, cb, n_lat_tiles):
    n = a_ext.shape[0] - 2 * HALO
    t = pl.program_id(1)
    has_prev = jnp.logical_and(t != 0, t != n_lat_tiles)
    has_next = jnp.logical_and(t != n_lat_tiles - 1, t != n_lat_tiles)
    a = a_ext[HALO:HALO + n]
    prev_row = jnp.where(has_prev, a_ext[HALO - 1:HALO], 0.0)
    next_row = jnp.where(has_next, a_ext[HALO + n:HALO + n + 1], 0.0)
    row = lax.broadcasted_iota(jnp.int32, (n, 1), 0)
    up = jnp.where(row == 0, prev_row, pltpu.roll(a, 1, 0))
    dn = jnp.where(row == n - 1, next_row, pltpu.roll(a, n - 1, 0))
    return cw[0:1, :] * up + cw[1:2, :] * a + cw[2:3, :] * dn + cb


def _ffn_up_kernel(h_ref, hp_ref, hn_ref, m_ref, gain_ref, w_ref, cw_ref, cb_ref, mid_ref, *, n_lat_tiles):
    u, u_ext = _modulated_with_halo(h_ref, hp_ref, hn_ref, gain_ref[...], m_ref, 3, 4)
    dff = mid_ref.shape[2]
    for lo, hi in _col_chunks(dff, 768):
        a_ext = jnp.dot(u_ext, w_ref[:, lo:hi], preferred_element_type=F32)
        v = jnp.dot(u, w_ref[:, dff + lo:dff + hi], preferred_element_type=F32)
        for bi in range(BB):
            c = _conv3_ext(a_ext[bi * TM_EXT:(bi + 1) * TM_EXT], cw_ref[:, lo:hi], cb_ref[:, lo:hi], n_lat_tiles)
            mid_ref[bi, :, lo:hi] = (_gelu_tanh(c) * v[bi * TM:(bi + 1) * TM]).astype(mid_ref.dtype)


def _ffn_up(hh, mod, gain, w, conv_w, conv_b, n_tiles, n_lat_tiles):
    b, tt_in, d = hh.shape
    dff = w.shape[1] // 2
    return pl.pallas_call(
        functools.partial(_ffn_up_kernel, n_lat_tiles=n_lat_tiles),
        grid=(b // BB, n_tiles),
        in_specs=[_tile_spec(d)] + _halo_specs(d, tt_in) + [
            _mod_spec(d, n_lat_tiles), _const_spec((1, d)), _const_spec(w.shape),
            _const_spec((3, dff)), _const_spec((1, dff))],
        out_specs=_tile_spec(dff),
        out_shape=jax.ShapeDtypeStruct((b, n_tiles * TM, dff), BF16),
        compiler_params=_cparams(("parallel", "parallel")),
        name="ffn_up",
    )(hh, hh, hh, mod, gain, w, conv_w, conv_b)


def _ffn_down_kernel(mid_ref, w_ref, h_ref, m_ref, o_ref):
    y = jnp.dot(jnp.concatenate([mid_ref[bi] for bi in range(BB)], axis=0), w_ref[...],
                preferred_element_type=F32)
    for bi in range(BB):
        o_ref[bi] = h_ref[bi] + m_ref[bi, 5:6, :] * y[bi * TM:(bi + 1) * TM]


def _ffn_down(mid, w, hh, mod, n_lat_tiles):
    b, tt, dff = mid.shape
    d = hh.shape[2]
    return pl.pallas_call(
        _ffn_down_kernel,
        grid=(b // BB, tt // TM),
        in_specs=[_tile_spec(dff), _const_spec(w.shape), _tile_spec(d), _mod_spec(d, n_lat_tiles)],
        out_specs=_tile_spec(d),
        out_shape=jax.ShapeDtypeStruct((b, tt, d), F32),
        compiler_params=_cparams(("parallel", "parallel")),
        name="ffn_down",
    )(mid, w, hh, mod)


def _inproj1_kernel(h_ref, hp_ref, hn_ref, m_ref, gain_ref, w_ref, cw_ref, cb_ref, dtb_ref, a_ref,
                    z_ref, xs_ref, bc_ref, dtp_ref, *, n_lat_tiles):
    u, u_ext = _modulated_with_halo(h_ref, hp_ref, hn_ref, gain_ref[...], m_ref, 0, 1)
    nz = z_ref.shape[2]
    nxs = xs_ref.shape[2]
    nx = nxs + bc_ref.shape[2]
    step = 512
    for lo, hi in _col_chunks(nz, step):
        z = _silu(jnp.dot(u, w_ref[:, lo:hi], preferred_element_type=F32))
        for bi in range(BB):
            z_ref[bi, :, lo:hi] = z[bi * TM:(bi + 1) * TM].astype(z_ref.dtype)
    for lo, hi in _col_chunks(nx, step):
        xbc_ext = jnp.dot(u_ext, w_ref[:, nz + lo:nz + hi], preferred_element_type=F32)
        for bi in range(BB):
            c = _silu(_conv3_ext(xbc_ext[bi * TM_EXT:(bi + 1) * TM_EXT], cw_ref[:, lo:hi], cb_ref[:, lo:hi],
                                 n_lat_tiles))
            if hi <= nxs:
                xs_ref[bi, :, lo:hi] = c.astype(xs_ref.dtype)
            else:
                bc_ref[bi, :, lo - nxs:hi - nxs] = c.astype(bc_ref.dtype)
    raw = jnp.dot(u, w_ref[:, nz + nx:], preferred_element_type=F32)
    dt = _softplus(raw + dtb_ref[...])
    la = dt * a_ref[...]
    for bi in range(BB):
        dtp_ref[bi, :, 0:2 * LANES] = dt[bi * TM:(bi + 1) * TM]
        dtp_ref[bi, :, 2 * LANES:4 * LANES] = la[bi * TM:(bi + 1) * TM]


def _inproj1(hh, mod, gain, w, conv_w, conv_b, dt_bias, a_neg, nz, nxs):
    b, tt, d = hh.shape
    nt = tt // TM
    n_lat_tiles = nt - 1
    nx = conv_w.shape[1]
    assert nxs % 512 == 0
    tile = _tile_spec
    return pl.pallas_call(
        functools.partial(_inproj1_kernel, n_lat_tiles=n_lat_tiles),
        grid=(b // BB, nt),
        in_specs=[tile(d)] + _halo_specs(d, tt) + [
            _mod_spec(d, n_lat_tiles), _const_spec((1, d)), _const_spec(w.shape),
            _const_spec((3, nx)), _const_spec((1, nx)),
            _const_spec((1, 2 * LANES)), _const_spec((1, 2 * LANES))],
        out_specs=[tile(nz), tile(nxs), tile(nx - nxs), tile(4 * LANES)],
        out_shape=[jax.ShapeDtypeStruct((b, tt, nz), BF16),
                   jax.ShapeDtypeStruct((b, tt, nxs), BF16),
                   jax.ShapeDtypeStruct((b, tt, nx - nxs), BF16),
                   jax.ShapeDtypeStruct((b, tt, 4 * LANES), F32)],
        compiler_params=_cparams(("parallel", "parallel")),
        name="inproj1",
    )(hh, hh, hh, mod, gain, w, conv_w, conv_b, dt_bias, a_neg)


class _SsdDir:
    def __init__(self, rev, x_ref, bc_ref, dt_ref, la_ref, e_ref, o_ref, s_ref):
        self.rev, self.x_ref, self.bc_ref, self.dt_ref, self.la_ref = rev, x_ref, bc_ref, dt_ref, la_ref
        self.e_ref, self.o_ref, self.s_ref = e_ref, o_ref, s_ref
        self.n_groups = s_ref.shape[0]
        self.gw = s_ref.shape[2]
        self.n_heads = x_ref.shape[2] // SSD_HEAD_DIM
        self.n_chunks = x_ref.shape[1] // CHUNK
        self.states = [s_ref[g] for g in range(self.n_groups)]
        self.tri = _tri_matrix(CHUNK, rev)
        rowc = lax.broadcasted_iota(jnp.int32, (CHUNK, self.gw), 0)
        pos = lax.broadcasted_iota(jnp.int32, (CHUNK, self.gw), 1) % SSD_HEAD_DIM
        self.on_diag = pos == rowc
        self.causal = (pos >= rowc) if rev else (pos <= rowc)

    def chunk(self, k):
        return self.n_chunks - 1 - k if self.rev else k

    def _stacked(self, v):
        lane = lax.broadcasted_iota(jnp.int32, (1, LANES), 1)
        p1, p2, p3 = _split3(v)
        return jnp.where(lane < self.n_heads, p1, jnp.where(lane < 2 * self.n_heads, p2,
                                                            jnp.where(lane < 3 * self.n_heads, p3, jnp.zeros_like(p1))))

    def scalars(self, cc):
        rs = slice(cc * CHUNK, (cc + 1) * CHUNK)
        return (self._stacked(_tri_cumsum(self.tri, self.la_ref[0, rs, :])), self._stacked(self.dt_ref[0, rs, :]))

    def prep(self, cc, g, scalars):
        cum_st, dt_st = scalars
        rs = slice(cc * CHUNK, (cc + 1) * CHUNK)
        gs = slice(g * self.gw, (g + 1) * self.gw)
        last = 0 if self.rev else CHUNK - 1
        cum = jnp.dot(cum_st, self.e_ref[:, gs], preferred_element_type=F32)
        dt = jnp.dot(dt_st, self.e_ref[:, gs], preferred_element_type=F32)
        cum_row = jnp.sum(jnp.where(self.on_diag, cum, 0.0), axis=0, keepdims=True)
        cum_last = cum[last:last + 1, :]
        x_dt = self.x_ref[0, rs, gs].astype(F32) * dt
        return dict(
            xw=(x_dt * jnp.exp(cum_last - cum)).astype(BF16),
            x_dt=x_dt.astype(BF16),
            decay=jnp.exp(jnp.where(self.causal, cum - cum_row, MASK_NEG)),
            e_cum=jnp.exp(cum),
            e_last=jnp.exp(cum_last))

    def first_matmuls(self, cc, g, p):
        rs = slice(cc * CHUNK, (cc + 1) * CHUNK)
        b_g = self.bc_ref[0, rs, g * SSD_STATE:(g + 1) * SSD_STATE]
        c_g = self.bc_ref[0, rs, (self.n_groups + g) * SSD_STATE:(self.n_groups + g + 1) * SSD_STATE]
        b_rep = jnp.concatenate([b_g] * (self.gw // CHUNK), axis=0)
        return dict(cb=_dot_nt(c_g, b_rep), read=_dot(c_g, self.states[g]),
                    update=lax.dot_general(b_g, p["xw"], (((0,), (0,)), ((), ())), preferred_element_type=F32))

    def intra(self, g, p, first):
        blk = 256
        blk_head = lax.broadcasted_iota(jnp.int32, (1, blk), 1) // SSD_HEAD_DIM
        m_g = (p["decay"] * first["cb"]).astype(BF16)
        y_parts = []
        for j in range(self.gw // blk):
            x4 = p["x_dt"][:, j * blk:(j + 1) * blk]
            x_bd = jnp.concatenate(
                [jnp.where(blk_head == hh, x4, jnp.zeros_like(x4)) for hh in range(blk // SSD_HEAD_DIM)], axis=0)
            y_parts.append(jnp.dot(m_g[:, j * blk:(j + 1) * blk], x_bd, preferred_element_type=F32))
        return jnp.concatenate(y_parts, axis=-1)

    def combine(self, cc, g, p, first, y_intra):
        rs = slice(cc * CHUNK, (cc + 1) * CHUNK)
        gs = slice(g * self.gw, (g + 1) * self.gw)
        self.o_ref[0, rs, gs] = (y_intra + first["read"] * p["e_cum"]).astype(self.o_ref.dtype)
        self.states[g] = self.states[g] * p["e_last"] + first["update"]

    def finish(self):
        for g in range(self.n_groups):
            self.s_ref[g] = self.states[g]


def _ssd_kernel(xf_ref, bcf_ref, dtf_ref, laf_ref, xb_ref, bcb_ref, dtb_ref, lab_ref, e_ref,
                of_ref, ob_ref, sf_ref, sb_ref):
    @pl.when(pl.program_id(1) == 0)
    def _():
        sf_ref[...] = jnp.zeros_like(sf_ref)
        sb_ref[...] = jnp.zeros_like(sb_ref)

    fwd = _SsdDir(False, xf_ref, bcf_ref, dtf_ref, laf_ref, e_ref, of_ref, sf_ref)
    bwd = _SsdDir(True, xb_ref, bcb_ref, dtb_ref, lab_ref, e_ref, ob_ref, sb_ref)
    groups = range(fwd.n_groups)
    steps = [(d, k) for k in range(fwd.n_chunks) for d in (fwd, bwd)]
    d0, k0 = steps[0]
    sc = d0.scalars(d0.chunk(k0))
    ready = [d0.prep(d0.chunk(k0), g, sc) for g in groups]
    for idx, (d, k) in enumerate(steps):
        nxt = steps[idx + 1] if idx + 1 < len(steps) else None
        if nxt is not None:
            nd, nk = nxt
            nsc = nd.scalars(nd.chunk(nk))
        cc = d.chunk(k)
        first = [d.first_matmuls(cc, g, ready[g]) for g in groups]
        following = []
        for g in groups:
            y_intra = d.intra(g, ready[g], first[g])
            if nxt is not None:
                following.append(nd.prep(nd.chunk(nk), g, nsc))
            d.combine(cc, g, ready[g], first[g], y_intra)
        ready = following
    fwd.finish()
    bwd.finish()


def _ssd_scan(xs, bc, dtp, expand_mat):
    b, tt, width = xs.shape
    nt = tt // TM
    nbc = bc.shape[2]
    gw = width // SSD_GROUPS

    def dir_specs(rev):
        blk = lambda i: _scan_block(i, nt, rev)
        d_col = 1 if rev else 0
        return [pl.BlockSpec((1, TM, width), lambda bi, i: (bi, blk(i), 0)),
                pl.BlockSpec((1, TM, nbc), lambda bi, i: (bi, blk(i), 0)),
                pl.BlockSpec((1, TM, LANES), lambda bi, i: (bi, blk(i), d_col)),
                pl.BlockSpec((1, TM, LANES), lambda bi, i: (bi, blk(i), 2 + d_col))]

    out_spec = lambda rev: pl.BlockSpec((1, TM, width), lambda bi, i: (bi, _scan_block(i, nt, rev), 0))
    return pl.pallas_call(
        _ssd_kernel,
        grid=(b, nt),
        in_specs=dir_specs(False) + dir_specs(True) + [pl.BlockSpec(expand_mat.shape, lambda bi, i: (0, 0))],
        out_specs=[out_spec(False), out_spec(True)],
        out_shape=[jax.ShapeDtypeStruct((b, tt, width), BF16)] * 2,
        scratch_shapes=[pltpu.VMEM((SSD_GROUPS, SSD_STATE, gw), F32)] * 2,
        compiler_params=_cparams(("parallel", "arbitrary")),
        name="ssd_scan",
    )(xs, bc, dtp, dtp, xs, bc, dtp, dtp, expand_mat)


def _outproj1_kernel(yf_ref, yb_ref, xs_ref, sz_ref, h_ref, m_ref, d_ref, gain_ref, w_ref, o_ref):
    width = xs_ref.shape[2]
    gw = width // SSD_GROUPS
    normed = []
    for bi in range(BB):
        yy = ((yf_ref[bi].astype(F32) + yb_ref[bi].astype(F32) + d_ref[...] * xs_ref[bi].astype(F32))
              * sz_ref[bi].astype(F32))
        parts = []
        for g in range(SSD_GROUPS):
            gs = slice(g * gw, (g + 1) * gw)
            yg = yy[:, gs]
            ms = jnp.mean(yg * yg, axis=-1, keepdims=True)
            parts.append((yg * lax.rsqrt(ms + EPS) * gain_ref[:, gs]).astype(BF16))
        normed.append(jnp.concatenate(parts, axis=-1))
    y = jnp.dot(jnp.concatenate(normed, axis=0), w_ref[...], preferred_element_type=F32)
    for bi in range(BB):
        o_ref[bi] = h_ref[bi] + m_ref[bi, 2:3, :] * y[bi * TM:(bi + 1) * TM]


def _outproj1(y_f, y_b, xs, sz, hh, mod, d_skip, gain, w, n_lat_tiles):
    b, _, d = hh.shape
    width = xs.shape[2]
    tile = _tile_spec
    return pl.pallas_call(
        _outproj1_kernel,
        grid=(b // BB, n_lat_tiles),
        in_specs=[tile(width), tile(width), tile(width), tile(width), tile(d),
                  _mod_spec(d, n_lat_tiles), _const_spec((1, width)), _const_spec((1, width)),
                  _const_spec(w.shape)],
        out_specs=tile(d),
        out_shape=jax.ShapeDtypeStruct((b, n_lat_tiles * TM, d), F32),
        compiler_params=_cparams(("parallel", "parallel")),
        name="outproj1",
    )(y_f, y_b, xs, sz, hh, mod, d_skip, gain, w)


def _mod_rows(mods_l, b, d):
    six = mods_l.reshape(mods_l.shape[0], 6, d)
    lat = six[:b]
    ctx = jnp.broadcast_to(six[b:b + 1], (b, 6, d))
    both = jnp.stack([lat, ctx], axis=1)
    return jnp.pad(both, ((0, 0), (0, 0), (0, 2), (0, 0)))


def kernel(x, c, ctx, c_ctx, w_mod, b_mod, norm_mix, norm_ffn, ffn_w_up, ffn_conv_w, ffn_conv_b, ffn_w_down,
           hy_w_in, hy_w_out, na_q_gain, na_k_gain, na_rpb, hg_out_gain, hg_lb_fwd, hg_lb_bwd, ssd_w_in,
           ssd_conv_w, ssd_conv_b, ssd_dt_bias_fwd, ssd_dt_bias_bwd, ssd_a_log_fwd, ssd_a_log_bwd, ssd_d,
           ssd_norm_gain, ssd_w_out):
    b, t_lat, d = x.shape
    l_ctx = ctx.shape[1]
    assert l_ctx == TM and t_lat % TM == 0 and w_mod.shape[0] == 2 and b % BB == 0
    n_lat_tiles = t_lat // TM
    nt = n_lat_tiles + 1

    rows = ((b + 1 + 7) // 8) * 8
    cond = jnp.concatenate([c, c_ctx[None], jnp.zeros((rows - b - 1, d), F32)], axis=0)
    mods = _modulation(cond, w_mod, b_mod)
    mod0 = _mod_rows(mods[0], b, d)
    mod1 = _mod_rows(mods[1], b, d)
    row = lambda v: v.reshape(1, -1).astype(F32)

    hg_w = hg_lb_fwd.shape[1]
    na_w = hy_w_out.shape[1] - hg_w
    n_na_heads = na_w // NA_HEAD_DIM
    qg = row(jnp.tile(na_q_gain[0], n_na_heads)) * (NA_HEAD_DIM ** -0.5)
    kg = row(jnp.tile(na_k_gain[0], n_na_heads))
    qkv, hraw = _inproj0(x, ctx, mod0, row(norm_mix[0]), hy_w_in[0].astype(BF16), qg, kg)
    bias = _na_bias_table(na_rpb[0], t_lat // GRID_W)
    o_na = _na_attention(qkv, bias, t_lat)

    def lb_rows(lb_param):
        lb = jnp.cumsum(jax.nn.softmax(lb_param.astype(F32), axis=0), axis=0)[0]
        return jnp.pad(jnp.stack([lb, 1.0 - lb], axis=0), ((0, 6), (0, 0)))

    o_f, o_b = _hgrn_scan(hraw, lb_rows(hg_lb_fwd), lb_rows(hg_lb_bwd))
    hg_gain = row(jnp.tile(hg_out_gain[0], hg_w // HG_HEAD_DIM))
    h1 = _outproj0(o_na, o_f, o_b, hraw, x, ctx, mod0, hg_gain, hy_w_out[0].astype(BF16))

    mid0 = _ffn_up(h1, mod0, row(norm_ffn[0]), ffn_w_up[0].astype(BF16), ffn_conv_w[0], row(ffn_conv_b[0]),
                   nt, n_lat_tiles)
    h2 = _ffn_down(mid0, ffn_w_down[0].astype(BF16), h1, mod0, n_lat_tiles)

    n_heads = ssd_d.shape[1]
    inner = ssd_w_out.shape[1]
    nxbc = ssd_conv_w.shape[2]
    w1 = ssd_w_in[0]
    rep = LANES // n_heads
    w_dtf = jnp.tile(w1[:, inner + nxbc:inner + nxbc + n_heads], (1, rep))
    w_dtb = jnp.tile(w1[:, inner + nxbc + n_heads:], (1, rep))
    w1p = jnp.concatenate([w1[:, :inner + nxbc], w_dtf, w_dtb], axis=1).astype(BF16)
    dt_bias = row(jnp.concatenate([jnp.tile(ssd_dt_bias_fwd[0], rep), jnp.tile(ssd_dt_bias_bwd[0], rep)]))
    a_neg = row(jnp.concatenate([jnp.tile(-jnp.exp(ssd_a_log_fwd[0].astype(F32)), rep),
                                 jnp.tile(-jnp.exp(ssd_a_log_bwd[0].astype(F32)), rep)]))
    z, xs, bc, dtp = _inproj1(h2, mod1, row(norm_mix[1]), w1p, ssd_conv_w[0], row(ssd_conv_b[0]), dt_bias,
                              a_neg, inner, inner)

    lane = np.arange(LANES)[:, None]
    colh = (np.arange(inner) // SSD_HEAD_DIM)[None, :]
    expand_mat = jnp.asarray(((lane % n_heads == colh) & (lane < 3 * n_heads)).astype(np.float32), dtype=BF16)
    y_f, y_b = _ssd_scan(xs, bc, dtp, expand_mat)
    d_skip = row(jnp.repeat(ssd_d[0], SSD_HEAD_DIM))
    h3 = _outproj1(y_f, y_b, xs, z, h2, mod1, d_skip, row(ssd_norm_gain[0]), ssd_w_out[0].astype(BF16),
                   n_lat_tiles)

    mid1 = _ffn_up(h3, mod1, row(norm_ffn[1]), ffn_w_up[1].astype(BF16), ffn_conv_w[1], row(ffn_conv_b[1]),
                   n_lat_tiles, n_lat_tiles)
    return _ffn_down(mid1, ffn_w_down[1].astype(BF16), h3, mod1, n_lat_tiles)
```

```python
import functools
import math

import numpy as np
import jax
import jax.numpy as jnp
from jax import lax
from jax.experimental import pallas as pl
from jax.experimental.pallas import tpu as pltpu

F32 = jnp.float32
BF16 = jnp.bfloat16
EPS = 1e-6

GRID_W = 64
NA_HEAD_DIM = 64
WIN_ROWS = 8
WIN_COLS = 16
NA_ROW_BLOCK = 4
HG_HEAD_DIM = 128
SSD_HEAD_DIM = 64
SSD_GROUPS = 4
SSD_STATE = 128
CHUNK = 64
SUB = 16
TM = 256
BB = 2
LANES = 128
MASK_NEG = -1e30
SAFE_RANGE = 60.0

VMEM_LIMIT = 56 * 1024 * 1024


def _cparams(sem):
    return pltpu.CompilerParams(dimension_semantics=sem, vmem_limit_bytes=VMEM_LIMIT)


def _dot(a, b):
    return jnp.dot(a.astype(BF16), b.astype(BF16), preferred_element_type=F32)


def _dot_nt(a, b):
    return lax.dot_general(a.astype(BF16), b.astype(BF16), (((1,), (1,)), ((), ())),
                           preferred_element_type=F32)


def _silu(x):
    h = 0.5 * x
    return h + h * jnp.tanh(h)


def _softplus(x):
    return jnp.maximum(x, 0.0) + jnp.log1p(jnp.exp(-jnp.abs(x)))


def _gelu_tanh(x):
    c = math.sqrt(2.0 / math.pi)
    return 0.5 * x * (1.0 + jnp.tanh(c * (x + 0.044715 * (x * x * x))))


def _rms_mod(x, gain, shift, scale):
    ms = jnp.mean(x * x, axis=-1, keepdims=True)
    return (x * lax.rsqrt(ms + EPS) * gain) * (1.0 + scale) + shift


def _split3(v):
    p1 = v.astype(BF16)
    r1 = v - p1.astype(F32)
    p2 = r1.astype(BF16)
    r2 = r1 - p2.astype(F32)
    return p1, p2, r2.astype(BF16)


def _tri_cumsum(tri, v):
    p1, p2, p3 = _split3(v)
    d = lambda p: jnp.dot(tri, p, preferred_element_type=F32)
    return d(p1) + d(p2) + d(p3)


def _mod_spec(d, n_lat_tiles):
    return pl.BlockSpec((BB, None, 8, d), lambda i, t: (i, t // n_lat_tiles, 0, 0))


def _tile_spec(width):
    return pl.BlockSpec((BB, TM, width), lambda i, t: (i, t, 0))


def _const_spec(shape):
    return pl.BlockSpec(shape, lambda i, t: (0,) * len(shape))


def _stream0_specs(d, n_lat_tiles):
    return [pl.BlockSpec((BB, TM, d), lambda i, t: (i, jnp.minimum(t, n_lat_tiles - 1), 0)),
            pl.BlockSpec((BB, TM, d), lambda i, t: (i, 0, 0))]


def _stream0_rows(x_ref, c_ref, bi, n_lat_tiles):
    return jnp.where(pl.program_id(1) == n_lat_tiles, c_ref[bi], x_ref[bi])


def _col_chunks(width, step):
    return [(lo, min(lo + step, width)) for lo in range(0, width, step)]


def _tri_matrix(n, rev):
    r = lax.broadcasted_iota(jnp.int32, (n, n), 0)
    c = lax.broadcasted_iota(jnp.int32, (n, n), 1)
    return jnp.where((c >= r) if rev else (c <= r), 1.0, 0.0).astype(BF16)


def _mod_kernel(s_ref, w_ref, b_ref, o_ref):
    s = _silu(s_ref[...])
    o_ref[0] = _dot(s, w_ref[0]) + b_ref[0]


def _modulation(cond, w_mod, b_mod):
    depth, d, n = w_mod.shape
    rows = cond.shape[0]
    tn = 1536
    return pl.pallas_call(
        _mod_kernel,
        grid=(depth, n // tn),
        in_specs=[pl.BlockSpec((rows, d), lambda l, j: (0, 0)),
                  pl.BlockSpec((1, d, tn), lambda l, j: (l, 0, j)),
                  pl.BlockSpec((1, 1, tn), lambda l, j: (l, 0, j))],
        out_specs=pl.BlockSpec((1, rows, tn), lambda l, j: (l, 0, j)),
        out_shape=jax.ShapeDtypeStruct((depth, rows, n), F32),
        compiler_params=_cparams(("parallel", "parallel")),
        name="modulation",
    )(cond, w_mod, b_mod.reshape(depth, 1, n))


def _inproj0_kernel(x_ref, c_ref, m_ref, gain_ref, w_ref, qg_ref, kg_ref, qkv_ref, hraw_ref, *, n_lat_tiles):
    u = jnp.concatenate(
        [_rms_mod(_stream0_rows(x_ref, c_ref, bi, n_lat_tiles), gain_ref[...], m_ref[bi, 0:1, :],
                  m_ref[bi, 1:2, :]).astype(BF16) for bi in range(BB)], axis=0)
    lo = lax.broadcasted_iota(jnp.int32, (1, LANES), 1) < NA_HEAD_DIM
    na_w = qg_ref.shape[1]

    def head_norm(y, g_ref):
        outs = []
        for c in range(na_w // LANES):
            yc = y[:, c * LANES:(c + 1) * LANES]
            sq = yc * yc
            s_lo = jnp.sum(jnp.where(lo, sq, 0.0), axis=-1, keepdims=True)
            s_hi = jnp.sum(jnp.where(lo, 0.0, sq), axis=-1, keepdims=True)
            inv = jnp.where(lo, lax.rsqrt(s_lo / NA_HEAD_DIM + EPS), lax.rsqrt(s_hi / NA_HEAD_DIM + EPS))
            outs.append(yc * inv * g_ref[:, c * LANES:(c + 1) * LANES])
        return jnp.concatenate(outs, axis=-1)

    def proj(lo_col, width):
        return jnp.dot(u, w_ref[:, lo_col:lo_col + width], preferred_element_type=F32)

    def store(ref, lo_col, y):
        for bi in range(BB):
            ref[bi, :, lo_col:lo_col + y.shape[1]] = y[bi * TM:(bi + 1) * TM].astype(ref.dtype)

    store(qkv_ref, 0, head_norm(proj(0, na_w), qg_ref))
    store(qkv_ref, na_w, head_norm(proj(na_w, na_w), kg_ref))
    store(qkv_ref, 2 * na_w, proj(2 * na_w, na_w))
    for lo_col, hi_col in _col_chunks(hraw_ref.shape[2], 512):
        store(hraw_ref, lo_col, proj(3 * na_w + lo_col, hi_col - lo_col))


def _inproj0(x, ctx, mod, gain, w, qg, kg):
    b, t_lat, d = x.shape
    n = w.shape[1]
    na_w = qg.shape[1]
    n_hg = n - 3 * na_w
    n_lat_tiles = t_lat // TM
    tt = t_lat + ctx.shape[1]
    return pl.pallas_call(
        functools.partial(_inproj0_kernel, n_lat_tiles=n_lat_tiles),
        grid=(b // BB, n_lat_tiles + 1),
        in_specs=_stream0_specs(d, n_lat_tiles) + [
            _mod_spec(d, n_lat_tiles), _const_spec((1, d)), _const_spec((d, n)),
            _const_spec((1, na_w)), _const_spec((1, na_w))],
        out_specs=[_tile_spec(3 * na_w), _tile_spec(n_hg)],
        out_shape=[jax.ShapeDtypeStruct((b, tt, 3 * na_w), BF16),
                   jax.ShapeDtypeStruct((b, tt, n_hg), F32)],
        compiler_params=_cparams(("parallel", "parallel")),
        name="inproj0",
    )(x, ctx, mod, gain, w, qg, kg)


def _na_kernel(q_ref, k_ref, v_ref, bias_ref, o_ref, *, t_lat, rows):
    tt = q_ref.shape[1]
    lane = lax.broadcasted_iota(jnp.int32, (1, LANES), 1)
    head_mask = [lane < NA_HEAD_DIM, lane >= NA_HEAD_DIM]
    kc = k_ref[0, t_lat:tt, :]
    vc = v_ref[0, t_lat:tt, :]
    _, ku, u0s, _, case_of_block = _na_block_plan(rows)
    n_loc = ku * GRID_W
    n_q = NA_ROW_BLOCK * GRID_W

    qc = q_ref[0, t_lat:tt, :]
    n_c = tt - t_lat
    s = _dot_nt(jnp.concatenate([jnp.where(hm, qc, jnp.zeros_like(qc)) for hm in head_mask], axis=0), kc)
    p = jnp.exp(s - jnp.max(s, axis=-1, keepdims=True))
    o_both = _dot(p, vc) / jnp.sum(p, axis=-1, keepdims=True)
    oc = None
    for hi, hm in enumerate(head_mask):
        o = o_both[hi * n_c:(hi + 1) * n_c]
        oc = o if oc is None else jnp.where(hm, o, oc)
    o_ref[0, t_lat:tt, :] = oc.astype(o_ref.dtype)

    n_blocks = rows // NA_ROW_BLOCK
    group = 2 if n_blocks % 2 == 0 else 1

    def blocks_body(it, carry):
        loaded, scores = [], []
        for j in range(group):
            i = it * group + j
            u0 = u0s[0]
            case = case_of_block[0]
            for bi in range(1, len(u0s)):
                u0 = jnp.where(i == bi, u0s[bi], u0)
                case = jnp.where(i == bi, case_of_block[bi], case)
            q_rows = pl.ds(pl.multiple_of(i * n_q, n_q), n_q)
            q_i = q_ref[0, q_rows, :]
            start = pl.multiple_of(u0 * GRID_W, GRID_W)
            k_loc = k_ref[0, pl.ds(start, n_loc), :]
            loaded.append((q_rows, v_ref[0, pl.ds(start, n_loc), :]))
            q_both = jnp.concatenate([jnp.where(hm, q_i, jnp.zeros_like(q_i)) for hm in head_mask], axis=0)
            s_loc = _dot_nt(q_both, k_loc)
            s_ctx = _dot_nt(q_both, kc)
            for hi in range(len(head_mask)):
                hr = slice(hi * n_q, (hi + 1) * n_q)
                scores.append((s_loc[hr] + bias_ref[hi, case], s_ctx[hr]))
        probs = []
        for s_loc, s_ctx in scores:
            m = jnp.maximum(jnp.max(s_loc, axis=-1, keepdims=True), jnp.max(s_ctx, axis=-1, keepdims=True))
            p_loc = jnp.exp(s_loc - m)
            p_ctx = jnp.exp(s_ctx - m)
            den = jnp.sum(p_loc, axis=-1, keepdims=True) + jnp.sum(p_ctx, axis=-1, keepdims=True)
            probs.append((p_loc.astype(BF16), p_ctx.astype(BF16), den))
        n_hd = len(head_mask)
        for j, (q_rows, v_loc) in enumerate(loaded):
            mine = probs[j * n_hd:(j + 1) * n_hd]
            o_both = (jnp.dot(jnp.concatenate([p[0] for p in mine], axis=0), v_loc, preferred_element_type=F32)
                      + jnp.dot(jnp.concatenate([p[1] for p in mine], axis=0), vc, preferred_element_type=F32))
            out = None
            for hi, hm in enumerate(head_mask):
                o = o_both[hi * n_q:(hi + 1) * n_q] / mine[hi][2]
                out = o if out is None else jnp.where(hm, o, out)
            o_ref[0, q_rows, :] = out.astype(o_ref.dtype)
        return carry

    lax.fori_loop(0, n_blocks // group, blocks_body, 0)


def _na_block_plan(rows):
    kr = min(WIN_ROWS, rows)
    ku = min(kr + NA_ROW_BLOCK - 1, rows)
    u0s, patterns, case_of_block = [], [], []
    for i in range(rows // NA_ROW_BLOCK):
        u0 = int(np.clip(NA_ROW_BLOCK * i - kr // 2, 0, rows - ku))
        rel = []
        for a in range(NA_ROW_BLOCK):
            r = NA_ROW_BLOCK * i + a
            r0 = int(np.clip(r - kr // 2, 0, rows - kr))
            assert u0 <= r0 and r0 + kr <= u0 + ku
            rel.append((r0 - u0, u0 - r))
        u0s.append(u0)
        if tuple(rel) not in patterns:
            patterns.append(tuple(rel))
        case_of_block.append(patterns.index(tuple(rel)))
    return kr, ku, u0s, patterns, case_of_block


def _na_bias_table(rpb, rows):
    kr, ku, _, patterns, _ = _na_block_plan(rows)
    q = np.arange(GRID_W)
    kcol = np.arange(GRID_W)
    ws = np.clip(q - WIN_COLS // 2, 0, GRID_W - WIN_COLS)
    in_win = (kcol[None, :] >= ws[:, None]) & (kcol[None, :] < ws[:, None] + WIN_COLS)
    dc = np.clip(kcol[None, :] - q[:, None] + WIN_COLS - 1, 0, 2 * WIN_COLS - 2)
    u = np.arange(ku)
    n_dr = 2 * WIN_ROWS - 1
    dr = np.full((len(patterns), NA_ROW_BLOCK, ku), n_dr, np.int32)
    for c, rel in enumerate(patterns):
        for a, (r0_rel, u0_minus_r) in enumerate(rel):
            row_ok = (u >= r0_rel) & (u < r0_rel + kr)
            dr[c, a] = np.where(row_ok, u0_minus_r + u + WIN_ROWS - 1, n_dr)
    h = rpb.shape[0]
    tiles = jnp.where(in_win[None, None], rpb.astype(F32)[:, :, dc], MASK_NEG)
    tiles = jnp.concatenate([tiles, jnp.full((h, 1, GRID_W, GRID_W), MASK_NEG, F32)], axis=1)
    tbl = jnp.take(tiles, jnp.asarray(dr.reshape(-1)), axis=1)
    tbl = tbl.reshape(h, len(patterns), NA_ROW_BLOCK, ku, GRID_W, GRID_W)
    return tbl.transpose(0, 1, 2, 4, 3, 5).reshape(h, len(patterns), NA_ROW_BLOCK * GRID_W, ku * GRID_W)


def _na_attention(qkv, bias, t_lat):
    b, tt, w3 = qkv.shape
    na_w = w3 // 3
    ncol = na_w // LANES
    rows = t_lat // GRID_W
    bias_block = (2,) + bias.shape[1:]
    kern = functools.partial(_na_kernel, t_lat=t_lat, rows=rows)
    return pl.pallas_call(
        kern,
        grid=(b, ncol),
        in_specs=[pl.BlockSpec((1, tt, LANES), lambda i, p: (i, 0, p)),
                  pl.BlockSpec((1, tt, LANES), lambda i, p: (i, 0, ncol + p)),
                  pl.BlockSpec((1, tt, LANES), lambda i, p: (i, 0, 2 * ncol + p)),
                  pl.BlockSpec(bias_block, lambda i, p: (p, 0, 0, 0))],
        out_specs=pl.BlockSpec((1, tt, LANES), lambda i, p: (i, 0, p)),
        out_shape=jax.ShapeDtypeStruct((b, tt, na_w), BF16),
        compiler_params=_cparams(("parallel", "parallel")),
        name="na_attention",
    )(qkv, qkv, qkv, bias)


def _scan_block(i, nt, rev):
    if rev:
        return jnp.where(i == 0, nt - 1, nt - 1 - i)
    return jnp.where(i == 0, nt - 1, i - 1)


class _HgrnDir:
    def __init__(self, rev, q_ref, f_ref, v_ref, lb_ref, o_ref, st_ref):
        self.rev, self.q_ref, self.f_ref, self.v_ref, self.o_ref, self.st_ref = rev, q_ref, f_ref, v_ref, o_ref, st_ref
        self.n_heads = st_ref.shape[0]
        self.lb = lb_ref[0:1, :]
        self.one_m_lb = lb_ref[1:2, :]
        self.n_chunks = q_ref.shape[1] // CHUNK
        self.order = list(range(self.n_chunks - 1, -1, -1)) if rev else list(range(self.n_chunks))
        self.heads = [slice(h * HG_HEAD_DIM, (h + 1) * HG_HEAD_DIM) for h in range(self.n_heads)]
        self.states = [st_ref[h] for h in range(self.n_heads)]
        self.gated = {}
        self.tri = _tri_matrix(CHUNK, rev)
        row = lax.broadcasted_iota(jnp.int32, (CHUNK, CHUNK), 0)
        col = lax.broadcasted_iota(jnp.int32, (CHUNK, CHUNK), 1)
        self.causal = (col >= row) if rev else (col <= row)

    def gate(self, cc):
        lb, one_m_lb, q_ref, f_ref = self.lb, self.one_m_lb, self.q_ref, self.f_ref
        rs = slice(cc * CHUNK, (cc + 1) * CHUNK)
        x = f_ref[0, rs, :]
        e = jnp.exp(-jnp.abs(x))
        r = 1.0 / (1.0 + e)
        x_pos = x >= 0.0
        f = jnp.where(x_pos, 1.0 + lb * e, lb + e) * r
        log_f = jnp.where(f > 0.0, jnp.log(f), x)
        k_all = one_m_lb * jnp.where(x_pos, e, 1.0) * r
        return _silu(q_ref[0, rs, :]), k_all, log_f

    def decays(self, cc, gates):
        rev = self.rev
        q_all, k_all, log_f = gates
        n_sub = CHUNK // SUB
        zero_row = jnp.zeros((1, log_f.shape[1]), F32)
        scan_blocks = list(range(n_sub - 1, -1, -1)) if rev else list(range(n_sub))
        rng = None
        g_all = _tri_cumsum(self.tri, log_f)
        c = [g_all[j * SUB:j * SUB + 1, :] if rev else g_all[(j + 1) * SUB - 1:(j + 1) * SUB, :]
             for j in range(n_sub)]
        c_prev = [zero_row] * n_sub
        for before, after in zip(scan_blocks[:-1], scan_blocks[1:]):
            c_prev[after] = c[before]
        for j in range(n_sub):
            d = c_prev[j] - c[j]
            rng = d if rng is None else jnp.maximum(rng, d)
        self.gated[cc] = (q_all, k_all, g_all, c, c_prev)
        return rng

    def slow_diag(self):
        rev = self.rev
        row = lax.broadcasted_iota(jnp.int32, (CHUNK, CHUNK), 0)
        col = lax.broadcasted_iota(jnp.int32, (CHUNK, CHUNK), 1)
        rowv = lax.broadcasted_iota(jnp.int32, (CHUNK, 1), 0)
        pos = rowv % SUB
        accs = []
        for cc in self.order:
            q_all, k_all, g_all = self.gated[cc][:3]
            for hs in self.heads:
                q, k, g = q_all[:, hs], k_all[:, hs], g_all[:, hs]
                acc = jnp.zeros((CHUNK, CHUNK), F32)
                for dlt in range(SUB):
                    shift = (CHUNK - dlt) % CHUNK if rev else dlt
                    k_d = pltpu.roll(k, shift, 0) if shift else k
                    g_d = pltpu.roll(g, shift, 0) if shift else g
                    valid = (pos + dlt <= SUB - 1) if rev else (pos >= dlt)
                    e = jnp.exp(jnp.where(valid, g - g_d, 0.0))
                    val = jnp.sum(q * k_d * e, axis=-1, keepdims=True)
                    partner = (row + dlt) if rev else (row - dlt)
                    acc = acc + jnp.where((col == partner) & valid, val, 0.0)
                accs.append(acc)
        return jnp.stack(accs, axis=0)

    def prepare(self, ci, safe):
        rev = self.rev
        n_sub = CHUNK // SUB
        cc = self.order[ci]
        q_all, k_all, g_all, c, c_prev = self.gated[cc]
        width = g_all.shape[1]
        scan_blocks = list(range(n_sub - 1, -1, -1)) if rev else list(range(n_sub))
        c_final = c[scan_blocks[-1]]
        blk = lambda x, j: x[j * SUB:(j + 1) * SUB]

        def per_block(fn):
            return jnp.concatenate([fn(j) for j in range(n_sub)], axis=0)

        q_b = per_block(lambda j: blk(q_all, j) * jnp.exp(blk(g_all, j) - c_prev[j]))
        k_end = per_block(lambda j: blk(k_all, j) * jnp.exp(c[j] - blk(g_all, j)))
        k_comb = []
        for sb in range(n_sub):
            def scaled(j, sb=sb):
                if j == sb:
                    return blk(k_end, j) * jnp.where(safe, jnp.exp(jnp.minimum(c_prev[sb] - c[sb], SAFE_RANGE)), 0.0)
                if scan_blocks.index(j) < scan_blocks.index(sb):
                    return blk(k_end, j) * jnp.exp(c_prev[sb] - c[j])
                return jnp.zeros((SUB, width), F32)
            k_comb.append(per_block(scaled))
        return dict(
            rows=slice(cc * CHUNK, (cc + 1) * CHUNK), q_b=q_b, k_comb=k_comb,
            q_g=per_block(lambda j: blk(q_b, j) * jnp.exp(c_prev[j])),
            k_last=per_block(lambda j: blk(k_end, j) * jnp.exp(c_final - c[j])),
            e_last=jnp.exp(c_final), v=self.v_ref[0, slice(cc * CHUNK, (cc + 1) * CHUNK), :], outs=[])

    def scores(self, h, p, diag_block):
        n_sub = CHUNK // SUB
        hs = self.heads[h]
        blocks = [_dot_nt(p["q_b"][sb * SUB:(sb + 1) * SUB, hs], p["k_comb"][sb][:, hs]) for sb in range(n_sub)]
        return jnp.where(self.causal, jnp.concatenate(blocks, axis=0), 0.0) + diag_block

    def values(self, h, p, a):
        hs = self.heads[h]
        st = self.states[h]
        p["outs"].append(_dot(a, p["v"][:, hs]) + _dot_nt(p["q_g"][:, hs], st))
        self.states[h] = st * p["e_last"][:, hs] + lax.dot_general(
            p["v"][:, hs].astype(BF16), p["k_last"][:, hs].astype(BF16), (((0,), (0,)), ((), ())),
            preferred_element_type=F32)

    def store(self, p):
        self.o_ref[0, p["rows"], :] = jnp.concatenate(p["outs"], axis=-1).astype(self.o_ref.dtype)

    def finish(self):
        for h in range(self.n_heads):
            self.st_ref[h] = self.states[h]


def _hgrn_kernel(qf_ref, ff_ref, vf_ref, lbf_ref, qb_ref, fb_ref, vb_ref, lbb_ref, of_ref, ob_ref, sf_ref, sb_ref):
    @pl.when(pl.program_id(1) == 0)
    def _():
        sf_ref[...] = jnp.zeros_like(sf_ref)
        sb_ref[...] = jnp.zeros_like(sb_ref)

    dirs = [_HgrnDir(False, qf_ref, ff_ref, vf_ref, lbf_ref, of_ref, sf_ref),
            _HgrnDir(True, qb_ref, fb_ref, vb_ref, lbb_ref, ob_ref, sb_ref)]
    n_chunks, n_heads = dirs[0].n_chunks, dirs[0].n_heads
    rng = None
    stage1 = [(d, d.order[ci]) for ci in range(n_chunks) for d in dirs]
    gates = [d.gate(cc) for d, cc in stage1]
    for (d, cc), gt in zip(stage1, gates):
        r = d.decays(cc, gt)
        rng = r if rng is None else jnp.maximum(rng, r)
    safe = jnp.max(rng) <= SAFE_RANGE
    per_dir = n_chunks * n_heads
    diag = lax.cond(safe, lambda: jnp.zeros((2 * per_dir, CHUNK, CHUNK), F32),
                    lambda: jnp.concatenate([d.slow_diag() for d in dirs], axis=0))
    for ci in range(n_chunks):
        preps = [d.prepare(ci, safe) for d in dirs]
        chains = [(h, di, d) for h in range(n_heads) for di, d in enumerate(dirs)]
        att = [d.scores(h, preps[di], diag[di * per_dir + ci * n_heads + h]) for h, di, d in chains]
        for (h, di, d), a in zip(chains, att):
            d.values(h, preps[di], a)
        for di, d in enumerate(dirs):
            d.store(preps[di])
    for d in dirs:
        d.finish()


def _hgrn_scan(hraw, lb_rows_f, lb_rows_b):
    b, tt, n = hraw.shape
    w = lb_rows_f.shape[1]
    nt = tt // TM
    n_heads = w // HG_HEAD_DIM

    def dir_specs(rev):
        blk = lambda i: _scan_block(i, nt, rev)
        f_col = 2 if rev else 1
        return [pl.BlockSpec((1, TM, w), lambda bi, i: (bi, blk(i), 0)),
                pl.BlockSpec((1, TM, w), lambda bi, i: (bi, blk(i), f_col)),
                pl.BlockSpec((1, TM, w), lambda bi, i: (bi, blk(i), 3)),
                pl.BlockSpec((8, w), lambda bi, i: (0, 0))]

    out_spec = lambda rev: pl.BlockSpec((1, TM, w), lambda bi, i: (bi, _scan_block(i, nt, rev), 0))
    return pl.pallas_call(
        _hgrn_kernel,
        grid=(b, nt),
        in_specs=dir_specs(False) + dir_specs(True),
        out_specs=[out_spec(False), out_spec(True)],
        out_shape=[jax.ShapeDtypeStruct((b, tt, w), BF16)] * 2,
        scratch_shapes=[pltpu.VMEM((n_heads, HG_HEAD_DIM, HG_HEAD_DIM), F32)] * 2,
        compiler_params=_cparams(("parallel", "arbitrary")),
        name="hgrn_scan",
    )(hraw, hraw, hraw, lb_rows_f, hraw, hraw, hraw, lb_rows_b)


def _outproj0_kernel(na_ref, of_ref, ob_ref, gate_ref, x_ref, c_ref, m_ref, gain_ref, w_ref, o_ref, *, n_lat_tiles):
    na_w = na_ref.shape[2]
    for bi in range(BB):
        acc = jnp.dot(na_ref[bi], w_ref[0:na_w, :], preferred_element_type=F32)
        o = of_ref[bi].astype(F32) + ob_ref[bi].astype(F32)
        gate = _silu(gate_ref[bi])
        ys = []
        for h in range(o.shape[1] // HG_HEAD_DIM):
            hs = slice(h * HG_HEAD_DIM, (h + 1) * HG_HEAD_DIM)
            oh = o[:, hs]
            ms = jnp.mean(oh * oh, axis=-1, keepdims=True)
            ys.append(oh * lax.rsqrt(ms + EPS) * gain_ref[:, hs] * gate[:, hs])
        acc = acc + jnp.dot(jnp.concatenate(ys, axis=-1).astype(BF16), w_ref[na_w:, :], preferred_element_type=F32)
        o_ref[bi] = _stream0_rows(x_ref, c_ref, bi, n_lat_tiles) + m_ref[bi, 2:3, :] * acc


def _outproj0(o_na, o_f, o_b, hraw, x, ctx, mod, gain, w):
    b, t_lat, d = x.shape
    tt = t_lat + ctx.shape[1]
    na_w = o_na.shape[2]
    hw = o_f.shape[2]
    n_lat_tiles = t_lat // TM
    return pl.pallas_call(
        functools.partial(_outproj0_kernel, n_lat_tiles=n_lat_tiles),
        grid=(b // BB, n_lat_tiles + 1),
        in_specs=[_tile_spec(na_w), _tile_spec(hw), _tile_spec(hw),
                  pl.BlockSpec((BB, TM, hw), lambda i, t: (i, t, 4))] + _stream0_specs(d, n_lat_tiles) + [
            _mod_spec(d, n_lat_tiles), _const_spec((1, hw)), _const_spec(w.shape)],
        out_specs=_tile_spec(d),
        out_shape=jax.ShapeDtypeStruct((b, tt, d), F32),
        compiler_params=_cparams(("parallel", "parallel")),
        name="outproj0",
    )(o_na, o_f, o_b, hraw, x, ctx, mod, gain, w)


HALO = 8
TM_EXT = TM + 2 * HALO


def _halo_specs(width, tt):
    r8 = TM // HALO
    last = tt // HALO - 1
    return [pl.BlockSpec((BB, HALO, width), lambda i, t: (i, jnp.maximum(t * r8 - 1, 0), 0)),
            pl.BlockSpec((BB, HALO, width), lambda i, t: (i, jnp.minimum((t + 1) * r8, last), 0))]


def _modulated_with_halo(h_ref, hp_ref, hn_ref, gain, m_ref, shift_row, scale_row):
    us, exts = [], []
    for bi in range(BB):
        shift = m_ref[bi, shift_row:shift_row + 1, :]
        scale = m_ref[bi, scale_row:scale_row + 1, :]
        u = _rms_mod(h_ref[bi], gain, shift, scale)
        us.append(u.astype(BF16))
        exts += [_rms_mod(hp_ref[bi], gain, shift, scale), u, _rms_mod(hn_ref[bi], gain, shift, scale)]
    return jnp.concatenate(us, axis=0), jnp.concatenate(exts, axis=0).astype(BF16)


def _conv3_ext(a_ext, cw, cb, n_lat_tiles):
    n = a_ext.shape[0] - 2 * HALO
    t = pl.program_id(1)
    has_prev = jnp.logical_and(t != 0, t != n_lat_tiles)
    has_next = jnp.logical_and(t != n_lat_tiles - 1, t != n_lat_tiles)
    a = a_ext[HALO:HALO + n]
    prev_row = jnp.where(has_prev, a_ext[HALO - 1:HALO], 0.0)
    next_row = jnp.where(has_next, a_ext[HALO + n:HALO + n + 1], 0.0)
    row = lax.broadcasted_iota(jnp.int32, (n, 1), 0)
    up = jnp.where(row == 0, prev_row, pltpu.roll(a, 1, 0))
    dn = jnp.where(row == n - 1, next_row, pltpu.roll(a, n - 1, 0))
    return cw[0:1, :] * up + cw[1:2, :] * a + cw[2:3, :] * dn + cb


def _ffn_up_kernel(h_ref, hp_ref, hn_ref, m_ref, gain_ref, w_ref, cw_ref, cb_ref, mid_ref, *, n_lat_tiles):
    u, u_ext = _modulated_with_halo(h_ref, hp_ref, hn_ref, gain_ref[...], m_ref, 3, 4)
    dff = mid_ref.shape[2]
    for lo, hi in _col_chunks(dff, 768):
        a_ext = jnp.dot(u_ext, w_ref[:, lo:hi], preferred_element_type=F32)
        v = jnp.dot(u, w_ref[:, dff + lo:dff + hi], preferred_element_type=F32)
        for bi in range(BB):
            c = _conv3_ext(a_ext[bi * TM_EXT:(bi + 1) * TM_EXT], cw_ref[:, lo:hi], cb_ref[:, lo:hi], n_lat_tiles)
            mid_ref[bi, :, lo:hi] = (_gelu_tanh(c) * v[bi * TM:(bi + 1) * TM]).astype(mid_ref.dtype)


def _ffn_up(hh, mod, gain, w, conv_w, conv_b, n_tiles, n_lat_tiles):
    b, tt_in, d = hh.shape
    dff = w.shape[1] // 2
    return pl.pallas_call(
        functools.partial(_ffn_up_kernel, n_lat_tiles=n_lat_tiles),
        grid=(b // BB, n_tiles),
        in_specs=[_tile_spec(d)] + _halo_specs(d, tt_in) + [
            _mod_spec(d, n_lat_tiles), _const_spec((1, d)), _const_spec(w.shape),
            _const_spec((3, dff)), _const_spec((1, dff))],
        out_specs=_tile_spec(dff),
        out_shape=jax.ShapeDtypeStruct((b, n_tiles * TM, dff), BF16),
        compiler_params=_cparams(("parallel", "parallel")),
        name="ffn_up",
    )(hh, hh, hh, mod, gain, w, conv_w, conv_b)


def _ffn_down_kernel(mid_ref, w_ref, h_ref, m_ref, o_ref):
    y = jnp.dot(jnp.concatenate([mid_ref[bi] for bi in range(BB)], axis=0), w_ref[...],
                preferred_element_type=F32)
    for bi in range(BB):
        o_ref[bi] = h_ref[bi] + m_ref[bi, 5:6, :] * y[bi * TM:(bi + 1) * TM]


def _ffn_down(mid, w, hh, mod, n_lat_tiles):
    b, tt, dff = mid.shape
    d = hh.shape[2]
    return pl.pallas_call(
        _ffn_down_kernel,
        grid=(b // BB, tt // TM),
        in_specs=[_tile_spec(dff), _const_spec(w.shape), _tile_spec(d), _mod_spec(d, n_lat_tiles)],
        out_specs=_tile_spec(d),
        out_shape=jax.ShapeDtypeStruct((b, tt, d), F32),
        compiler_params=_cparams(("parallel", "parallel")),
        name="ffn_down",
    )(mid, w, hh, mod)


def _inproj1_kernel(h_ref, hp_ref, hn_ref, m_ref, gain_ref, w_ref, cw_ref, cb_ref, dtb_ref, a_ref,
                    z_ref, xs_ref, bc_ref, dtp_ref, *, n_lat_tiles):
    u, u_ext = _modulated_with_halo(h_ref, hp_ref, hn_ref, gain_ref[...], m_ref, 0, 1)
    nz = z_ref.shape[2]
    nxs = xs_ref.shape[2]
    nx = nxs + bc_ref.shape[2]
    step = 512
    for lo, hi in _col_chunks(nz, step):
        z = _silu(jnp.dot(u, w_ref[:, lo:hi], preferred_element_type=F32))
        for bi in range(BB):
            z_ref[bi, :, lo:hi] = z[bi * TM:(bi + 1) * TM].astype(z_ref.dtype)
    for lo, hi in _col_chunks(nx, step):
        xbc_ext = jnp.dot(u_ext, w_ref[:, nz + lo:nz + hi], preferred_element_type=F32)
        for bi in range(BB):
            c = _silu(_conv3_ext(xbc_ext[bi * TM_EXT:(bi + 1) * TM_EXT], cw_ref[:, lo:hi], cb_ref[:, lo:hi],
                                 n_lat_tiles))
            if hi <= nxs:
                xs_ref[bi, :, lo:hi] = c.astype(xs_ref.dtype)
            else:
                bc_ref[bi, :, lo - nxs:hi - nxs] = c.astype(bc_ref.dtype)
    raw = jnp.dot(u, w_ref[:, nz + nx:], preferred_element_type=F32)
    dt = _softplus(raw + dtb_ref[...])
    la = dt * a_ref[...]
    for bi in range(BB):
        dtp_ref[bi, :, 0:2 * LANES] = dt[bi * TM:(bi + 1) * TM]
        dtp_ref[bi, :, 2 * LANES:4 * LANES] = la[bi * TM:(bi + 1) * TM]


def _inproj1(hh, mod, gain, w, conv_w, conv_b, dt_bias, a_neg, nz, nxs):
    b, tt, d = hh.shape
    nt = tt // TM
    n_lat_tiles = nt - 1
    nx = conv_w.shape[1]
    assert nxs % 512 == 0
    tile = _tile_spec
    return pl.pallas_call(
        functools.partial(_inproj1_kernel, n_lat_tiles=n_lat_tiles),
        grid=(b // BB, nt),
        in_specs=[tile(d)] + _halo_specs(d, tt) + [
            _mod_spec(d, n_lat_tiles), _const_spec((1, d)), _const_spec(w.shape),
            _const_spec((3, nx)), _const_spec((1, nx)),
            _const_spec((1, 2 * LANES)), _const_spec((1, 2 * LANES))],
        out_specs=[tile(nz), tile(nxs), tile(nx - nxs), tile(4 * LANES)],
        out_shape=[jax.ShapeDtypeStruct((b, tt, nz), BF16),
                   jax.ShapeDtypeStruct((b, tt, nxs), BF16),
                   jax.ShapeDtypeStruct((b, tt, nx - nxs), BF16),
                   jax.ShapeDtypeStruct((b, tt, 4 * LANES), F32)],
        compiler_params=_cparams(("parallel", "parallel")),
        name="inproj1",
    )(hh, hh, hh, mod, gain, w, conv_w, conv_b, dt_bias, a_neg)


class _SsdDir:
    def __init__(self, rev, x_ref, bc_ref, dt_ref, la_ref, e_ref, o_ref, s_ref):
        self.rev, self.x_ref, self.bc_ref, self.dt_ref, self.la_ref = rev, x_ref, bc_ref, dt_ref, la_ref
        self.e_ref, self.o_ref, self.s_ref = e_ref, o_ref, s_ref
        self.n_groups = s_ref.shape[0]
        self.gw = s_ref.shape[2]
        self.n_heads = x_ref.shape[2] // SSD_HEAD_DIM
        self.n_chunks = x_ref.shape[1] // CHUNK
        self.states = [s_ref[g] for g in range(self.n_groups)]
        self.tri = _tri_matrix(CHUNK, rev)
        rowc = lax.broadcasted_iota(jnp.int32, (CHUNK, self.gw), 0)
        pos = lax.broadcasted_iota(jnp.int32, (CHUNK, self.gw), 1) % SSD_HEAD_DIM
        self.on_diag = pos == rowc
        self.causal = (pos >= rowc) if rev else (pos <= rowc)

    def chunk(self, k):
        return self.n_chunks - 1 - k if self.rev else k

    def _stacked(self, v):
        lane = lax.broadcasted_iota(jnp.int32, (1, LANES), 1)
        p1, p2, p3 = _split3(v)
        return jnp.where(lane < self.n_heads, p1, jnp.where(lane < 2 * self.n_heads, p2,
                                                            jnp.where(lane < 3 * self.n_heads, p3, jnp.zeros_like(p1))))

    def scalars(self, cc):
        rs = slice(cc * CHUNK, (cc + 1) * CHUNK)
        return (self._stacked(_tri_cumsum(self.tri, self.la_ref[0, rs, :])), self._stacked(self.dt_ref[0, rs, :]))

    def expand(self, g, scalars):
        cum_st, dt_st = scalars
        gs = slice(g * self.gw, (g + 1) * self.gw)
        return (jnp.dot(cum_st, self.e_ref[:, gs], preferred_element_type=F32),
                jnp.dot(dt_st, self.e_ref[:, gs], preferred_element_type=F32))

    def prep(self, cc, g, expanded):
        cum, dt = expanded
        rs = slice(cc * CHUNK, (cc + 1) * CHUNK)
        gs = slice(g * self.gw, (g + 1) * self.gw)
        last = 0 if self.rev else CHUNK - 1
        cum_row = jnp.sum(jnp.where(self.on_diag, cum, 0.0), axis=0, keepdims=True)
        cum_last = cum[last:last + 1, :]
        x_dt = self.x_ref[0, rs, gs].astype(F32) * dt
        return dict(
            xw=(x_dt * jnp.exp(cum_last - cum)).astype(BF16),
            x_dt=x_dt.astype(BF16),
            decay=jnp.exp(jnp.where(self.causal, cum - cum_row, MASK_NEG)),
            e_cum=jnp.exp(cum),
            e_last=jnp.exp(cum_last))

    def first_matmuls(self, cc, g, p):
        rs = slice(cc * CHUNK, (cc + 1) * CHUNK)
        b_g = self.bc_ref[0, rs, g * SSD_STATE:(g + 1) * SSD_STATE]
        c_g = self.bc_ref[0, rs, (self.n_groups + g) * SSD_STATE:(self.n_groups + g + 1) * SSD_STATE]
        b_rep = jnp.concatenate([b_g] * (self.gw // CHUNK), axis=0)
        return dict(cb=_dot_nt(c_g, b_rep), read=_dot(c_g, self.states[g]),
                    update=lax.dot_general(b_g, p["xw"], (((0,), (0,)), ((), ())), preferred_element_type=F32))

    def intra(self, g, p, first):
        blk = 256
        blk_head = lax.broadcasted_iota(jnp.int32, (1, blk), 1) // SSD_HEAD_DIM
        m_g = (p["decay"] * first["cb"]).astype(BF16)
        y_parts = []
        for j in range(self.gw // blk):
            x4 = p["x_dt"][:, j * blk:(j + 1) * blk]
            x_bd = jnp.concatenate(
                [jnp.where(blk_head == hh, x4, jnp.zeros_like(x4)) for hh in range(blk // SSD_HEAD_DIM)], axis=0)
            y_parts.append(jnp.dot(m_g[:, j * blk:(j + 1) * blk], x_bd, preferred_element_type=F32))
        return jnp.concatenate(y_parts, axis=-1)

    def combine(self, cc, g, p, first, y_intra):
        rs = slice(cc * CHUNK, (cc + 1) * CHUNK)
        gs = slice(g * self.gw, (g + 1) * self.gw)
        self.o_ref[0, rs, gs] = (y_intra + first["read"] * p["e_cum"]).astype(self.o_ref.dtype)
        self.states[g] = self.states[g] * p["e_last"] + first["update"]

    def finish(self):
        for g in range(self.n_groups):
            self.s_ref[g] = self.states[g]


def _ssd_kernel(xf_ref, bcf_ref, dtf_ref, laf_ref, xb_ref, bcb_ref, dtb_ref, lab_ref, e_ref,
                of_ref, ob_ref, sf_ref, sb_ref):
    @pl.when(pl.program_id(1) == 0)
    def _():
        sf_ref[...] = jnp.zeros_like(sf_ref)
        sb_ref[...] = jnp.zeros_like(sb_ref)

    fwd = _SsdDir(False, xf_ref, bcf_ref, dtf_ref, laf_ref, e_ref, of_ref, sf_ref)
    bwd = _SsdDir(True, xb_ref, bcb_ref, dtb_ref, lab_ref, e_ref, ob_ref, sb_ref)
    groups = range(fwd.n_groups)
    steps = [(d, k) for k in range(fwd.n_chunks) for d in (fwd, bwd)]
    d0, k0 = steps[0]
    sc = d0.scalars(d0.chunk(k0))
    ready = [d0.prep(d0.chunk(k0), g, d0.expand(g, sc)) for g in groups]
    for idx, (d, k) in enumerate(steps):
        nxt = steps[idx + 1] if idx + 1 < len(steps) else None
        if nxt is not None:
            nd, nk = nxt
            nsc = nd.scalars(nd.chunk(nk))
        cc = d.chunk(k)
        first = [d.first_matmuls(cc, g, ready[g]) for g in groups]
        following = []
        for g in groups:
            y_intra = d.intra(g, ready[g], first[g])
            if nxt is not None:
                following.append(nd.prep(nd.chunk(nk), g, nd.expand(g, nsc)))
            d.combine(cc, g, ready[g], first[g], y_intra)
        ready = following
    fwd.finish()
    bwd.finish()


def _ssd_scan(xs, bc, dtp, expand_mat):
    b, tt, width = xs.shape
    nt = tt // TM
    nbc = bc.shape[2]
    gw = width // SSD_GROUPS

    def dir_specs(rev):
        blk = lambda i: _scan_block(i, nt, rev)
        d_col = 1 if rev else 0
        return [pl.BlockSpec((1, TM, width), lambda bi, i: (bi, blk(i), 0)),
                pl.BlockSpec((1, TM, nbc), lambda bi, i: (bi, blk(i), 0)),
                pl.BlockSpec((1, TM, LANES), lambda bi, i: (bi, blk(i), d_col)),
                pl.BlockSpec((1, TM, LANES), lambda bi, i: (bi, blk(i), 2 + d_col))]

    out_spec = lambda rev: pl.BlockSpec((1, TM, width), lambda bi, i: (bi, _scan_block(i, nt, rev), 0))
    return pl.pallas_call(
        _ssd_kernel,
        grid=(b, nt),
        in_specs=dir_specs(False) + dir_specs(True) + [pl.BlockSpec(expand_mat.shape, lambda bi, i: (0, 0))],
        out_specs=[out_spec(False), out_spec(True)],
        out_shape=[jax.ShapeDtypeStruct((b, tt, width), BF16)] * 2,
        scratch_shapes=[pltpu.VMEM((SSD_GROUPS, SSD_STATE, gw), F32)] * 2,
        compiler_params=_cparams(("parallel", "arbitrary")),
        name="ssd_scan",
    )(xs, bc, dtp, dtp, xs, bc, dtp, dtp, expand_mat)


def _outproj1_kernel(yf_ref, yb_ref, xs_ref, sz_ref, h_ref, m_ref, d_ref, gain_ref, w_ref, o_ref):
    width = xs_ref.shape[2]
    gw = width // SSD_GROUPS
    for bi in range(BB):
        acc = None
        for g in range(SSD_GROUPS):
            gs = slice(g * gw, (g + 1) * gw)
            yg = ((yf_ref[bi, :, gs].astype(F32) + yb_ref[bi, :, gs].astype(F32)
                   + d_ref[:, gs] * xs_ref[bi, :, gs].astype(F32)) * sz_ref[bi, :, gs].astype(F32))
            ms = jnp.mean(yg * yg, axis=-1, keepdims=True)
            part = jnp.dot((yg * lax.rsqrt(ms + EPS) * gain_ref[:, gs]).astype(BF16), w_ref[gs, :],
                           preferred_element_type=F32)
            acc = part if acc is None else acc + part
        o_ref[bi] = h_ref[bi] + m_ref[bi, 2:3, :] * acc


def _outproj1(y_f, y_b, xs, sz, hh, mod, d_skip, gain, w, n_lat_tiles):
    b, _, d = hh.shape
    width = xs.shape[2]
    tile = _tile_spec
    return pl.pallas_call(
        _outproj1_kernel,
        grid=(b // BB, n_lat_tiles),
        in_specs=[tile(width), tile(width), tile(width), tile(width), tile(d),
                  _mod_spec(d, n_lat_tiles), _const_spec((1, width)), _const_spec((1, width)),
                  _const_spec(w.shape)],
        out_specs=tile(d),
        out_shape=jax.ShapeDtypeStruct((b, n_lat_tiles * TM, d), F32),
        compiler_params=_cparams(("parallel", "parallel")),
        name="outproj1",
    )(y_f, y_b, xs, sz, hh, mod, d_skip, gain, w)


def _mod_rows(mods_l, b, d):
    six = mods_l.reshape(mods_l.shape[0], 6, d)
    lat = six[:b]
    ctx = jnp.broadcast_to(six[b:b + 1], (b, 6, d))
    both = jnp.stack([lat, ctx], axis=1)
    return jnp.pad(both, ((0, 0), (0, 0), (0, 2), (0, 0)))


def kernel(x, c, ctx, c_ctx, w_mod, b_mod, norm_mix, norm_ffn, ffn_w_up, ffn_conv_w, ffn_conv_b, ffn_w_down,
           hy_w_in, hy_w_out, na_q_gain, na_k_gain, na_rpb, hg_out_gain, hg_lb_fwd, hg_lb_bwd, ssd_w_in,
           ssd_conv_w, ssd_conv_b, ssd_dt_bias_fwd, ssd_dt_bias_bwd, ssd_a_log_fwd, ssd_a_log_bwd, ssd_d,
           ssd_norm_gain, ssd_w_out):
    b, t_lat, d = x.shape
    l_ctx = ctx.shape[1]
    assert l_ctx == TM and t_lat % TM == 0 and w_mod.shape[0] == 2 and b % BB == 0
    n_lat_tiles = t_lat // TM
    nt = n_lat_tiles + 1

    rows = ((b + 1 + 7) // 8) * 8
    cond = jnp.concatenate([c, c_ctx[None], jnp.zeros((rows - b - 1, d), F32)], axis=0)
    mods = _modulation(cond, w_mod, b_mod)
    mod0 = _mod_rows(mods[0], b, d)
    mod1 = _mod_rows(mods[1], b, d)
    row = lambda v: v.reshape(1, -1).astype(F32)

    hg_w = hg_lb_fwd.shape[1]
    na_w = hy_w_out.shape[1] - hg_w
    n_na_heads = na_w // NA_HEAD_DIM
    qg = row(jnp.tile(na_q_gain[0], n_na_heads)) * (NA_HEAD_DIM ** -0.5)
    kg = row(jnp.tile(na_k_gain[0], n_na_heads))
    qkv, hraw = _inproj0(x, ctx, mod0, row(norm_mix[0]), hy_w_in[0].astype(BF16), qg, kg)
    bias = _na_bias_table(na_rpb[0], t_lat // GRID_W)
    o_na = _na_attention(qkv, bias, t_lat)

    def lb_rows(lb_param):
        lb = jnp.cumsum(jax.nn.softmax(lb_param.astype(F32), axis=0), axis=0)[0]
        return jnp.pad(jnp.stack([lb, 1.0 - lb], axis=0), ((0, 6), (0, 0)))

    o_f, o_b = _hgrn_scan(hraw, lb_rows(hg_lb_fwd), lb_rows(hg_lb_bwd))
    hg_gain = row(jnp.tile(hg_out_gain[0], hg_w // HG_HEAD_DIM))
    h1 = _outproj0(o_na, o_f, o_b, hraw, x, ctx, mod0, hg_gain, hy_w_out[0].astype(BF16))

    mid0 = _ffn_up(h1, mod0, row(norm_ffn[0]), ffn_w_up[0].astype(BF16), ffn_conv_w[0], row(ffn_conv_b[0]),
                   nt, n_lat_tiles)
    h2 = _ffn_down(mid0, ffn_w_down[0].astype(BF16), h1, mod0, n_lat_tiles)

    n_heads = ssd_d.shape[1]
    inner = ssd_w_out.shape[1]
    nxbc = ssd_conv_w.shape[2]
    w1 = ssd_w_in[0]
    rep = LANES // n_heads
    w_dtf = jnp.tile(w1[:, inner + nxbc:inner + nxbc + n_heads], (1, rep))
    w_dtb = jnp.tile(w1[:, inner + nxbc + n_heads:], (1, rep))
    w1p = jnp.concatenate([w1[:, :inner + nxbc], w_dtf, w_dtb], axis=1).astype(BF16)
    dt_bias = row(jnp.concatenate([jnp.tile(ssd_dt_bias_fwd[0], rep), jnp.tile(ssd_dt_bias_bwd[0], rep)]))
    a_neg = row(jnp.concatenate([jnp.tile(-jnp.exp(ssd_a_log_fwd[0].astype(F32)), rep),
                                 jnp.tile(-jnp.exp(ssd_a_log_bwd[0].astype(F32)), rep)]))
    z, xs, bc, dtp = _inproj1(h2, mod1, row(norm_mix[1]), w1p, ssd_conv_w[0], row(ssd_conv_b[0]), dt_bias,
                              a_neg, inner, inner)

    lane = np.arange(LANES)[:, None]
    colh = (np.arange(inner) // SSD_HEAD_DIM)[None, :]
    expand_mat = jnp.asarray(((lane % n_heads == colh) & (lane < 3 * n_heads)).astype(np.float32), dtype=BF16)
    y_f, y_b = _ssd_scan(xs, bc, dtp, expand_mat)
    d_skip = row(jnp.repeat(ssd_d[0], SSD_HEAD_DIM))
    h3 = _outproj1(y_f, y_b, xs, z, h2, mod1, d_skip, row(ssd_norm_gain[0]), ssd_w_out[0].astype(BF16),
                   n_lat_tiles)

    mid1 = _ffn_up(h3, mod1, row(norm_ffn[1]), ffn_w_up[1].astype(BF16), ffn_conv_w[1], row(ffn_conv_b[1]),
                   n_lat_tiles, n_lat_tiles)
    return _ffn_down(mid1, ffn_w_down[1].astype(BF16), h3, mod1, n_lat_tiles)
```

```python
import functools
import math

import numpy as np
import jax
import jax.numpy as jnp
from jax import lax
from jax.experimental import pallas as pl
from jax.experimental.pallas import tpu as pltpu

F32 = jnp.float32
BF16 = jnp.bfloat16
EPS = 1e-6

GRID_W = 64
NA_HEAD_DIM = 64
WIN_ROWS = 8
WIN_COLS = 16
NA_ROW_BLOCK = 4
HG_HEAD_DIM = 128
SSD_HEAD_DIM = 64
SSD_GROUPS = 4
SSD_STATE = 128
CHUNK = 64
SUB = 16
TM = 256
BB = 2
LANES = 128
MASK_NEG = -1e30
SAFE_RANGE = 60.0

VMEM_LIMIT = 56 * 1024 * 1024


def _cparams(sem):
    return pltpu.CompilerParams(dimension_semantics=sem, vmem_limit_bytes=VMEM_LIMIT)


def _dot(a, b):
    return jnp.dot(a.astype(BF16), b.astype(BF16), preferred_element_type=F32)


def _dot_nt(a, b):
    return lax.dot_general(a.astype(BF16), b.astype(BF16), (((1,), (1,)), ((), ())),
                           preferred_element_type=F32)


def _silu(x):
    h = 0.5 * x
    return h + h * jnp.tanh(h)


def _softplus(x):
    return jnp.maximum(x, 0.0) + jnp.log1p(jnp.exp(-jnp.abs(x)))


def _gelu_tanh(x):
    c = math.sqrt(2.0 / math.pi)
    return 0.5 * x * (1.0 + jnp.tanh(c * (x + 0.044715 * (x * x * x))))


def _rms_mod(x, gain, shift, scale):
    ms = jnp.mean(x * x, axis=-1, keepdims=True)
    return (x * lax.rsqrt(ms + EPS) * gain) * (1.0 + scale) + shift


def _split3(v):
    p1 = v.astype(BF16)
    r1 = v - p1.astype(F32)
    p2 = r1.astype(BF16)
    r2 = r1 - p2.astype(F32)
    return p1, p2, r2.astype(BF16)


def _tri_cumsum(tri, v):
    p1, p2, p3 = _split3(v)
    d = lambda p: jnp.dot(tri, p, preferred_element_type=F32)
    return d(p1) + d(p2) + d(p3)


def _mod_spec(d, n_lat_tiles):
    return pl.BlockSpec((BB, None, 8, d), lambda i, t: (i, t // n_lat_tiles, 0, 0))


def _tile_spec(width):
    return pl.BlockSpec((BB, TM, width), lambda i, t: (i, t, 0))


def _const_spec(shape):
    return pl.BlockSpec(shape, lambda i, t: (0,) * len(shape))


def _stream0_specs(d, n_lat_tiles):
    return [pl.BlockSpec((BB, TM, d), lambda i, t: (i, jnp.minimum(t, n_lat_tiles - 1), 0)),
            pl.BlockSpec((BB, TM, d), lambda i, t: (i, 0, 0))]


def _stream0_rows(x_ref, c_ref, bi, n_lat_tiles):
    return jnp.where(pl.program_id(1) == n_lat_tiles, c_ref[bi], x_ref[bi])


def _col_chunks(width, step):
    return [(lo, min(lo + step, width)) for lo in range(0, width, step)]


def _tri_matrix(n, rev):
    r = lax.broadcasted_iota(jnp.int32, (n, n), 0)
    c = lax.broadcasted_iota(jnp.int32, (n, n), 1)
    return jnp.where((c >= r) if rev else (c <= r), 1.0, 0.0).astype(BF16)


def _mod_kernel(s_ref, w_ref, b_ref, o_ref):
    s = _silu(s_ref[...])
    o_ref[0] = _dot(s, w_ref[0]) + b_ref[0]


def _modulation(cond, w_mod, b_mod):
    depth, d, n = w_mod.shape
    rows = cond.shape[0]
    tn = 1536
    return pl.pallas_call(
        _mod_kernel,
        grid=(depth, n // tn),
        in_specs=[pl.BlockSpec((rows, d), lambda l, j: (0, 0)),
                  pl.BlockSpec((1, d, tn), lambda l, j: (l, 0, j)),
                  pl.BlockSpec((1, 1, tn), lambda l, j: (l, 0, j))],
        out_specs=pl.BlockSpec((1, rows, tn), lambda l, j: (l, 0, j)),
        out_shape=jax.ShapeDtypeStruct((depth, rows, n), F32),
        compiler_params=_cparams(("parallel", "parallel")),
        name="modulation",
    )(cond, w_mod, b_mod.reshape(depth, 1, n))


def _inproj0_kernel(x_ref, c_ref, m_ref, gain_ref, w_ref, qg_ref, kg_ref, qkv_ref, hraw_ref, *, n_lat_tiles):
    u = jnp.concatenate(
        [_rms_mod(_stream0_rows(x_ref, c_ref, bi, n_lat_tiles), gain_ref[...], m_ref[bi, 0:1, :],
                  m_ref[bi, 1:2, :]).astype(BF16) for bi in range(BB)], axis=0)
    lo = lax.broadcasted_iota(jnp.int32, (1, LANES), 1) < NA_HEAD_DIM
    na_w = qg_ref.shape[1]

    def head_norm(y, g_ref):
        outs = []
        for c in range(na_w // LANES):
            yc = y[:, c * LANES:(c + 1) * LANES]
            sq = yc * yc
            s_lo = jnp.sum(jnp.where(lo, sq, 0.0), axis=-1, keepdims=True)
            s_hi = jnp.sum(jnp.where(lo, 0.0, sq), axis=-1, keepdims=True)
            inv = jnp.where(lo, lax.rsqrt(s_lo / NA_HEAD_DIM + EPS), lax.rsqrt(s_hi / NA_HEAD_DIM + EPS))
            outs.append(yc * inv * g_ref[:, c * LANES:(c + 1) * LANES])
        return jnp.concatenate(outs, axis=-1)

    def proj(lo_col, width):
        return jnp.dot(u, w_ref[:, lo_col:lo_col + width], preferred_element_type=F32)

    def store(ref, lo_col, y):
        for bi in range(BB):
            ref[bi, :, lo_col:lo_col + y.shape[1]] = y[bi * TM:(bi + 1) * TM].astype(ref.dtype)

    store(qkv_ref, 0, head_norm(proj(0, na_w), qg_ref))
    store(qkv_ref, na_w, head_norm(proj(na_w, na_w), kg_ref))
    store(qkv_ref, 2 * na_w, proj(2 * na_w, na_w))
    for lo_col, hi_col in _col_chunks(hraw_ref.shape[2], 512):
        store(hraw_ref, lo_col, proj(3 * na_w + lo_col, hi_col - lo_col))


def _inproj0(x, ctx, mod, gain, w, qg, kg):
    b, t_lat, d = x.shape
    n = w.shape[1]
    na_w = qg.shape[1]
    n_hg = n - 3 * na_w
    n_lat_tiles = t_lat // TM
    tt = t_lat + ctx.shape[1]
    return pl.pallas_call(
        functools.partial(_inproj0_kernel, n_lat_tiles=n_lat_tiles),
        grid=(b // BB, n_lat_tiles + 1),
        in_specs=_stream0_specs(d, n_lat_tiles) + [
            _mod_spec(d, n_lat_tiles), _const_spec((1, d)), _const_spec((d, n)),
            _const_spec((1, na_w)), _const_spec((1, na_w))],
        out_specs=[_tile_spec(3 * na_w), _tile_spec(n_hg)],
        out_shape=[jax.ShapeDtypeStruct((b, tt, 3 * na_w), BF16),
                   jax.ShapeDtypeStruct((b, tt, n_hg), F32)],
        compiler_params=_cparams(("parallel", "parallel")),
        name="inproj0",
    )(x, ctx, mod, gain, w, qg, kg)


def _na_kernel(q_ref, k_ref, v_ref, bias_ref, o_ref, *, t_lat, rows):
    tt = q_ref.shape[1]
    lane = lax.broadcasted_iota(jnp.int32, (1, LANES), 1)
    head_mask = [lane < NA_HEAD_DIM, lane >= NA_HEAD_DIM]
    kc = k_ref[0, t_lat:tt, :]
    vc = v_ref[0, t_lat:tt, :]
    _, ku, u0s, _, case_of_block = _na_block_plan(rows)
    n_loc = ku * GRID_W
    n_q = NA_ROW_BLOCK * GRID_W

    qc = q_ref[0, t_lat:tt, :]
    n_c = tt - t_lat
    s = _dot_nt(jnp.concatenate([jnp.where(hm, qc, jnp.zeros_like(qc)) for hm in head_mask], axis=0), kc)
    p = jnp.exp(s - jnp.max(s, axis=-1, keepdims=True))
    o_both = _dot(p, vc) / jnp.sum(p, axis=-1, keepdims=True)
    oc = None
    for hi, hm in enumerate(head_mask):
        o = o_both[hi * n_c:(hi + 1) * n_c]
        oc = o if oc is None else jnp.where(hm, o, oc)
    o_ref[0, t_lat:tt, :] = oc.astype(o_ref.dtype)

    n_blocks = rows // NA_ROW_BLOCK
    group = 2 if n_blocks % 2 == 0 else 1

    def blocks_body(it, carry):
        loaded, scores = [], []
        for j in range(group):
            i = it * group + j
            u0 = u0s[0]
            case = case_of_block[0]
            for bi in range(1, len(u0s)):
                u0 = jnp.where(i == bi, u0s[bi], u0)
                case = jnp.where(i == bi, case_of_block[bi], case)
            q_rows = pl.ds(pl.multiple_of(i * n_q, n_q), n_q)
            q_i = q_ref[0, q_rows, :]
            start = pl.multiple_of(u0 * GRID_W, GRID_W)
            k_loc = k_ref[0, pl.ds(start, n_loc), :]
            loaded.append((q_rows, v_ref[0, pl.ds(start, n_loc), :]))
            q_both = jnp.concatenate([jnp.where(hm, q_i, jnp.zeros_like(q_i)) for hm in head_mask], axis=0)
            s_loc = _dot_nt(q_both, k_loc)
            s_ctx = _dot_nt(q_both, kc)
            for hi in range(len(head_mask)):
                hr = slice(hi * n_q, (hi + 1) * n_q)
                scores.append((s_loc[hr] + bias_ref[hi, case], s_ctx[hr]))
        probs = []
        for s_loc, s_ctx in scores:
            m = jnp.maximum(jnp.max(s_loc, axis=-1, keepdims=True), jnp.max(s_ctx, axis=-1, keepdims=True))
            p_loc = jnp.exp(s_loc - m)
            p_ctx = jnp.exp(s_ctx - m)
            den = jnp.sum(p_loc, axis=-1, keepdims=True) + jnp.sum(p_ctx, axis=-1, keepdims=True)
            probs.append((p_loc.astype(BF16), p_ctx.astype(BF16), den))
        n_hd = len(head_mask)
        for j, (q_rows, v_loc) in enumerate(loaded):
            mine = probs[j * n_hd:(j + 1) * n_hd]
            o_both = (jnp.dot(jnp.concatenate([p[0] for p in mine], axis=0), v_loc, preferred_element_type=F32)
                      + jnp.dot(jnp.concatenate([p[1] for p in mine], axis=0), vc, preferred_element_type=F32))
            out = None
            for hi, hm in enumerate(head_mask):
                o = o_both[hi * n_q:(hi + 1) * n_q] / mine[hi][2]
                out = o if out is None else jnp.where(hm, o, out)
            o_ref[0, q_rows, :] = out.astype(o_ref.dtype)
        return carry

    lax.fori_loop(0, n_blocks // group, blocks_body, 0)


def _na_block_plan(rows):
    kr = min(WIN_ROWS, rows)
    ku = min(kr + NA_ROW_BLOCK - 1, rows)
    u0s, patterns, case_of_block = [], [], []
    for i in range(rows // NA_ROW_BLOCK):
        u0 = int(np.clip(NA_ROW_BLOCK * i - kr // 2, 0, rows - ku))
        rel = []
        for a in range(NA_ROW_BLOCK):
            r = NA_ROW_BLOCK * i + a
            r0 = int(np.clip(r - kr // 2, 0, rows - kr))
            assert u0 <= r0 and r0 + kr <= u0 + ku
            rel.append((r0 - u0, u0 - r))
        u0s.append(u0)
        if tuple(rel) not in patterns:
            patterns.append(tuple(rel))
        case_of_block.append(patterns.index(tuple(rel)))
    return kr, ku, u0s, patterns, case_of_block


def _na_bias_table(rpb, rows):
    kr, ku, _, patterns, _ = _na_block_plan(rows)
    q = np.arange(GRID_W)
    kcol = np.arange(GRID_W)
    ws = np.clip(q - WIN_COLS // 2, 0, GRID_W - WIN_COLS)
    in_win = (kcol[None, :] >= ws[:, None]) & (kcol[None, :] < ws[:, None] + WIN_COLS)
    dc = np.clip(kcol[None, :] - q[:, None] + WIN_COLS - 1, 0, 2 * WIN_COLS - 2)
    u = np.arange(ku)
    n_dr = 2 * WIN_ROWS - 1
    dr = np.full((len(patterns), NA_ROW_BLOCK, ku), n_dr, np.int32)
    for c, rel in enumerate(patterns):
        for a, (r0_rel, u0_minus_r) in enumerate(rel):
            row_ok = (u >= r0_rel) & (u < r0_rel + kr)
            dr[c, a] = np.where(row_ok, u0_minus_r + u + WIN_ROWS - 1, n_dr)
    h = rpb.shape[0]
    tiles = jnp.where(in_win[None, None], rpb.astype(F32)[:, :, dc], MASK_NEG)
    tiles = jnp.concatenate([tiles, jnp.full((h, 1, GRID_W, GRID_W), MASK_NEG, F32)], axis=1)
    tbl = jnp.take(tiles, jnp.asarray(dr.reshape(-1)), axis=1)
    tbl = tbl.reshape(h, len(patterns), NA_ROW_BLOCK, ku, GRID_W, GRID_W)
    return tbl.transpose(0, 1, 2, 4, 3, 5).reshape(h, len(patterns), NA_ROW_BLOCK * GRID_W, ku * GRID_W)


def _na_attention(qkv, bias, t_lat):
    b, tt, w3 = qkv.shape
    na_w = w3 // 3
    ncol = na_w // LANES
    rows = t_lat // GRID_W
    bias_block = (2,) + bias.shape[1:]
    kern = functools.partial(_na_kernel, t_lat=t_lat, rows=rows)
    return pl.pallas_call(
        kern,
        grid=(b, ncol),
        in_specs=[pl.BlockSpec((1, tt, LANES), lambda i, p: (i, 0, p)),
                  pl.BlockSpec((1, tt, LANES), lambda i, p: (i, 0, ncol + p)),
                  pl.BlockSpec((1, tt, LANES), lambda i, p: (i, 0, 2 * ncol + p)),
                  pl.BlockSpec(bias_block, lambda i, p: (p, 0, 0, 0))],
        out_specs=pl.BlockSpec((1, tt, LANES), lambda i, p: (i, 0, p)),
        out_shape=jax.ShapeDtypeStruct((b, tt, na_w), BF16),
        compiler_params=_cparams(("parallel", "parallel")),
        name="na_attention",
    )(qkv, qkv, qkv, bias)


def _scan_block(i, nt, rev):
    if rev:
        return jnp.where(i == 0, nt - 1, nt - 1 - i)
    return jnp.where(i == 0, nt - 1, i - 1)


class _HgrnDir:
    def __init__(self, rev, q_ref, f_ref, v_ref, lb_ref, o_ref, st_ref):
        self.rev, self.q_ref, self.f_ref, self.v_ref, self.o_ref, self.st_ref = rev, q_ref, f_ref, v_ref, o_ref, st_ref
        self.n_heads = st_ref.shape[0]
        self.lb = lb_ref[0:1, :]
        self.one_m_lb = lb_ref[1:2, :]
        self.n_chunks = q_ref.shape[1] // CHUNK
        self.order = list(range(self.n_chunks - 1, -1, -1)) if rev else list(range(self.n_chunks))
        self.heads = [slice(h * HG_HEAD_DIM, (h + 1) * HG_HEAD_DIM) for h in range(self.n_heads)]
        self.states = [st_ref[h] for h in range(self.n_heads)]
        self.gated = {}
        self.tri = _tri_matrix(CHUNK, rev)
        row = lax.broadcasted_iota(jnp.int32, (CHUNK, CHUNK), 0)
        col = lax.broadcasted_iota(jnp.int32, (CHUNK, CHUNK), 1)
        self.causal = (col >= row) if rev else (col <= row)

    def gate(self, cc):
        lb, one_m_lb, q_ref, f_ref = self.lb, self.one_m_lb, self.q_ref, self.f_ref
        rs = slice(cc * CHUNK, (cc + 1) * CHUNK)
        x = f_ref[0, rs, :]
        e = jnp.exp(-jnp.abs(x))
        r = 1.0 / (1.0 + e)
        x_pos = x >= 0.0
        f = jnp.where(x_pos, 1.0 + lb * e, lb + e) * r
        log_f = jnp.where(f > 0.0, jnp.log(f), x)
        k_all = one_m_lb * jnp.where(x_pos, e, 1.0) * r
        return _silu(q_ref[0, rs, :]), k_all, log_f

    def decays(self, cc, gates):
        rev = self.rev
        q_all, k_all, log_f = gates
        n_sub = CHUNK // SUB
        zero_row = jnp.zeros((1, log_f.shape[1]), F32)
        scan_blocks = list(range(n_sub - 1, -1, -1)) if rev else list(range(n_sub))
        rng = None
        g_all = _tri_cumsum(self.tri, log_f)
        c = [g_all[j * SUB:j * SUB + 1, :] if rev else g_all[(j + 1) * SUB - 1:(j + 1) * SUB, :]
             for j in range(n_sub)]
        c_prev = [zero_row] * n_sub
        for before, after in zip(scan_blocks[:-1], scan_blocks[1:]):
            c_prev[after] = c[before]
        for j in range(n_sub):
            d = c_prev[j] - c[j]
            rng = d if rng is None else jnp.maximum(rng, d)
        self.gated[cc] = (q_all, k_all, g_all, c, c_prev)
        return rng

    def slow_diag(self):
        rev = self.rev
        row = lax.broadcasted_iota(jnp.int32, (CHUNK, CHUNK), 0)
        col = lax.broadcasted_iota(jnp.int32, (CHUNK, CHUNK), 1)
        rowv = lax.broadcasted_iota(jnp.int32, (CHUNK, 1), 0)
        pos = rowv % SUB
        accs = []
        for cc in self.order:
            q_all, k_all, g_all = self.gated[cc][:3]
            for hs in self.heads:
                q, k, g = q_all[:, hs], k_all[:, hs], g_all[:, hs]
                acc = jnp.zeros((CHUNK, CHUNK), F32)
                for dlt in range(SUB):
                    shift = (CHUNK - dlt) % CHUNK if rev else dlt
                    k_d = pltpu.roll(k, shift, 0) if shift else k
                    g_d = pltpu.roll(g, shift, 0) if shift else g
                    valid = (pos + dlt <= SUB - 1) if rev else (pos >= dlt)
                    e = jnp.exp(jnp.where(valid, g - g_d, 0.0))
                    val = jnp.sum(q * k_d * e, axis=-1, keepdims=True)
                    partner = (row + dlt) if rev else (row - dlt)
                    acc = acc + jnp.where((col == partner) & valid, val, 0.0)
                accs.append(acc)
        return jnp.stack(accs, axis=0)

    def prepare(self, ci, safe):
        rev = self.rev
        n_sub = CHUNK // SUB
        cc = self.order[ci]
        q_all, k_all, g_all, c, c_prev = self.gated[cc]
        width = g_all.shape[1]
        scan_blocks = list(range(n_sub - 1, -1, -1)) if rev else list(range(n_sub))
        c_final = c[scan_blocks[-1]]
        blk = lambda x, j: x[j * SUB:(j + 1) * SUB]

        def per_block(fn):
            return jnp.concatenate([fn(j) for j in range(n_sub)], axis=0)

        q_b = per_block(lambda j: blk(q_all, j) * jnp.exp(blk(g_all, j) - c_prev[j]))
        k_end = per_block(lambda j: blk(k_all, j) * jnp.exp(c[j] - blk(g_all, j)))
        k_comb = []
        for sb in range(n_sub):
            def scaled(j, sb=sb):
                if j == sb:
                    return blk(k_end, j) * jnp.where(safe, jnp.exp(jnp.minimum(c_prev[sb] - c[sb], SAFE_RANGE)), 0.0)
                if scan_blocks.index(j) < scan_blocks.index(sb):
                    return blk(k_end, j) * jnp.exp(c_prev[sb] - c[j])
                return jnp.zeros((SUB, width), F32)
            k_comb.append(per_block(scaled))
        return dict(
            rows=slice(cc * CHUNK, (cc + 1) * CHUNK), q_b=q_b, k_comb=k_comb,
            q_g=per_block(lambda j: blk(q_b, j) * jnp.exp(c_prev[j])),
            k_last=per_block(lambda j: blk(k_end, j) * jnp.exp(c_final - c[j])),
            e_last=jnp.exp(c_final), v=self.v_ref[0, slice(cc * CHUNK, (cc + 1) * CHUNK), :], outs=[])

    def scores(self, h, p, diag_block):
        n_sub = CHUNK // SUB
        hs = self.heads[h]
        blocks = [_dot_nt(p["q_b"][sb * SUB:(sb + 1) * SUB, hs], p["k_comb"][sb][:, hs]) for sb in range(n_sub)]
        return jnp.where(self.causal, jnp.concatenate(blocks, axis=0), 0.0) + diag_block

    def values(self, h, p, a):
        hs = self.heads[h]
        st = self.states[h]
        p["outs"].append(_dot(a, p["v"][:, hs]) + _dot_nt(p["q_g"][:, hs], st))
        self.states[h] = st * p["e_last"][:, hs] + lax.dot_general(
            p["v"][:, hs].astype(BF16), p["k_last"][:, hs].astype(BF16), (((0,), (0,)), ((), ())),
            preferred_element_type=F32)

    def store(self, p):
        self.o_ref[0, p["rows"], :] = jnp.concatenate(p["outs"], axis=-1).astype(self.o_ref.dtype)

    def finish(self):
        for h in range(self.n_heads):
            self.st_ref[h] = self.states[h]


def _hgrn_kernel(qf_ref, ff_ref, vf_ref, lbf_ref, qb_ref, fb_ref, vb_ref, lbb_ref, of_ref, ob_ref, sf_ref, sb_ref):
    @pl.when(pl.program_id(1) == 0)
    def _():
        sf_ref[...] = jnp.zeros_like(sf_ref)
        sb_ref[...] = jnp.zeros_like(sb_ref)

    dirs = [_HgrnDir(False, qf_ref, ff_ref, vf_ref, lbf_ref, of_ref, sf_ref),
            _HgrnDir(True, qb_ref, fb_ref, vb_ref, lbb_ref, ob_ref, sb_ref)]
    n_chunks, n_heads = dirs[0].n_chunks, dirs[0].n_heads
    rng = None
    stage1 = [(d, d.order[ci]) for ci in range(n_chunks) for d in dirs]
    gates = [d.gate(cc) for d, cc in stage1]
    for (d, cc), gt in zip(stage1, gates):
        r = d.decays(cc, gt)
        rng = r if rng is None else jnp.maximum(rng, r)
    safe = jnp.max(rng) <= SAFE_RANGE
    per_dir = n_chunks * n_heads
    diag = lax.cond(safe, lambda: jnp.zeros((2 * per_dir, CHUNK, CHUNK), F32),
                    lambda: jnp.concatenate([d.slow_diag() for d in dirs], axis=0))
    for ci in range(n_chunks):
        preps = [d.prepare(ci, safe) for d in dirs]
        chains = [(h, di, d) for h in range(n_heads) for di, d in enumerate(dirs)]
        att = [d.scores(h, preps[di], diag[di * per_dir + ci * n_heads + h]) for h, di, d in chains]
        for (h, di, d), a in zip(chains, att):
            d.values(h, preps[di], a)
        for di, d in enumerate(dirs):
            d.store(preps[di])
    for d in dirs:
        d.finish()


def _hgrn_scan(hraw, lb_rows_f, lb_rows_b):
    b, tt, n = hraw.shape
    w = lb_rows_f.shape[1]
    nt = tt // TM
    n_heads = w // HG_HEAD_DIM

    def dir_specs(rev):
        blk = lambda i: _scan_block(i, nt, rev)
        f_col = 2 if rev else 1
        return [pl.BlockSpec((1, TM, w), lambda bi, i: (bi, blk(i), 0)),
                pl.BlockSpec((1, TM, w), lambda bi, i: (bi, blk(i), f_col)),
                pl.BlockSpec((1, TM, w), lambda bi, i: (bi, blk(i), 3)),
                pl.BlockSpec((8, w), lambda bi, i: (0, 0))]

    out_spec = lambda rev: pl.BlockSpec((1, TM, w), lambda bi, i: (bi, _scan_block(i, nt, rev), 0))
    return pl.pallas_call(
        _hgrn_kernel,
        grid=(b, nt),
        in_specs=dir_specs(False) + dir_specs(True),
        out_specs=[out_spec(False), out_spec(True)],
        out_shape=[jax.ShapeDtypeStruct((b, tt, w), BF16)] * 2,
        scratch_shapes=[pltpu.VMEM((n_heads, HG_HEAD_DIM, HG_HEAD_DIM), F32)] * 2,
        compiler_params=_cparams(("parallel", "arbitrary")),
        name="hgrn_scan",
    )(hraw, hraw, hraw, lb_rows_f, hraw, hraw, hraw, lb_rows_b)


def _outproj0_kernel(na_ref, of_ref, ob_ref, gate_ref, x_ref, c_ref, m_ref, gain_ref, w_ref, o_ref, *, n_lat_tiles):
    na_w = na_ref.shape[2]
    for bi in range(BB):
        acc = jnp.dot(na_ref[bi], w_ref[0:na_w, :], preferred_element_type=F32)
        o = of_ref[bi].astype(F32) + ob_ref[bi].astype(F32)
        gate = _silu(gate_ref[bi])
        ys = []
        for h in range(o.shape[1] // HG_HEAD_DIM):
            hs = slice(h * HG_HEAD_DIM, (h + 1) * HG_HEAD_DIM)
            oh = o[:, hs]
            ms = jnp.mean(oh * oh, axis=-1, keepdims=True)
            ys.append(oh * lax.rsqrt(ms + EPS) * gain_ref[:, hs] * gate[:, hs])
        acc = acc + jnp.dot(jnp.concatenate(ys, axis=-1).astype(BF16), w_ref[na_w:, :], preferred_element_type=F32)
        o_ref[bi] = _stream0_rows(x_ref, c_ref, bi, n_lat_tiles) + m_ref[bi, 2:3, :] * acc


def _outproj0(o_na, o_f, o_b, hraw, x, ctx, mod, gain, w):
    b, t_lat, d = x.shape
    tt = t_lat + ctx.shape[1]
    na_w = o_na.shape[2]
    hw = o_f.shape[2]
    n_lat_tiles = t_lat // TM
    return pl.pallas_call(
        functools.partial(_outproj0_kernel, n_lat_tiles=n_lat_tiles),
        grid=(b // BB, n_lat_tiles + 1),
        in_specs=[_tile_spec(na_w), _tile_spec(hw), _tile_spec(hw),
                  pl.BlockSpec((BB, TM, hw), lambda i, t: (i, t, 4))] + _stream0_specs(d, n_lat_tiles) + [
            _mod_spec(d, n_lat_tiles), _const_spec((1, hw)), _const_spec(w.shape)],
        out_specs=_tile_spec(d),
        out_shape=jax.ShapeDtypeStruct((b, tt, d), F32),
        compiler_params=_cparams(("parallel", "parallel")),
        name="outproj0",
    )(o_na, o_f, o_b, hraw, x, ctx, mod, gain, w)


HALO = 8
TM_EXT = TM + 2 * HALO


def _halo_specs(width, tt):
    r8 = TM // HALO
    last = tt // HALO - 1
    return [pl.BlockSpec((BB, HALO, width), lambda i, t: (i, jnp.maximum(t * r8 - 1, 0), 0)),
            pl.BlockSpec((BB, HALO, width), lambda i, t: (i, jnp.minimum((t + 1) * r8, last), 0))]


def _modulated_with_halo(h_ref, hp_ref, hn_ref, gain, m_ref, shift_row, scale_row):
    us, exts = [], []
    for bi in range(BB):
        shift = m_ref[bi, shift_row:shift_row + 1, :]
        scale = m_ref[bi, scale_row:scale_row + 1, :]
        u = _rms_mod(h_ref[bi], gain, shift, scale)
        us.append(u.astype(BF16))
        exts += [_rms_mod(hp_ref[bi], gain, shift, scale), u, _rms_mod(hn_ref[bi], gain, shift, scale)]
    return jnp.concatenate(us, axis=0), jnp.concatenate(exts, axis=0).astype(BF16)


def _conv3_ext(a_ext, cw, cb, n_lat_tiles):
    n = a_ext.shape[0] - 2 * HALO
    t = pl.program_id(1)
    has_prev = jnp.logical_and(t != 0, t != n_lat_tiles)
    has_next = jnp.logical_and(t != n_lat_tiles - 1, t != n_lat_tiles)
    a = a_ext[HALO:HALO + n]
    prev_row = jnp.where(has_prev, a_ext[HALO - 1:HALO], 0.0)
    next_row = jnp.where(has_next, a_ext[HALO + n:HALO + n + 1], 0.0)
    row = lax.broadcasted_iota(jnp.int32, (n, 1), 0)
    up = jnp.where(row == 0, prev_row, pltpu.roll(a, 1, 0))
    dn = jnp.where(row == n - 1, next_row, pltpu.roll(a, n - 1, 0))
    return cw[0:1, :] * up + cw[1:2, :] * a + cw[2:3, :] * dn + cb


def _ffn_up_kernel(h_ref, hp_ref, hn_ref, m_ref, gain_ref, w_ref, cw_ref, cb_ref, mid_ref, *, n_lat_tiles):
    u, u_ext = _modulated_with_halo(h_ref, hp_ref, hn_ref, gain_ref[...], m_ref, 3, 4)
    dff = mid_ref.shape[2]
    for lo, hi in _col_chunks(dff, 768):
        a_ext = jnp.dot(u_ext, w_ref[:, lo:hi], preferred_element_type=F32)
        v = jnp.dot(u, w_ref[:, dff + lo:dff + hi], preferred_element_type=F32)
        for bi in range(BB):
            c = _conv3_ext(a_ext[bi * TM_EXT:(bi + 1) * TM_EXT], cw_ref[:, lo:hi], cb_ref[:, lo:hi], n_lat_tiles)
            mid_ref[bi, :, lo:hi] = (_gelu_tanh(c) * v[bi * TM:(bi + 1) * TM]).astype(mid_ref.dtype)


def _ffn_up(hh, mod, gain, w, conv_w, conv_b, n_tiles, n_lat_tiles):
    b, tt_in, d = hh.shape
    dff = w.shape[1] // 2
    return pl.pallas_call(
        functools.partial(_ffn_up_kernel, n_lat_tiles=n_lat_tiles),
        grid=(b // BB, n_tiles),
        in_specs=[_tile_spec(d)] + _halo_specs(d, tt_in) + [
            _mod_spec(d, n_lat_tiles), _const_spec((1, d)), _const_spec(w.shape),
            _const_spec((3, dff)), _const_spec((1, dff))],
        out_specs=_tile_spec(dff),
        out_shape=jax.ShapeDtypeStruct((b, n_tiles * TM, dff), BF16),
        compiler_params=_cparams(("parallel", "parallel")),
        name="ffn_up",
    )(hh, hh, hh, mod, gain, w, conv_w, conv_b)


def _ffn_down_kernel(mid_ref, w_ref, h_ref, m_ref, o_ref):
    y = jnp.dot(jnp.concatenate([mid_ref[bi] for bi in range(BB)], axis=0), w_ref[...],
                preferred_element_type=F32)
    for bi in range(BB):
        o_ref[bi] = h_ref[bi] + m_ref[bi, 5:6, :] * y[bi * TM:(bi + 1) * TM]


def _ffn_down(mid, w, hh, mod, n_lat_tiles):
    b, tt, dff = mid.shape
    d = hh.shape[2]
    return pl.pallas_call(
        _ffn_down_kernel,
        grid=(b // BB, tt // TM),
        in_specs=[_tile_spec(dff), _const_spec(w.shape), _tile_spec(d), _mod_spec(d, n_lat_tiles)],
        out_specs=_tile_spec(d),
        out_shape=jax.ShapeDtypeStruct((b, tt, d), F32),
        compiler_params=_cparams(("parallel", "parallel")),
        name="ffn_down",
    )(mid, w, hh, mod)


def _inproj1_kernel(h_ref, hp_ref, hn_ref, m_ref, gain_ref, w_ref, cw_ref, cb_ref, dtb_ref, a_ref,
                    z_ref, xs_ref, bc_ref, dtp_ref, *, n_lat_tiles):
    u, u_ext = _modulated_with_halo(h_ref, hp_ref, hn_ref, gain_ref[...], m_ref, 0, 1)
    nz = z_ref.shape[2]
    nxs = xs_ref.shape[2]
    nx = nxs + bc_ref.shape[2]
    step = 512
    for lo, hi in _col_chunks(nz, step):
        z = jnp.dot(u, w_ref[:, lo:hi], preferred_element_type=F32)
        for bi in range(BB):
            z_ref[bi, :, lo:hi] = z[bi * TM:(bi + 1) * TM].astype(z_ref.dtype)
    for lo, hi in _col_chunks(nx, step):
        xbc_ext = jnp.dot(u_ext, w_ref[:, nz + lo:nz + hi], preferred_element_type=F32)
        for bi in range(BB):
            c = _silu(_conv3_ext(xbc_ext[bi * TM_EXT:(bi + 1) * TM_EXT], cw_ref[:, lo:hi], cb_ref[:, lo:hi],
                                 n_lat_tiles))
            if hi <= nxs:
                xs_ref[bi, :, lo:hi] = c.astype(xs_ref.dtype)
            else:
                bc_ref[bi, :, lo - nxs:hi - nxs] = c.astype(bc_ref.dtype)
    raw = jnp.dot(u, w_ref[:, nz + nx:], preferred_element_type=F32)
    dt = _softplus(raw + dtb_ref[...])
    la = dt * a_ref[...]
    for bi in range(BB):
        dtp_ref[bi, :, 0:2 * LANES] = dt[bi * TM:(bi + 1) * TM]
        dtp_ref[bi, :, 2 * LANES:4 * LANES] = la[bi * TM:(bi + 1) * TM]


def _inproj1(hh, mod, gain, w, conv_w, conv_b, dt_bias, a_neg, nz, nxs):
    b, tt, d = hh.shape
    nt = tt // TM
    n_lat_tiles = nt - 1
    nx = conv_w.shape[1]
    assert nxs % 512 == 0
    tile = _tile_spec
    return pl.pallas_call(
        functools.partial(_inproj1_kernel, n_lat_tiles=n_lat_tiles),
        grid=(b // BB, nt),
        in_specs=[tile(d)] + _halo_specs(d, tt) + [
            _mod_spec(d, n_lat_tiles), _const_spec((1, d)), _const_spec(w.shape),
            _const_spec((3, nx)), _const_spec((1, nx)),
            _const_spec((1, 2 * LANES)), _const_spec((1, 2 * LANES))],
        out_specs=[tile(nz), tile(nxs), tile(nx - nxs), tile(4 * LANES)],
        out_shape=[jax.ShapeDtypeStruct((b, tt, nz), BF16),
                   jax.ShapeDtypeStruct((b, tt, nxs), BF16),
                   jax.ShapeDtypeStruct((b, tt, nx - nxs), BF16),
                   jax.ShapeDtypeStruct((b, tt, 4 * LANES), F32)],
        compiler_params=_cparams(("parallel", "parallel")),
        name="inproj1",
    )(hh, hh, hh, mod, gain, w, conv_w, conv_b, dt_bias, a_neg)


class _SsdDir:
    def __init__(self, rev, x_ref, bc_ref, dt_ref, la_ref, e_ref, o_ref, s_ref, skip_ref=None):
        self.rev, self.x_ref, self.bc_ref, self.dt_ref, self.la_ref = rev, x_ref, bc_ref, dt_ref, la_ref
        self.e_ref, self.o_ref, self.s_ref, self.skip_ref = e_ref, o_ref, s_ref, skip_ref
        self.n_groups = s_ref.shape[0]
        self.gw = s_ref.shape[2]
        self.n_heads = x_ref.shape[2] // SSD_HEAD_DIM
        self.n_chunks = x_ref.shape[1] // CHUNK
        self.states = [s_ref[g] for g in range(self.n_groups)]
        self.tri = _tri_matrix(CHUNK, rev)
        rowc = lax.broadcasted_iota(jnp.int32, (CHUNK, self.gw), 0)
        pos = lax.broadcasted_iota(jnp.int32, (CHUNK, self.gw), 1) % SSD_HEAD_DIM
        self.on_diag = pos == rowc
        self.causal = (pos >= rowc) if rev else (pos <= rowc)

    def chunk(self, k):
        return self.n_chunks - 1 - k if self.rev else k

    def _stacked(self, v):
        lane = lax.broadcasted_iota(jnp.int32, (1, LANES), 1)
        p1, p2, p3 = _split3(v)
        return jnp.where(lane < self.n_heads, p1, jnp.where(lane < 2 * self.n_heads, p2,
                                                            jnp.where(lane < 3 * self.n_heads, p3, jnp.zeros_like(p1))))

    def scalars(self, cc):
        rs = slice(cc * CHUNK, (cc + 1) * CHUNK)
        return (self._stacked(_tri_cumsum(self.tri, self.la_ref[0, rs, :])), self._stacked(self.dt_ref[0, rs, :]))

    def expand(self, g, scalars):
        cum_st, dt_st = scalars
        gs = slice(g * self.gw, (g + 1) * self.gw)
        return (jnp.dot(cum_st, self.e_ref[:, gs], preferred_element_type=F32),
                jnp.dot(dt_st, self.e_ref[:, gs], preferred_element_type=F32))

    def prep(self, cc, g, expanded):
        cum, dt = expanded
        rs = slice(cc * CHUNK, (cc + 1) * CHUNK)
        gs = slice(g * self.gw, (g + 1) * self.gw)
        last = 0 if self.rev else CHUNK - 1
        cum_row = jnp.sum(jnp.where(self.on_diag, cum, 0.0), axis=0, keepdims=True)
        cum_last = cum[last:last + 1, :]
        x = self.x_ref[0, rs, gs].astype(F32)
        x_dt = x * dt
        return dict(
            skip=None if self.skip_ref is None else x * self.skip_ref[:, gs],
            xw=(x_dt * jnp.exp(cum_last - cum)).astype(BF16),
            x_dt=x_dt.astype(BF16),
            decay=jnp.exp(jnp.where(self.causal, cum - cum_row, MASK_NEG)),
            e_cum=jnp.exp(cum),
            e_last=jnp.exp(cum_last))

    def first_matmuls(self, cc, g, p):
        rs = slice(cc * CHUNK, (cc + 1) * CHUNK)
        b_g = self.bc_ref[0, rs, g * SSD_STATE:(g + 1) * SSD_STATE]
        c_g = self.bc_ref[0, rs, (self.n_groups + g) * SSD_STATE:(self.n_groups + g + 1) * SSD_STATE]
        b_rep = jnp.concatenate([b_g] * (self.gw // CHUNK), axis=0)
        return dict(cb=_dot_nt(c_g, b_rep), read=_dot(c_g, self.states[g]),
                    update=lax.dot_general(b_g, p["xw"], (((0,), (0,)), ((), ())), preferred_element_type=F32))

    def intra(self, g, p, first):
        blk = 256
        blk_head = lax.broadcasted_iota(jnp.int32, (1, blk), 1) // SSD_HEAD_DIM
        m_g = (p["decay"] * first["cb"]).astype(BF16)
        y_parts = []
        for j in range(self.gw // blk):
            x4 = p["x_dt"][:, j * blk:(j + 1) * blk]
            x_bd = jnp.concatenate(
                [jnp.where(blk_head == hh, x4, jnp.zeros_like(x4)) for hh in range(blk // SSD_HEAD_DIM)], axis=0)
            y_parts.append(jnp.dot(m_g[:, j * blk:(j + 1) * blk], x_bd, preferred_element_type=F32))
        return jnp.concatenate(y_parts, axis=-1)

    def combine(self, cc, g, p, first, y_intra):
        rs = slice(cc * CHUNK, (cc + 1) * CHUNK)
        gs = slice(g * self.gw, (g + 1) * self.gw)
        y = y_intra + first["read"] * p["e_cum"]
        if p["skip"] is not None:
            y = y + p["skip"]
        self.o_ref[0, rs, gs] = y.astype(self.o_ref.dtype)
        self.states[g] = self.states[g] * p["e_last"] + first["update"]

    def finish(self):
        for g in range(self.n_groups):
            self.s_ref[g] = self.states[g]


def _ssd_kernel(xf_ref, bcf_ref, dtf_ref, laf_ref, xb_ref, bcb_ref, dtb_ref, lab_ref, e_ref, skip_ref,
                of_ref, ob_ref, sf_ref, sb_ref):
    @pl.when(pl.program_id(1) == 0)
    def _():
        sf_ref[...] = jnp.zeros_like(sf_ref)
        sb_ref[...] = jnp.zeros_like(sb_ref)

    fwd = _SsdDir(False, xf_ref, bcf_ref, dtf_ref, laf_ref, e_ref, of_ref, sf_ref, skip_ref)
    bwd = _SsdDir(True, xb_ref, bcb_ref, dtb_ref, lab_ref, e_ref, ob_ref, sb_ref)
    groups = range(fwd.n_groups)
    steps = [(d, k) for k in range(fwd.n_chunks) for d in (fwd, bwd)]
    d0, k0 = steps[0]
    sc = d0.scalars(d0.chunk(k0))
    ready = [d0.prep(d0.chunk(k0), g, d0.expand(g, sc)) for g in groups]
    for idx, (d, k) in enumerate(steps):
        nxt = steps[idx + 1] if idx + 1 < len(steps) else None
        if nxt is not None:
            nd, nk = nxt
            nsc = nd.scalars(nd.chunk(nk))
        cc = d.chunk(k)
        first = [d.first_matmuls(cc, g, ready[g]) for g in groups]
        following = []
        for g in groups:
            y_intra = d.intra(g, ready[g], first[g])
            if nxt is not None:
                following.append(nd.prep(nd.chunk(nk), g, nd.expand(g, nsc)))
            d.combine(cc, g, ready[g], first[g], y_intra)
        ready = following
    fwd.finish()
    bwd.finish()


def _ssd_scan(xs, bc, dtp, expand_mat, d_skip):
    b, tt, width = xs.shape
    nt = tt // TM
    nbc = bc.shape[2]
    gw = width // SSD_GROUPS

    def dir_specs(rev):
        blk = lambda i: _scan_block(i, nt, rev)
        d_col = 1 if rev else 0
        return [pl.BlockSpec((1, TM, width), lambda bi, i: (bi, blk(i), 0)),
                pl.BlockSpec((1, TM, nbc), lambda bi, i: (bi, blk(i), 0)),
                pl.BlockSpec((1, TM, LANES), lambda bi, i: (bi, blk(i), d_col)),
                pl.BlockSpec((1, TM, LANES), lambda bi, i: (bi, blk(i), 2 + d_col))]

    out_spec = lambda rev: pl.BlockSpec((1, TM, width), lambda bi, i: (bi, _scan_block(i, nt, rev), 0))
    return pl.pallas_call(
        _ssd_kernel,
        grid=(b, nt),
        in_specs=dir_specs(False) + dir_specs(True) + [pl.BlockSpec(expand_mat.shape, lambda bi, i: (0, 0)),
                                                       pl.BlockSpec((1, width), lambda bi, i: (0, 0))],
        out_specs=[out_spec(False), out_spec(True)],
        out_shape=[jax.ShapeDtypeStruct((b, tt, width), BF16)] * 2,
        scratch_shapes=[pltpu.VMEM((SSD_GROUPS, SSD_STATE, gw), F32)] * 2,
        compiler_params=_cparams(("parallel", "arbitrary")),
        name="ssd_scan",
    )(xs, bc, dtp, dtp, xs, bc, dtp, dtp, expand_mat, d_skip)


def _outproj1_kernel(yf_ref, yb_ref, z_ref, h_ref, m_ref, gain_ref, w_ref, o_ref):
    width = z_ref.shape[2]
    gw = width // SSD_GROUPS
    for bi in range(BB):
        acc = None
        for g in range(SSD_GROUPS):
            gs = slice(g * gw, (g + 1) * gw)
            yg = (yf_ref[bi, :, gs].astype(F32) + yb_ref[bi, :, gs].astype(F32)) * _silu(z_ref[bi, :, gs].astype(F32))
            ms = jnp.mean(yg * yg, axis=-1, keepdims=True)
            part = jnp.dot((yg * lax.rsqrt(ms + EPS) * gain_ref[:, gs]).astype(BF16), w_ref[gs, :],
                           preferred_element_type=F32)
            acc = part if acc is None else acc + part
        o_ref[bi] = h_ref[bi] + m_ref[bi, 2:3, :] * acc


def _outproj1(y_f, y_b, z, hh, mod, gain, w, n_lat_tiles):
    b, _, d = hh.shape
    width = z.shape[2]
    tile = _tile_spec
    return pl.pallas_call(
        _outproj1_kernel,
        grid=(b // BB, n_lat_tiles),
        in_specs=[tile(width), tile(width), tile(width), tile(d),
                  _mod_spec(d, n_lat_tiles), _const_spec((1, width)), _const_spec(w.shape)],
        out_specs=tile(d),
        out_shape=jax.ShapeDtypeStruct((b, n_lat_tiles * TM, d), F32),
        compiler_params=_cparams(("parallel", "parallel")),
        name="outproj1",
    )(y_f, y_b, z, hh, mod, gain, w)


def _mod_rows(mods_l, b, d):
    six = mods_l.reshape(mods_l.shape[0], 6, d)
    lat = six[:b]
    ctx = jnp.broadcast_to(six[b:b + 1], (b, 6, d))
    both = jnp.stack([lat, ctx], axis=1)
    return jnp.pad(both, ((0, 0), (0, 0), (0, 2), (0, 0)))


def kernel(x, c, ctx, c_ctx, w_mod, b_mod, norm_mix, norm_ffn, ffn_w_up, ffn_conv_w, ffn_conv_b, ffn_w_down,
           hy_w_in, hy_w_out, na_q_gain, na_k_gain, na_rpb, hg_out_gain, hg_lb_fwd, hg_lb_bwd, ssd_w_in,
           ssd_conv_w, ssd_conv_b, ssd_dt_bias_fwd, ssd_dt_bias_bwd, ssd_a_log_fwd, ssd_a_log_bwd, ssd_d,
           ssd_norm_gain, ssd_w_out):
    b, t_lat, d = x.shape
    l_ctx = ctx.shape[1]
    assert l_ctx == TM and t_lat % TM == 0 and w_mod.shape[0] == 2 and b % BB == 0
    n_lat_tiles = t_lat // TM
    nt = n_lat_tiles + 1

    rows = ((b + 1 + 7) // 8) * 8
    cond = jnp.concatenate([c, c_ctx[None], jnp.zeros((rows - b - 1, d), F32)], axis=0)
    mods = _modulation(cond, w_mod, b_mod)
    mod0 = _mod_rows(mods[0], b, d)
    mod1 = _mod_rows(mods[1], b, d)
    row = lambda v: v.reshape(1, -1).astype(F32)

    hg_w = hg_lb_fwd.shape[1]
    na_w = hy_w_out.shape[1] - hg_w
    n_na_heads = na_w // NA_HEAD_DIM
    qg = row(jnp.tile(na_q_gain[0], n_na_heads)) * (NA_HEAD_DIM ** -0.5)
    kg = row(jnp.tile(na_k_gain[0], n_na_heads))
    qkv, hraw = _inproj0(x, ctx, mod0, row(norm_mix[0]), hy_w_in[0].astype(BF16), qg, kg)
    bias = _na_bias_table(na_rpb[0], t_lat // GRID_W)
    o_na = _na_attention(qkv, bias, t_lat)

    def lb_rows(lb_param):
        lb = jnp.cumsum(jax.nn.softmax(lb_param.astype(F32), axis=0), axis=0)[0]
        return jnp.pad(jnp.stack([lb, 1.0 - lb], axis=0), ((0, 6), (0, 0)))

    o_f, o_b = _hgrn_scan(hraw, lb_rows(hg_lb_fwd), lb_rows(hg_lb_bwd))
    hg_gain = row(jnp.tile(hg_out_gain[0], hg_w // HG_HEAD_DIM))
    h1 = _outproj0(o_na, o_f, o_b, hraw, x, ctx, mod0, hg_gain, hy_w_out[0].astype(BF16))

    mid0 = _ffn_up(h1, mod0, row(norm_ffn[0]), ffn_w_up[0].astype(BF16), ffn_conv_w[0], row(ffn_conv_b[0]),
                   nt, n_lat_tiles)
    h2 = _ffn_down(mid0, ffn_w_down[0].astype(BF16), h1, mod0, n_lat_tiles)

    n_heads = ssd_d.shape[1]
    inner = ssd_w_out.shape[1]
    nxbc = ssd_conv_w.shape[2]
    w1 = ssd_w_in[0]
    rep = LANES // n_heads
    w_dtf = jnp.tile(w1[:, inner + nxbc:inner + nxbc + n_heads], (1, rep))
    w_dtb = jnp.tile(w1[:, inner + nxbc + n_heads:], (1, rep))
    w1p = jnp.concatenate([w1[:, :inner + nxbc], w_dtf, w_dtb], axis=1).astype(BF16)
    dt_bias = row(jnp.concatenate([jnp.tile(ssd_dt_bias_fwd[0], rep), jnp.tile(ssd_dt_bias_bwd[0], rep)]))
    a_neg = row(jnp.concatenate([jnp.tile(-jnp.exp(ssd_a_log_fwd[0].astype(F32)), rep),
                                 jnp.tile(-jnp.exp(ssd_a_log_bwd[0].astype(F32)), rep)]))
    z, xs, bc, dtp = _inproj1(h2, mod1, row(norm_mix[1]), w1p, ssd_conv_w[0], row(ssd_conv_b[0]), dt_bias,
                              a_neg, inner, inner)

    lane = np.arange(LANES)[:, None]
    colh = (np.arange(inner) // SSD_HEAD_DIM)[None, :]
    expand_mat = jnp.asarray(((lane % n_heads == colh) & (lane < 3 * n_heads)).astype(np.float32), dtype=BF16)
    d_skip = row(jnp.repeat(ssd_d[0], SSD_HEAD_DIM))
    y_f, y_b = _ssd_scan(xs, bc, dtp, expand_mat, d_skip)
    h3 = _outproj1(y_f, y_b, z, h2, mod1, row(ssd_norm_gain[0]), ssd_w_out[0].astype(BF16), n_lat_tiles)

    mid1 = _ffn_up(h3, mod1, row(norm_ffn[1]), ffn_w_up[1].astype(BF16), ffn_conv_w[1], row(ffn_conv_b[1]),
                   n_lat_tiles, n_lat_tiles)
    return _ffn_down(mid1, ffn_w_down[1].astype(BF16), h3, mod1, n_lat_tiles)
```

```python
import functools
import math

import numpy as np
import jax
import jax.numpy as jnp
from jax import lax
from jax.experimental import pallas as pl
from jax.experimental.pallas import tpu as pltpu

F32 = jnp.float32
BF16 = jnp.bfloat16
EPS = 1e-6

GRID_W = 64
NA_HEAD_DIM = 64
WIN_ROWS = 8
WIN_COLS = 16
NA_ROW_BLOCK = 4
HG_HEAD_DIM = 128
SSD_HEAD_DIM = 64
SSD_GROUPS = 4
SSD_STATE = 128
CHUNK = 64
SUB = 16
TM = 256
BB = 2
LANES = 128
MASK_NEG = -1e30
SAFE_RANGE = 60.0

VMEM_LIMIT = 56 * 1024 * 1024


def _cparams(sem):
    return pltpu.CompilerParams(dimension_semantics=sem, vmem_limit_bytes=VMEM_LIMIT)


def _dot(a, b):
    return jnp.dot(a.astype(BF16), b.astype(BF16), preferred_element_type=F32)


def _dot_nt(a, b):
    return lax.dot_general(a.astype(BF16), b.astype(BF16), (((1,), (1,)), ((), ())),
                           preferred_element_type=F32)


def _silu(x):
    h = 0.5 * x
    return h + h * jnp.tanh(h)


def _softplus(x):
    return jnp.maximum(x, 0.0) + jnp.log1p(jnp.exp(-jnp.abs(x)))


def _gelu_tanh(x):
    c = math.sqrt(2.0 / math.pi)
    return 0.5 * x * (1.0 + jnp.tanh(c * (x + 0.044715 * (x * x * x))))


def _rms_mod(x, gain, shift, scale):
    ms = jnp.mean(x * x, axis=-1, keepdims=True)
    return (x * lax.rsqrt(ms + EPS) * gain) * (1.0 + scale) + shift


def _split3(v):
    p1 = v.astype(BF16)
    r1 = v - p1.astype(F32)
    p2 = r1.astype(BF16)
    r2 = r1 - p2.astype(F32)
    return p1, p2, r2.astype(BF16)


def _tri_cumsum(tri, v):
    p1, p2, p3 = _split3(v)
    d = lambda p: jnp.dot(tri, p, preferred_element_type=F32)
    return d(p1) + d(p2) + d(p3)


def _mod_spec(d, n_lat_tiles):
    return pl.BlockSpec((BB, None, 8, d), lambda i, t: (i, t // n_lat_tiles, 0, 0))


def _tile_spec(width):
    return pl.BlockSpec((BB, TM, width), lambda i, t: (i, t, 0))


def _const_spec(shape):
    return pl.BlockSpec(shape, lambda i, t: (0,) * len(shape))


def _stream0_specs(d, n_lat_tiles):
    return [pl.BlockSpec((BB, TM, d), lambda i, t: (i, jnp.minimum(t, n_lat_tiles - 1), 0)),
            pl.BlockSpec((BB, TM, d), lambda i, t: (i, 0, 0))]


def _stream0_rows(x_ref, c_ref, bi, n_lat_tiles):
    return jnp.where(pl.program_id(1) == n_lat_tiles, c_ref[bi], x_ref[bi])


def _col_chunks(width, step):
    return [(lo, min(lo + step, width)) for lo in range(0, width, step)]


def _tri_matrix(n, rev):
    r = lax.broadcasted_iota(jnp.int32, (n, n), 0)
    c = lax.broadcasted_iota(jnp.int32, (n, n), 1)
    return jnp.where((c >= r) if rev else (c <= r), 1.0, 0.0).astype(BF16)


def _mod_kernel(s_ref, w_ref, b_ref, o_ref):
    s = _silu(s_ref[...])
    o_ref[0] = _dot(s, w_ref[0]) + b_ref[0]


def _modulation(cond, w_mod, b_mod):
    depth, d, n = w_mod.shape
    rows = cond.shape[0]
    tn = 1536
    return pl.pallas_call(
        _mod_kernel,
        grid=(depth, n // tn),
        in_specs=[pl.BlockSpec((rows, d), lambda l, j: (0, 0)),
                  pl.BlockSpec((1, d, tn), lambda l, j: (l, 0, j)),
                  pl.BlockSpec((1, 1, tn), lambda l, j: (l, 0, j))],
        out_specs=pl.BlockSpec((1, rows, tn), lambda l, j: (l, 0, j)),
        out_shape=jax.ShapeDtypeStruct((depth, rows, n), F32),
        compiler_params=_cparams(("parallel", "parallel")),
        name="modulation",
    )(cond, w_mod, b_mod.reshape(depth, 1, n))


def _inproj0_kernel(x_ref, c_ref, m_ref, gain_ref, w_ref, qg_ref, kg_ref, qkv_ref, hraw_ref, *, n_lat_tiles):
    u = jnp.concatenate(
        [_rms_mod(_stream0_rows(x_ref, c_ref, bi, n_lat_tiles), gain_ref[...], m_ref[bi, 0:1, :],
                  m_ref[bi, 1:2, :]).astype(BF16) for bi in range(BB)], axis=0)
    lo = lax.broadcasted_iota(jnp.int32, (1, LANES), 1) < NA_HEAD_DIM
    na_w = qg_ref.shape[1]

    def head_norm(y, g_ref):
        outs = []
        for c in range(na_w // LANES):
            yc = y[:, c * LANES:(c + 1) * LANES]
            sq = yc * yc
            s_lo = jnp.sum(jnp.where(lo, sq, 0.0), axis=-1, keepdims=True)
            s_hi = jnp.sum(jnp.where(lo, 0.0, sq), axis=-1, keepdims=True)
            inv = jnp.where(lo, lax.rsqrt(s_lo / NA_HEAD_DIM + EPS), lax.rsqrt(s_hi / NA_HEAD_DIM + EPS))
            outs.append(yc * inv * g_ref[:, c * LANES:(c + 1) * LANES])
        return jnp.concatenate(outs, axis=-1)

    def proj(lo_col, width):
        return jnp.dot(u, w_ref[:, lo_col:lo_col + width], preferred_element_type=F32)

    def store(ref, lo_col, y):
        for bi in range(BB):
            ref[bi, :, lo_col:lo_col + y.shape[1]] = y[bi * TM:(bi + 1) * TM].astype(ref.dtype)

    store(qkv_ref, 0, head_norm(proj(0, na_w), qg_ref))
    store(qkv_ref, na_w, head_norm(proj(na_w, na_w), kg_ref))
    store(qkv_ref, 2 * na_w, proj(2 * na_w, na_w))
    for lo_col, hi_col in _col_chunks(hraw_ref.shape[2], 512):
        store(hraw_ref, lo_col, proj(3 * na_w + lo_col, hi_col - lo_col))


def _inproj0(x, ctx, mod, gain, w, qg, kg):
    b, t_lat, d = x.shape
    n = w.shape[1]
    na_w = qg.shape[1]
    n_hg = n - 3 * na_w
    n_lat_tiles = t_lat // TM
    tt = t_lat + ctx.shape[1]
    return pl.pallas_call(
        functools.partial(_inproj0_kernel, n_lat_tiles=n_lat_tiles),
        grid=(b // BB, n_lat_tiles + 1),
        in_specs=_stream0_specs(d, n_lat_tiles) + [
            _mod_spec(d, n_lat_tiles), _const_spec((1, d)), _const_spec((d, n)),
            _const_spec((1, na_w)), _const_spec((1, na_w))],
        out_specs=[_tile_spec(3 * na_w), _tile_spec(n_hg)],
        out_shape=[jax.ShapeDtypeStruct((b, tt, 3 * na_w), BF16),
                   jax.ShapeDtypeStruct((b, tt, n_hg), F32)],
        compiler_params=_cparams(("parallel", "parallel")),
        name="inproj0",
    )(x, ctx, mod, gain, w, qg, kg)


def _na_kernel(q_ref, k_ref, v_ref, bias_ref, o_ref, *, t_lat, rows):
    tt = q_ref.shape[1]
    lane = lax.broadcasted_iota(jnp.int32, (1, LANES), 1)
    head_mask = [lane < NA_HEAD_DIM, lane >= NA_HEAD_DIM]
    kc = k_ref[0, t_lat:tt, :]
    vc = v_ref[0, t_lat:tt, :]
    _, ku, u0s, _, case_of_block = _na_block_plan(rows)
    n_loc = ku * GRID_W
    n_q = NA_ROW_BLOCK * GRID_W

    qc = q_ref[0, t_lat:tt, :]
    n_c = tt - t_lat
    s = _dot_nt(jnp.concatenate([jnp.where(hm, qc, jnp.zeros_like(qc)) for hm in head_mask], axis=0), kc)
    p = jnp.exp(s - jnp.max(s, axis=-1, keepdims=True))
    o_both = _dot(p, vc) / jnp.sum(p, axis=-1, keepdims=True)
    oc = None
    for hi, hm in enumerate(head_mask):
        o = o_both[hi * n_c:(hi + 1) * n_c]
        oc = o if oc is None else jnp.where(hm, o, oc)
    o_ref[0, t_lat:tt, :] = oc.astype(o_ref.dtype)

    n_blocks = rows // NA_ROW_BLOCK
    group = 2 if n_blocks % 2 == 0 else 1

    def blocks_body(it, carry):
        loaded, scores = [], []
        for j in range(group):
            i = it * group + j
            u0 = u0s[0]
            case = case_of_block[0]
            for bi in range(1, len(u0s)):
                u0 = jnp.where(i == bi, u0s[bi], u0)
                case = jnp.where(i == bi, case_of_block[bi], case)
            q_rows = pl.ds(pl.multiple_of(i * n_q, n_q), n_q)
            q_i = q_ref[0, q_rows, :]
            start = pl.multiple_of(u0 * GRID_W, GRID_W)
            k_loc = k_ref[0, pl.ds(start, n_loc), :]
            loaded.append((q_rows, v_ref[0, pl.ds(start, n_loc), :]))
            q_both = jnp.concatenate([jnp.where(hm, q_i, jnp.zeros_like(q_i)) for hm in head_mask], axis=0)
            s_loc = _dot_nt(q_both, k_loc)
            s_ctx = _dot_nt(q_both, kc)
            for hi in range(len(head_mask)):
                hr = slice(hi * n_q, (hi + 1) * n_q)
                scores.append((s_loc[hr] + bias_ref[hi, case], s_ctx[hr]))
        probs = []
        for s_loc, s_ctx in scores:
            m = jnp.maximum(jnp.max(s_loc, axis=-1, keepdims=True), jnp.max(s_ctx, axis=-1, keepdims=True))
            p_loc = jnp.exp(s_loc - m)
            p_ctx = jnp.exp(s_ctx - m)
            den = jnp.sum(p_loc, axis=-1, keepdims=True) + jnp.sum(p_ctx, axis=-1, keepdims=True)
            probs.append((p_loc.astype(BF16), p_ctx.astype(BF16), den))
        n_hd = len(head_mask)
        for j, (q_rows, v_loc) in enumerate(loaded):
            mine = probs[j * n_hd:(j + 1) * n_hd]
            o_both = (jnp.dot(jnp.concatenate([p[0] for p in mine], axis=0), v_loc, preferred_element_type=F32)
                      + jnp.dot(jnp.concatenate([p[1] for p in mine], axis=0), vc, preferred_element_type=F32))
            out = None
            for hi, hm in enumerate(head_mask):
                o = o_both[hi * n_q:(hi + 1) * n_q] / mine[hi][2]
                out = o if out is None else jnp.where(hm, o, out)
            o_ref[0, q_rows, :] = out.astype(o_ref.dtype)
        return carry

    lax.fori_loop(0, n_blocks // group, blocks_body, 0)


def _na_block_plan(rows):
    kr = min(WIN_ROWS, rows)
    ku = min(kr + NA_ROW_BLOCK - 1, rows)
    u0s, patterns, case_of_block = [], [], []
    for i in range(rows // NA_ROW_BLOCK):
        u0 = int(np.clip(NA_ROW_BLOCK * i - kr // 2, 0, rows - ku))
        rel = []
        for a in range(NA_ROW_BLOCK):
            r = NA_ROW_BLOCK * i + a
            r0 = int(np.clip(r - kr // 2, 0, rows - kr))
            assert u0 <= r0 and r0 + kr <= u0 + ku
            rel.append((r0 - u0, u0 - r))
        u0s.append(u0)
        if tuple(rel) not in patterns:
            patterns.append(tuple(rel))
        case_of_block.append(patterns.index(tuple(rel)))
    return kr, ku, u0s, patterns, case_of_block


def _na_bias_table(rpb, rows):
    kr, ku, _, patterns, _ = _na_block_plan(rows)
    q = np.arange(GRID_W)
    kcol = np.arange(GRID_W)
    ws = np.clip(q - WIN_COLS // 2, 0, GRID_W - WIN_COLS)
    in_win = (kcol[None, :] >= ws[:, None]) & (kcol[None, :] < ws[:, None] + WIN_COLS)
    dc = np.clip(kcol[None, :] - q[:, None] + WIN_COLS - 1, 0, 2 * WIN_COLS - 2)
    u = np.arange(ku)
    n_dr = 2 * WIN_ROWS - 1
    dr = np.full((len(patterns), NA_ROW_BLOCK, ku), n_dr, np.int32)
    for c, rel in enumerate(patterns):
        for a, (r0_rel, u0_minus_r) in enumerate(rel):
            row_ok = (u >= r0_rel) & (u < r0_rel + kr)
            dr[c, a] = np.where(row_ok, u0_minus_r + u + WIN_ROWS - 1, n_dr)
    h = rpb.shape[0]
    tiles = jnp.where(in_win[None, None], rpb.astype(F32)[:, :, dc], MASK_NEG)
    tiles = jnp.concatenate([tiles, jnp.full((h, 1, GRID_W, GRID_W), MASK_NEG, F32)], axis=1)
    tbl = jnp.take(tiles, jnp.asarray(dr.reshape(-1)), axis=1)
    tbl = tbl.reshape(h, len(patterns), NA_ROW_BLOCK, ku, GRID_W, GRID_W)
    return tbl.transpose(0, 1, 2, 4, 3, 5).reshape(h, len(patterns), NA_ROW_BLOCK * GRID_W, ku * GRID_W)


def _na_attention(qkv, bias, t_lat):
    b, tt, w3 = qkv.shape
    na_w = w3 // 3
    ncol = na_w // LANES
    rows = t_lat // GRID_W
    bias_block = (2,) + bias.shape[1:]
    kern = functools.partial(_na_kernel, t_lat=t_lat, rows=rows)
    return pl.pallas_call(
        kern,
        grid=(b, ncol),
        in_specs=[pl.BlockSpec((1, tt, LANES), lambda i, p: (i, 0, p)),
                  pl.BlockSpec((1, tt, LANES), lambda i, p: (i, 0, ncol + p)),
                  pl.BlockSpec((1, tt, LANES), lambda i, p: (i, 0, 2 * ncol + p)),
                  pl.BlockSpec(bias_block, lambda i, p: (p, 0, 0, 0))],
        out_specs=pl.BlockSpec((1, tt, LANES), lambda i, p: (i, 0, p)),
        out_shape=jax.ShapeDtypeStruct((b, tt, na_w), BF16),
        compiler_params=_cparams(("parallel", "parallel")),
        name="na_attention",
    )(qkv, qkv, qkv, bias)


def _scan_block(i, nt, rev):
    if rev:
        return jnp.where(i == 0, nt - 1, nt - 1 - i)
    return jnp.where(i == 0, nt - 1, i - 1)


class _HgrnDir:
    def __init__(self, rev, q_ref, f_ref, v_ref, lb_ref, o_ref, st_ref):
        self.rev, self.q_ref, self.f_ref, self.v_ref, self.o_ref, self.st_ref = rev, q_ref, f_ref, v_ref, o_ref, st_ref
        self.n_heads = st_ref.shape[0]
        self.lb = lb_ref[0:1, :]
        self.one_m_lb = lb_ref[1:2, :]
        self.n_chunks = q_ref.shape[1] // CHUNK
        self.order = list(range(self.n_chunks - 1, -1, -1)) if rev else list(range(self.n_chunks))
        self.heads = [slice(h * HG_HEAD_DIM, (h + 1) * HG_HEAD_DIM) for h in range(self.n_heads)]
        self.states = [st_ref[h] for h in range(self.n_heads)]
        self.gated = {}
        self.tri = _tri_matrix(CHUNK, rev)
        row = lax.broadcasted_iota(jnp.int32, (CHUNK, CHUNK), 0)
        col = lax.broadcasted_iota(jnp.int32, (CHUNK, CHUNK), 1)
        self.causal = (col >= row) if rev else (col <= row)

    def gate(self, cc):
        lb, one_m_lb, q_ref, f_ref = self.lb, self.one_m_lb, self.q_ref, self.f_ref
        rs = slice(cc * CHUNK, (cc + 1) * CHUNK)
        x = f_ref[0, rs, :]
        e = jnp.exp(-jnp.abs(x))
        r = 1.0 / (1.0 + e)
        x_pos = x >= 0.0
        f = jnp.where(x_pos, 1.0 + lb * e, lb + e) * r
        log_f = jnp.where(f > 0.0, jnp.log(f), x)
        k_all = one_m_lb * jnp.where(x_pos, e, 1.0) * r
        return _silu(q_ref[0, rs, :]), k_all, log_f

    def decays(self, cc, gates):
        rev = self.rev
        q_all, k_all, log_f = gates
        n_sub = CHUNK // SUB
        zero_row = jnp.zeros((1, log_f.shape[1]), F32)
        scan_blocks = list(range(n_sub - 1, -1, -1)) if rev else list(range(n_sub))
        rng = None
        g_all = _tri_cumsum(self.tri, log_f)
        c = [g_all[j * SUB:j * SUB + 1, :] if rev else g_all[(j + 1) * SUB - 1:(j + 1) * SUB, :]
             for j in range(n_sub)]
        c_prev = [zero_row] * n_sub
        for before, after in zip(scan_blocks[:-1], scan_blocks[1:]):
            c_prev[after] = c[before]
        for j in range(n_sub):
            d = c_prev[j] - c[j]
            rng = d if rng is None else jnp.maximum(rng, d)
        self.gated[cc] = (q_all, k_all, g_all, c, c_prev)
        return rng

    def slow_diag(self):
        rev = self.rev
        row = lax.broadcasted_iota(jnp.int32, (CHUNK, CHUNK), 0)
        col = lax.broadcasted_iota(jnp.int32, (CHUNK, CHUNK), 1)
        rowv = lax.broadcasted_iota(jnp.int32, (CHUNK, 1), 0)
        pos = rowv % SUB
        accs = []
        for cc in self.order:
            q_all, k_all, g_all = self.gated[cc][:3]
            for hs in self.heads:
                q, k, g = q_all[:, hs], k_all[:, hs], g_all[:, hs]
                acc = jnp.zeros((CHUNK, CHUNK), F32)
                for dlt in range(SUB):
                    shift = (CHUNK - dlt) % CHUNK if rev else dlt
                    k_d = pltpu.roll(k, shift, 0) if shift else k
                    g_d = pltpu.roll(g, shift, 0) if shift else g
                    valid = (pos + dlt <= SUB - 1) if rev else (pos >= dlt)
                    e = jnp.exp(jnp.where(valid, g - g_d, 0.0))
                    val = jnp.sum(q * k_d * e, axis=-1, keepdims=True)
                    partner = (row + dlt) if rev else (row - dlt)
                    acc = acc + jnp.where((col == partner) & valid, val, 0.0)
                accs.append(acc)
        return jnp.stack(accs, axis=0)

    def prepare(self, ci, safe):
        rev = self.rev
        n_sub = CHUNK // SUB
        cc = self.order[ci]
        q_all, k_all, g_all, c, c_prev = self.gated[cc]
        width = g_all.shape[1]
        scan_blocks = list(range(n_sub - 1, -1, -1)) if rev else list(range(n_sub))
        c_final = c[scan_blocks[-1]]
        blk = lambda x, j: x[j * SUB:(j + 1) * SUB]

        def per_block(fn):
            return jnp.concatenate([fn(j) for j in range(n_sub)], axis=0)

        q_b = per_block(lambda j: blk(q_all, j) * jnp.exp(blk(g_all, j) - c_prev[j]))
        k_end = per_block(lambda j: blk(k_all, j) * jnp.exp(c[j] - blk(g_all, j)))
        k_comb = []
        for sb in range(n_sub):
            def scaled(j, sb=sb):
                if j == sb:
                    return blk(k_end, j) * jnp.where(safe, jnp.exp(jnp.minimum(c_prev[sb] - c[sb], SAFE_RANGE)), 0.0)
                if scan_blocks.index(j) < scan_blocks.index(sb):
                    return blk(k_end, j) * jnp.exp(c_prev[sb] - c[j])
                return jnp.zeros((SUB, width), F32)
            k_comb.append(per_block(scaled))
        return dict(
            rows=slice(cc * CHUNK, (cc + 1) * CHUNK), q_b=q_b, k_comb=k_comb,
            q_g=per_block(lambda j: blk(q_b, j) * jnp.exp(c_prev[j])),
            k_last=per_block(lambda j: blk(k_end, j) * jnp.exp(c_final - c[j])),
            e_last=jnp.exp(c_final), v=self.v_ref[0, slice(cc * CHUNK, (cc + 1) * CHUNK), :], outs=[])

    def scores(self, h, p, diag_block):
        n_sub = CHUNK // SUB
        hs = self.heads[h]
        blocks = [_dot_nt(p["q_b"][sb * SUB:(sb + 1) * SUB, hs], p["k_comb"][sb][:, hs]) for sb in range(n_sub)]
        return jnp.where(self.causal, jnp.concatenate(blocks, axis=0), 0.0) + diag_block

    def values(self, h, p, a):
        hs = self.heads[h]
        st = self.states[h]
        p["outs"].append(_dot(a, p["v"][:, hs]) + _dot_nt(p["q_g"][:, hs], st))
        self.states[h] = st * p["e_last"][:, hs] + lax.dot_general(
            p["v"][:, hs].astype(BF16), p["k_last"][:, hs].astype(BF16), (((0,), (0,)), ((), ())),
            preferred_element_type=F32)

    def store(self, p):
        self.o_ref[0, p["rows"], :] = jnp.concatenate(p["outs"], axis=-1).astype(self.o_ref.dtype)

    def finish(self):
        for h in range(self.n_heads):
            self.st_ref[h] = self.states[h]


def _hgrn_kernel(qf_ref, ff_ref, vf_ref, lbf_ref, qb_ref, fb_ref, vb_ref, lbb_ref, of_ref, ob_ref, sf_ref, sb_ref):
    @pl.when(pl.program_id(1) == 0)
    def _():
        sf_ref[...] = jnp.zeros_like(sf_ref)
        sb_ref[...] = jnp.zeros_like(sb_ref)

    dirs = [_HgrnDir(False, qf_ref, ff_ref, vf_ref, lbf_ref, of_ref, sf_ref),
            _HgrnDir(True, qb_ref, fb_ref, vb_ref, lbb_ref, ob_ref, sb_ref)]
    n_chunks, n_heads = dirs[0].n_chunks, dirs[0].n_heads
    rng = None
    stage1 = [(d, d.order[ci]) for ci in range(n_chunks) for d in dirs]
    gates = [d.gate(cc) for d, cc in stage1]
    for (d, cc), gt in zip(stage1, gates):
        r = d.decays(cc, gt)
        rng = r if rng is None else jnp.maximum(rng, r)
    safe = jnp.max(rng) <= SAFE_RANGE
    per_dir = n_chunks * n_heads
    diag = lax.cond(safe, lambda: jnp.zeros((2 * per_dir, CHUNK, CHUNK), F32),
                    lambda: jnp.concatenate([d.slow_diag() for d in dirs], axis=0))
    for ci in range(n_chunks):
        preps = [d.prepare(ci, safe) for d in dirs]
        chains = [(h, di, d) for h in range(n_heads) for di, d in enumerate(dirs)]
        att = [d.scores(h, preps[di], diag[di * per_dir + ci * n_heads + h]) for h, di, d in chains]
        for (h, di, d), a in zip(chains, att):
            d.values(h, preps[di], a)
        for di, d in enumerate(dirs):
            d.store(preps[di])
    for d in dirs:
        d.finish()


def _hgrn_scan(hraw, lb_rows_f, lb_rows_b):
    b, tt, n = hraw.shape
    w = lb_rows_f.shape[1]
    nt = tt // TM
    n_heads = w // HG_HEAD_DIM

    def dir_specs(rev):
        blk = lambda i: _scan_block(i, nt, rev)
        f_col = 2 if rev else 1
        return [pl.BlockSpec((1, TM, w), lambda bi, i: (bi, blk(i), 0)),
                pl.BlockSpec((1, TM, w), lambda bi, i: (bi, blk(i), f_col)),
                pl.BlockSpec((1, TM, w), lambda bi, i: (bi, blk(i), 3)),
                pl.BlockSpec((8, w), lambda bi, i: (0, 0))]

    out_spec = lambda rev: pl.BlockSpec((1, TM, w), lambda bi, i: (bi, _scan_block(i, nt, rev), 0))
    return pl.pallas_call(
        _hgrn_kernel,
        grid=(b, nt),
        in_specs=dir_specs(False) + dir_specs(True),
        out_specs=[out_spec(False), out_spec(True)],
        out_shape=[jax.ShapeDtypeStruct((b, tt, w), BF16)] * 2,
        scratch_shapes=[pltpu.VMEM((n_heads, HG_HEAD_DIM, HG_HEAD_DIM), F32)] * 2,
        compiler_params=_cparams(("parallel", "arbitrary")),
        name="hgrn_scan",
    )(hraw, hraw, hraw, lb_rows_f, hraw, hraw, hraw, lb_rows_b)


def _outproj0_kernel(na_ref, of_ref, ob_ref, gate_ref, x_ref, c_ref, m_ref, gain_ref, w_ref, o_ref, *, n_lat_tiles):
    na_w = na_ref.shape[2]
    for bi in range(BB):
        acc = jnp.dot(na_ref[bi], w_ref[0:na_w, :], preferred_element_type=F32)
        o = of_ref[bi].astype(F32) + ob_ref[bi].astype(F32)
        gate = _silu(gate_ref[bi])
        ys = []
        for h in range(o.shape[1] // HG_HEAD_DIM):
            hs = slice(h * HG_HEAD_DIM, (h + 1) * HG_HEAD_DIM)
            oh = o[:, hs]
            ms = jnp.mean(oh * oh, axis=-1, keepdims=True)
            ys.append(oh * lax.rsqrt(ms + EPS) * gain_ref[:, hs] * gate[:, hs])
        acc = acc + jnp.dot(jnp.concatenate(ys, axis=-1).astype(BF16), w_ref[na_w:, :], preferred_element_type=F32)
        o_ref[bi] = _stream0_rows(x_ref, c_ref, bi, n_lat_tiles) + m_ref[bi, 2:3, :] * acc


def _outproj0(o_na, o_f, o_b, hraw, x, ctx, mod, gain, w):
    b, t_lat, d = x.shape
    tt = t_lat + ctx.shape[1]
    na_w = o_na.shape[2]
    hw = o_f.shape[2]
    n_lat_tiles = t_lat // TM
    return pl.pallas_call(
        functools.partial(_outproj0_kernel, n_lat_tiles=n_lat_tiles),
        grid=(b // BB, n_lat_tiles + 1),
        in_specs=[_tile_spec(na_w), _tile_spec(hw), _tile_spec(hw),
                  pl.BlockSpec((BB, TM, hw), lambda i, t: (i, t, 4))] + _stream0_specs(d, n_lat_tiles) + [
            _mod_spec(d, n_lat_tiles), _const_spec((1, hw)), _const_spec(w.shape)],
        out_specs=_tile_spec(d),
        out_shape=jax.ShapeDtypeStruct((b, tt, d), F32),
        compiler_params=_cparams(("parallel", "parallel")),
        name="outproj0",
    )(o_na, o_f, o_b, hraw, x, ctx, mod, gain, w)


HALO = 8
TM_EXT = TM + 2 * HALO


def _halo_specs(width, tt):
    r8 = TM // HALO
    last = tt // HALO - 1
    return [pl.BlockSpec((BB, HALO, width), lambda i, t: (i, jnp.maximum(t * r8 - 1, 0), 0)),
            pl.BlockSpec((BB, HALO, width), lambda i, t: (i, jnp.minimum((t + 1) * r8, last), 0))]


def _modulated_with_halo(h_ref, hp_ref, hn_ref, gain, m_ref, shift_row, scale_row):
    us, exts = [], []
    for bi in range(BB):
        shift = m_ref[bi, shift_row:shift_row + 1, :]
        scale = m_ref[bi, scale_row:scale_row + 1, :]
        u = _rms_mod(h_ref[bi], gain, shift, scale)
        us.append(u.astype(BF16))
        exts += [_rms_mod(hp_ref[bi], gain, shift, scale), u, _rms_mod(hn_ref[bi], gain, shift, scale)]
    return jnp.concatenate(us, axis=0), jnp.concatenate(exts, axis=0).astype(BF16)


def _conv3_ext(a_ext, cw, cb, n_lat_tiles):
    n = a_ext.shape[0] - 2 * HALO
    t = pl.program_id(1)
    has_prev = jnp.logical_and(t != 0, t != n_lat_tiles)
    has_next = jnp.logical_and(t != n_lat_tiles - 1, t != n_lat_tiles)
    a = a_ext[HALO:HALO + n]
    prev_row = jnp.where(has_prev, a_ext[HALO - 1:HALO], 0.0)
    next_row = jnp.where(has_next, a_ext[HALO + n:HALO + n + 1], 0.0)
    row = lax.broadcasted_iota(jnp.int32, (n, 1), 0)
    up = jnp.where(row == 0, prev_row, pltpu.roll(a, 1, 0))
    dn = jnp.where(row == n - 1, next_row, pltpu.roll(a, n - 1, 0))
    return cw[0:1, :] * up + cw[1:2, :] * a + cw[2:3, :] * dn + cb


def _ffn_up_kernel(h_ref, hp_ref, hn_ref, m_ref, gain_ref, w_ref, cw_ref, cb_ref, mid_ref, *, n_lat_tiles):
    u, u_ext = _modulated_with_halo(h_ref, hp_ref, hn_ref, gain_ref[...], m_ref, 3, 4)
    dff = mid_ref.shape[2]
    for lo, hi in _col_chunks(dff, 768):
        a_ext = jnp.dot(u_ext, w_ref[:, lo:hi], preferred_element_type=F32)
        v = jnp.dot(u, w_ref[:, dff + lo:dff + hi], preferred_element_type=F32)
        for bi in range(BB):
            c = _conv3_ext(a_ext[bi * TM_EXT:(bi + 1) * TM_EXT], cw_ref[:, lo:hi], cb_ref[:, lo:hi], n_lat_tiles)
            mid_ref[bi, :, lo:hi] = (_gelu_tanh(c) * v[bi * TM:(bi + 1) * TM]).astype(mid_ref.dtype)


def _ffn_up(hh, mod, gain, w, conv_w, conv_b, n_tiles, n_lat_tiles):
    b, tt_in, d = hh.shape
    dff = w.shape[1] // 2
    return pl.pallas_call(
        functools.partial(_ffn_up_kernel, n_lat_tiles=n_lat_tiles),
        grid=(b // BB, n_tiles),
        in_specs=[_tile_spec(d)] + _halo_specs(d, tt_in) + [
            _mod_spec(d, n_lat_tiles), _const_spec((1, d)), _const_spec(w.shape),
            _const_spec((3, dff)), _const_spec((1, dff))],
        out_specs=_tile_spec(dff),
        out_shape=jax.ShapeDtypeStruct((b, n_tiles * TM, dff), BF16),
        compiler_params=_cparams(("parallel", "parallel")),
        name="ffn_up",
    )(hh, hh, hh, mod, gain, w, conv_w, conv_b)


def _ffn_down_kernel(mid_ref, w_ref, h_ref, m_ref, o_ref):
    y = jnp.dot(jnp.concatenate([mid_ref[bi] for bi in range(BB)], axis=0), w_ref[...],
                preferred_element_type=F32)
    for bi in range(BB):
        o_ref[bi] = h_ref[bi] + m_ref[bi, 5:6, :] * y[bi * TM:(bi + 1) * TM]


def _ffn_down(mid, w, hh, mod, n_lat_tiles):
    b, tt, dff = mid.shape
    d = hh.shape[2]
    return pl.pallas_call(
        _ffn_down_kernel,
        grid=(b // BB, tt // TM),
        in_specs=[_tile_spec(dff), _const_spec(w.shape), _tile_spec(d), _mod_spec(d, n_lat_tiles)],
        out_specs=_tile_spec(d),
        out_shape=jax.ShapeDtypeStruct((b, tt, d), F32),
        compiler_params=_cparams(("parallel", "parallel")),
        name="ffn_down",
    )(mid, w, hh, mod)


def _inproj1_kernel(h_ref, hp_ref, hn_ref, m_ref, gain_ref, w_ref, cw_ref, cb_ref, dtb_ref, a_ref,
                    z_ref, xs_ref, bc_ref, dtp_ref, *, n_lat_tiles):
    u, u_ext = _modulated_with_halo(h_ref, hp_ref, hn_ref, gain_ref[...], m_ref, 0, 1)
    nz = z_ref.shape[2]
    nxs = xs_ref.shape[2]
    nx = nxs + bc_ref.shape[2]
    step = 512
    for lo, hi in _col_chunks(nz, step):
        z = jnp.dot(u, w_ref[:, lo:hi], preferred_element_type=F32)
        for bi in range(BB):
            z_ref[bi, :, lo:hi] = z[bi * TM:(bi + 1) * TM].astype(z_ref.dtype)
    for lo, hi in _col_chunks(nx, step):
        xbc_ext = jnp.dot(u_ext, w_ref[:, nz + lo:nz + hi], preferred_element_type=F32)
        for bi in range(BB):
            c = _silu(_conv3_ext(xbc_ext[bi * TM_EXT:(bi + 1) * TM_EXT], cw_ref[:, lo:hi], cb_ref[:, lo:hi],
                                 n_lat_tiles))
            if hi <= nxs:
                xs_ref[bi, :, lo:hi] = c.astype(xs_ref.dtype)
            else:
                bc_ref[bi, :, lo - nxs:hi - nxs] = c.astype(bc_ref.dtype)
    raw = jnp.dot(u, w_ref[:, nz + nx:], preferred_element_type=F32)
    dt = _softplus(raw + dtb_ref[...])
    la = dt * a_ref[...]
    for bi in range(BB):
        dtp_ref[bi, :, 0:2 * LANES] = dt[bi * TM:(bi + 1) * TM]
        dtp_ref[bi, :, 2 * LANES:4 * LANES] = la[bi * TM:(bi + 1) * TM]


def _inproj1(hh, mod, gain, w, conv_w, conv_b, dt_bias, a_neg, nz, nxs):
    b, tt, d = hh.shape
    nt = tt // TM
    n_lat_tiles = nt - 1
    nx = conv_w.shape[1]
    assert nxs % 512 == 0
    tile = _tile_spec
    return pl.pallas_call(
        functools.partial(_inproj1_kernel, n_lat_tiles=n_lat_tiles),
        grid=(b // BB, nt),
        in_specs=[tile(d)] + _halo_specs(d, tt) + [
            _mod_spec(d, n_lat_tiles), _const_spec((1, d)), _const_spec(w.shape),
            _const_spec((3, nx)), _const_spec((1, nx)),
            _const_spec((1, 2 * LANES)), _const_spec((1, 2 * LANES))],
        out_specs=[tile(nz), tile(nxs), tile(nx - nxs), tile(4 * LANES)],
        out_shape=[jax.ShapeDtypeStruct((b, tt, nz), BF16),
                   jax.ShapeDtypeStruct((b, tt, nxs), BF16),
                   jax.ShapeDtypeStruct((b, tt, nx - nxs), BF16),
                   jax.ShapeDtypeStruct((b, tt, 4 * LANES), F32)],
        compiler_params=_cparams(("parallel", "parallel")),
        name="inproj1",
    )(hh, hh, hh, mod, gain, w, conv_w, conv_b, dt_bias, a_neg)


class _SsdDir:
    def __init__(self, rev, x_ref, bc_ref, dt_ref, la_ref, e_ref, o_ref, s_ref, skip_ref=None):
        self.rev, self.x_ref, self.bc_ref, self.dt_ref, self.la_ref = rev, x_ref, bc_ref, dt_ref, la_ref
        self.e_ref, self.o_ref, self.s_ref, self.skip_ref = e_ref, o_ref, s_ref, skip_ref
        self.n_groups = s_ref.shape[0]
        self.gw = s_ref.shape[2]
        self.n_heads = x_ref.shape[2] // SSD_HEAD_DIM
        self.n_chunks = x_ref.shape[1] // CHUNK
        self.states = [s_ref[g] for g in range(self.n_groups)]
        self.tri = _tri_matrix(CHUNK, rev)
        rowc = lax.broadcasted_iota(jnp.int32, (CHUNK, self.gw), 0)
        pos = lax.broadcasted_iota(jnp.int32, (CHUNK, self.gw), 1) % SSD_HEAD_DIM
        self.on_diag = pos == rowc
        self.causal = (pos >= rowc) if rev else (pos <= rowc)

    def chunk(self, k):
        return self.n_chunks - 1 - k if self.rev else k

    def _stacked(self, v):
        lane = lax.broadcasted_iota(jnp.int32, (1, LANES), 1)
        p1, p2, p3 = _split3(v)
        return jnp.where(lane < self.n_heads, p1, jnp.where(lane < 2 * self.n_heads, p2,
                                                            jnp.where(lane < 3 * self.n_heads, p3, jnp.zeros_like(p1))))

    def scalars(self, cc):
        rs = slice(cc * CHUNK, (cc + 1) * CHUNK)
        return (self._stacked(_tri_cumsum(self.tri, self.la_ref[0, rs, :])), self._stacked(self.dt_ref[0, rs, :]))

    def expand(self, g, scalars):
        cum_st, dt_st = scalars
        gs = slice(g * self.gw, (g + 1) * self.gw)
        return (jnp.dot(cum_st, self.e_ref[:, gs], preferred_element_type=F32),
                jnp.dot(dt_st, self.e_ref[:, gs], preferred_element_type=F32))

    def prep(self, cc, g, expanded):
        cum, dt = expanded
        rs = slice(cc * CHUNK, (cc + 1) * CHUNK)
        gs = slice(g * self.gw, (g + 1) * self.gw)
        last = 0 if self.rev else CHUNK - 1
        cum_row = jnp.sum(jnp.where(self.on_diag, cum, 0.0), axis=0, keepdims=True)
        cum_last = cum[last:last + 1, :]
        x = self.x_ref[0, rs, gs].astype(F32)
        x_dt = x * dt
        return dict(
            skip=None if self.skip_ref is None else x * self.skip_ref[:, gs],
            xw=(x_dt * jnp.exp(cum_last - cum)).astype(BF16),
            x_dt=x_dt.astype(BF16),
            decay=jnp.exp(jnp.where(self.causal, cum - cum_row, MASK_NEG)),
            e_cum=jnp.exp(cum),
            e_last=jnp.exp(cum_last))

    def first_matmuls(self, cc, g, p):
        rs = slice(cc * CHUNK, (cc + 1) * CHUNK)
        b_g = self.bc_ref[0, rs, g * SSD_STATE:(g + 1) * SSD_STATE]
        c_g = self.bc_ref[0, rs, (self.n_groups + g) * SSD_STATE:(self.n_groups + g + 1) * SSD_STATE]
        b_rep = jnp.concatenate([b_g] * (self.gw // CHUNK), axis=0)
        return dict(cb=_dot_nt(c_g, b_rep), read=_dot(c_g, self.states[g]),
                    update=lax.dot_general(b_g, p["xw"], (((0,), (0,)), ((), ())), preferred_element_type=F32))

    def intra(self, g, p, first):
        blk = 256
        blk_head = lax.broadcasted_iota(jnp.int32, (1, blk), 1) // SSD_HEAD_DIM
        m_g = (p["decay"] * first["cb"]).astype(BF16)
        y_parts = []
        for j in range(self.gw // blk):
            x4 = p["x_dt"][:, j * blk:(j + 1) * blk]
            x_bd = jnp.concatenate(
                [jnp.where(blk_head == hh, x4, jnp.zeros_like(x4)) for hh in range(blk // SSD_HEAD_DIM)], axis=0)
            y_parts.append(jnp.dot(m_g[:, j * blk:(j + 1) * blk], x_bd, preferred_element_type=F32))
        return jnp.concatenate(y_parts, axis=-1)

    def combine(self, cc, g, p, first, y_intra):
        rs = slice(cc * CHUNK, (cc + 1) * CHUNK)
        gs = slice(g * self.gw, (g + 1) * self.gw)
        y = y_intra + first["read"] * p["e_cum"]
        if p["skip"] is not None:
            y = y + p["skip"]
        self.o_ref[0, rs, gs] = y.astype(self.o_ref.dtype)
        self.states[g] = self.states[g] * p["e_last"] + first["update"]

    def finish(self):
        for g in range(self.n_groups):
            self.s_ref[g] = self.states[g]


def _ssd_kernel(xf_ref, bcf_ref, dtf_ref, laf_ref, xb_ref, bcb_ref, dtb_ref, lab_ref, e_ref, skip_ref,
                of_ref, ob_ref, sf_ref, sb_ref):
    @pl.when(pl.program_id(1) == 0)
    def _():
        sf_ref[...] = jnp.zeros_like(sf_ref)
        sb_ref[...] = jnp.zeros_like(sb_ref)

    fwd = _SsdDir(False, xf_ref, bcf_ref, dtf_ref, laf_ref, e_ref, of_ref, sf_ref, skip_ref)
    bwd = _SsdDir(True, xb_ref, bcb_ref, dtb_ref, lab_ref, e_ref, ob_ref, sb_ref)
    groups = range(fwd.n_groups)
    steps = [(d, k) for k in range(fwd.n_chunks) for d in (fwd, bwd)]
    d0, k0 = steps[0]
    all_sc = [d.scalars(d.chunk(k)) for d, k in steps]
    sc = all_sc[0]
    ready = [d0.prep(d0.chunk(k0), g, d0.expand(g, sc)) for g in groups]
    for idx, (d, k) in enumerate(steps):
        nxt = steps[idx + 1] if idx + 1 < len(steps) else None
        if nxt is not None:
            nd, nk = nxt
            nsc = all_sc[idx + 1]
        cc = d.chunk(k)
        first = [d.first_matmuls(cc, g, ready[g]) for g in groups]
        following = []
        for g in groups:
            y_intra = d.intra(g, ready[g], first[g])
            if nxt is not None:
                following.append(nd.prep(nd.chunk(nk), g, nd.expand(g, nsc)))
            d.combine(cc, g, ready[g], first[g], y_intra)
        ready = following
    fwd.finish()
    bwd.finish()


def _ssd_scan(xs, bc, dtp, expand_mat, d_skip):
    b, tt, width = xs.shape
    nt = tt // TM
    nbc = bc.shape[2]
    gw = width // SSD_GROUPS

    def dir_specs(rev):
        blk = lambda i: _scan_block(i, nt, rev)
        d_col = 1 if rev else 0
        return [pl.BlockSpec((1, TM, width), lambda bi, i: (bi, blk(i), 0)),
                pl.BlockSpec((1, TM, nbc), lambda bi, i: (bi, blk(i), 0)),
                pl.BlockSpec((1, TM, LANES), lambda bi, i: (bi, blk(i), d_col)),
                pl.BlockSpec((1, TM, LANES), lambda bi, i: (bi, blk(i), 2 + d_col))]

    out_spec = lambda rev: pl.BlockSpec((1, TM, width), lambda bi, i: (bi, _scan_block(i, nt, rev), 0))
    return pl.pallas_call(
        _ssd_kernel,
        grid=(b, nt),
        in_specs=dir_specs(False) + dir_specs(True) + [pl.BlockSpec(expand_mat.shape, lambda bi, i: (0, 0)),
                                                       pl.BlockSpec((1, width), lambda bi, i: (0, 0))],
        out_specs=[out_spec(False), out_spec(True)],
        out_shape=[jax.ShapeDtypeStruct((b, tt, width), BF16)] * 2,
        scratch_shapes=[pltpu.VMEM((SSD_GROUPS, SSD_STATE, gw), F32)] * 2,
        compiler_params=_cparams(("parallel", "arbitrary")),
        name="ssd_scan",
    )(xs, bc, dtp, dtp, xs, bc, dtp, dtp, expand_mat, d_skip)


def _outproj1_kernel(yf_ref, yb_ref, z_ref, h_ref, m_ref, gain_ref, w_ref, o_ref):
    width = z_ref.shape[2]
    gw = width // SSD_GROUPS
    for bi in range(BB):
        acc = None
        for g in range(SSD_GROUPS):
            gs = slice(g * gw, (g + 1) * gw)
            yg = (yf_ref[bi, :, gs].astype(F32) + yb_ref[bi, :, gs].astype(F32)) * _silu(z_ref[bi, :, gs].astype(F32))
            ms = jnp.mean(yg * yg, axis=-1, keepdims=True)
            part = jnp.dot((yg * lax.rsqrt(ms + EPS) * gain_ref[:, gs]).astype(BF16), w_ref[gs, :],
                           preferred_element_type=F32)
            acc = part if acc is None else acc + part
        o_ref[bi] = h_ref[bi] + m_ref[bi, 2:3, :] * acc


def _outproj1(y_f, y_b, z, hh, mod, gain, w, n_lat_tiles):
    b, _, d = hh.shape
    width = z.shape[2]
    tile = _tile_spec
    return pl.pallas_call(
        _outproj1_kernel,
        grid=(b // BB, n_lat_tiles),
        in_specs=[tile(width), tile(width), tile(width), tile(d),
                  _mod_spec(d, n_lat_tiles), _const_spec((1, width)), _const_spec(w.shape)],
        out_specs=tile(d),
        out_shape=jax.ShapeDtypeStruct((b, n_lat_tiles * TM, d), F32),
        compiler_params=_cparams(("parallel", "parallel")),
        name="outproj1",
    )(y_f, y_b, z, hh, mod, gain, w)


def _mod_rows(mods_l, b, d):
    six = mods_l.reshape(mods_l.shape[0], 6, d)
    lat = six[:b]
    ctx = jnp.broadcast_to(six[b:b + 1], (b, 6, d))
    both = jnp.stack([lat, ctx], axis=1)
    return jnp.pad(both, ((0, 0), (0, 0), (0, 2), (0, 0)))


def kernel(x, c, ctx, c_ctx, w_mod, b_mod, norm_mix, norm_ffn, ffn_w_up, ffn_conv_w, ffn_conv_b, ffn_w_down,
           hy_w_in, hy_w_out, na_q_gain, na_k_gain, na_rpb, hg_out_gain, hg_lb_fwd, hg_lb_bwd, ssd_w_in,
           ssd_conv_w, ssd_conv_b, ssd_dt_bias_fwd, ssd_dt_bias_bwd, ssd_a_log_fwd, ssd_a_log_bwd, ssd_d,
           ssd_norm_gain, ssd_w_out):
    b, t_lat, d = x.shape
    l_ctx = ctx.shape[1]
    assert l_ctx == TM and t_lat % TM == 0 and w_mod.shape[0] == 2 and b % BB == 0
    n_lat_tiles = t_lat // TM
    nt = n_lat_tiles + 1

    rows = ((b + 1 + 7) // 8) * 8
    cond = jnp.concatenate([c, c_ctx[None], jnp.zeros((rows - b - 1, d), F32)], axis=0)
    mods = _modulation(cond, w_mod, b_mod)
    mod0 = _mod_rows(mods[0], b, d)
    mod1 = _mod_rows(mods[1], b, d)
    row = lambda v: v.reshape(1, -1).astype(F32)

    hg_w = hg_lb_fwd.shape[1]
    na_w = hy_w_out.shape[1] - hg_w
    n_na_heads = na_w // NA_HEAD_DIM
    qg = row(jnp.tile(na_q_gain[0], n_na_heads)) * (NA_HEAD_DIM ** -0.5)
    kg = row(jnp.tile(na_k_gain[0], n_na_heads))
    qkv, hraw = _inproj0(x, ctx, mod0, row(norm_mix[0]), hy_w_in[0].astype(BF16), qg, kg)
    bias = _na_bias_table(na_rpb[0], t_lat // GRID_W)
    o_na = _na_attention(qkv, bias, t_lat)

    def lb_rows(lb_param):
        lb = jnp.cumsum(jax.nn.softmax(lb_param.astype(F32), axis=0), axis=0)[0]
        return jnp.pad(jnp.stack([lb, 1.0 - lb], axis=0), ((0, 6), (0, 0)))

    o_f, o_b = _hgrn_scan(hraw, lb_rows(hg_lb_fwd), lb_rows(hg_lb_bwd))
    hg_gain = row(jnp.tile(hg_out_gain[0], hg_w // HG_HEAD_DIM))
    h1 = _outproj0(o_na, o_f, o_b, hraw, x, ctx, mod0, hg_gain, hy_w_out[0].astype(BF16))

    mid0 = _ffn_up(h1, mod0, row(norm_ffn[0]), ffn_w_up[0].astype(BF16), ffn_conv_w[0], row(ffn_conv_b[0]),
                   nt, n_lat_tiles)
    h2 = _ffn_down(mid0, ffn_w_down[0].astype(BF16), h1, mod0, n_lat_tiles)

    n_heads = ssd_d.shape[1]
    inner = ssd_w_out.shape[1]
    nxbc = ssd_conv_w.shape[2]
    w1 = ssd_w_in[0]
    rep = LANES // n_heads
    w_dtf = jnp.tile(w1[:, inner + nxbc:inner + nxbc + n_heads], (1, rep))
    w_dtb = jnp.tile(w1[:, inner + nxbc + n_heads:], (1, rep))
    w1p = jnp.concatenate([w1[:, :inner + nxbc], w_dtf, w_dtb], axis=1).astype(BF16)
    dt_bias = row(jnp.concatenate([jnp.tile(ssd_dt_bias_fwd[0], rep), jnp.tile(ssd_dt_bias_bwd[0], rep)]))
    a_neg = row(jnp.concatenate([jnp.tile(-jnp.exp(ssd_a_log_fwd[0].astype(F32)), rep),
                                 jnp.tile(-jnp.exp(ssd_a_log_bwd[0].astype(F32)), rep)]))
    z, xs, bc, dtp = _inproj1(h2, mod1, row(norm_mix[1]), w1p, ssd_conv_w[0], row(ssd_conv_b[0]), dt_bias,
                              a_neg, inner, inner)

    lane = np.arange(LANES)[:, None]
    colh = (np.arange(inner) // SSD_HEAD_DIM)[None, :]
    expand_mat = jnp.asarray(((lane % n_heads == colh) & (lane < 3 * n_heads)).astype(np.float32), dtype=BF16)
    d_skip = row(jnp.repeat(ssd_d[0], SSD_HEAD_DIM))
    y_f, y_b = _ssd_scan(xs, bc, dtp, expand_mat, d_skip)
    h3 = _outproj1(y_f, y_b, z, h2, mod1, row(ssd_norm_gain[0]), ssd_w_out[0].astype(BF16), n_lat_tiles)

    mid1 = _ffn_up(h3, mod1, row(norm_ffn[1]), ffn_w_up[1].astype(BF16), ffn_conv_w[1], row(ffn_conv_b[1]),
                   n_lat_tiles, n_lat_tiles)
    return _ffn_down(mid1, ffn_w_down[1].astype(BF16), h3, mod1, n_lat_tiles)
```

```python
import functools
import math

import numpy as np
import jax
import jax.numpy as jnp
from jax import lax
from jax.experimental import pallas as pl
from jax.experimental.pallas import tpu as pltpu

F32 = jnp.float32
BF16 = jnp.bfloat16
EPS = 1e-6

GRID_W = 64
NA_HEAD_DIM = 64
WIN_ROWS = 8
WIN_COLS = 16
NA_ROW_BLOCK = 4
HG_HEAD_DIM = 128
SSD_HEAD_DIM = 64
SSD_GROUPS = 4
SSD_STATE = 128
CHUNK = 64
SUB = 16
TM = 256
BB = 2
LANES = 128
MASK_NEG = -1e30
SAFE_RANGE = 60.0

VMEM_LIMIT = 56 * 1024 * 1024


def _cparams(sem):
    return pltpu.CompilerParams(dimension_semantics=sem, vmem_limit_bytes=VMEM_LIMIT)


def _dot(a, b):
    return jnp.dot(a.astype(BF16), b.astype(BF16), preferred_element_type=F32)


def _dot_nt(a, b):
    return lax.dot_general(a.astype(BF16), b.astype(BF16), (((1,), (1,)), ((), ())),
                           preferred_element_type=F32)


def _silu(x):
    h = 0.5 * x
    return h + h * jnp.tanh(h)


def _softplus(x):
    return jnp.maximum(x, 0.0) + jnp.log1p(jnp.exp(-jnp.abs(x)))


def _gelu_tanh(x):
    c = math.sqrt(2.0 / math.pi)
    return 0.5 * x * (1.0 + jnp.tanh(c * (x + 0.044715 * (x * x * x))))


def _rms_mod(x, gain, shift, scale):
    ms = jnp.mean(x * x, axis=-1, keepdims=True)
    return (x * lax.rsqrt(ms + EPS) * gain) * (1.0 + scale) + shift


def _split3(v):
    p1 = v.astype(BF16)
    r1 = v - p1.astype(F32)
    p2 = r1.astype(BF16)
    r2 = r1 - p2.astype(F32)
    return p1, p2, r2.astype(BF16)


def _tri_cumsum(tri, v):
    p1, p2, p3 = _split3(v)
    d = lambda p: jnp.dot(tri, p, preferred_element_type=F32)
    return d(p1) + d(p2) + d(p3)


def _mod_spec(d, n_lat_tiles):
    return pl.BlockSpec((BB, None, 8, d), lambda i, t: (i, t // n_lat_tiles, 0, 0))


def _tile_spec(width, buffers=None):
    if buffers is None:
        return pl.BlockSpec((BB, TM, width), lambda i, t: (i, t, 0))
    return pl.BlockSpec((BB, TM, width), lambda i, t: (i, t, 0), pipeline_mode=pl.Buffered(buffers))


def _const_spec(shape):
    return pl.BlockSpec(shape, lambda i, t: (0,) * len(shape))


def _stream0_specs(d, n_lat_tiles):
    return [pl.BlockSpec((BB, TM, d), lambda i, t: (i, jnp.minimum(t, n_lat_tiles - 1), 0)),
            pl.BlockSpec((BB, TM, d), lambda i, t: (i, 0, 0))]


def _stream0_rows(x_ref, c_ref, bi, n_lat_tiles):
    return jnp.where(pl.program_id(1) == n_lat_tiles, c_ref[bi], x_ref[bi])


def _col_chunks(width, step):
    return [(lo, min(lo + step, width)) for lo in range(0, width, step)]


def _tri_matrix(n, rev):
    r = lax.broadcasted_iota(jnp.int32, (n, n), 0)
    c = lax.broadcasted_iota(jnp.int32, (n, n), 1)
    return jnp.where((c >= r) if rev else (c <= r), 1.0, 0.0).astype(BF16)


def _mod_kernel(s_ref, w_ref, b_ref, o_ref):
    s = _silu(s_ref[...])
    o_ref[0] = _dot(s, w_ref[0]) + b_ref[0]


def _modulation(cond, w_mod, b_mod):
    depth, d, n = w_mod.shape
    rows = cond.shape[0]
    tn = 1536
    return pl.pallas_call(
        _mod_kernel,
        grid=(depth, n // tn),
        in_specs=[pl.BlockSpec((rows, d), lambda l, j: (0, 0)),
                  pl.BlockSpec((1, d, tn), lambda l, j: (l, 0, j)),
                  pl.BlockSpec((1, 1, tn), lambda l, j: (l, 0, j))],
        out_specs=pl.BlockSpec((1, rows, tn), lambda l, j: (l, 0, j)),
        out_shape=jax.ShapeDtypeStruct((depth, rows, n), F32),
        compiler_params=_cparams(("parallel", "parallel")),
        name="modulation",
    )(cond, w_mod, b_mod.reshape(depth, 1, n))


def _inproj0_kernel(x_ref, c_ref, m_ref, gain_ref, w_ref, qg_ref, kg_ref, qkv_ref, hraw_ref, *, n_lat_tiles):
    u = jnp.concatenate(
        [_rms_mod(_stream0_rows(x_ref, c_ref, bi, n_lat_tiles), gain_ref[...], m_ref[bi, 0:1, :],
                  m_ref[bi, 1:2, :]).astype(BF16) for bi in range(BB)], axis=0)
    lo = lax.broadcasted_iota(jnp.int32, (1, LANES), 1) < NA_HEAD_DIM
    na_w = qg_ref.shape[1]

    def head_norm(y, g_ref):
        outs = []
        for c in range(na_w // LANES):
            yc = y[:, c * LANES:(c + 1) * LANES]
            sq = yc * yc
            s_lo = jnp.sum(jnp.where(lo, sq, 0.0), axis=-1, keepdims=True)
            s_hi = jnp.sum(jnp.where(lo, 0.0, sq), axis=-1, keepdims=True)
            inv = jnp.where(lo, lax.rsqrt(s_lo / NA_HEAD_DIM + EPS), lax.rsqrt(s_hi / NA_HEAD_DIM + EPS))
            outs.append(yc * inv * g_ref[:, c * LANES:(c + 1) * LANES])
        return jnp.concatenate(outs, axis=-1)

    def proj(lo_col, width):
        return jnp.dot(u, w_ref[:, lo_col:lo_col + width], preferred_element_type=F32)

    def store(ref, lo_col, y):
        for bi in range(BB):
            ref[bi, :, lo_col:lo_col + y.shape[1]] = y[bi * TM:(bi + 1) * TM].astype(ref.dtype)

    store(qkv_ref, 0, head_norm(proj(0, na_w), qg_ref))
    store(qkv_ref, na_w, head_norm(proj(na_w, na_w), kg_ref))
    store(qkv_ref, 2 * na_w, proj(2 * na_w, na_w))
    for lo_col, hi_col in _col_chunks(hraw_ref.shape[2], 512):
        store(hraw_ref, lo_col, proj(3 * na_w + lo_col, hi_col - lo_col))


def _inproj0(x, ctx, mod, gain, w, qg, kg):
    b, t_lat, d = x.shape
    n = w.shape[1]
    na_w = qg.shape[1]
    n_hg = n - 3 * na_w
    n_lat_tiles = t_lat // TM
    tt = t_lat + ctx.shape[1]
    return pl.pallas_call(
        functools.partial(_inproj0_kernel, n_lat_tiles=n_lat_tiles),
        grid=(b // BB, n_lat_tiles + 1),
        in_specs=_stream0_specs(d, n_lat_tiles) + [
            _mod_spec(d, n_lat_tiles), _const_spec((1, d)), _const_spec((d, n)),
            _const_spec((1, na_w)), _const_spec((1, na_w))],
        out_specs=[_tile_spec(3 * na_w), _tile_spec(n_hg)],
        out_shape=[jax.ShapeDtypeStruct((b, tt, 3 * na_w), BF16),
                   jax.ShapeDtypeStruct((b, tt, n_hg), F32)],
        compiler_params=_cparams(("parallel", "parallel")),
        name="inproj0",
    )(x, ctx, mod, gain, w, qg, kg)


def _na_kernel(q_ref, k_ref, v_ref, bias_ref, o_ref, *, t_lat, rows):
    tt = q_ref.shape[1]
    lane = lax.broadcasted_iota(jnp.int32, (1, LANES), 1)
    head_mask = [lane < NA_HEAD_DIM, lane >= NA_HEAD_DIM]
    kc = k_ref[0, t_lat:tt, :]
    vc = v_ref[0, t_lat:tt, :]
    _, ku, u0s, _, case_of_block = _na_block_plan(rows)
    n_loc = ku * GRID_W
    n_q = NA_ROW_BLOCK * GRID_W

    qc = q_ref[0, t_lat:tt, :]
    n_c = tt - t_lat
    s = _dot_nt(jnp.concatenate([jnp.where(hm, qc, jnp.zeros_like(qc)) for hm in head_mask], axis=0), kc)
    p = jnp.exp(s - jnp.max(s, axis=-1, keepdims=True))
    o_both = _dot(p, vc) / jnp.sum(p, axis=-1, keepdims=True)
    oc = None
    for hi, hm in enumerate(head_mask):
        o = o_both[hi * n_c:(hi + 1) * n_c]
        oc = o if oc is None else jnp.where(hm, o, oc)
    o_ref[0, t_lat:tt, :] = oc.astype(o_ref.dtype)

    n_blocks = rows // NA_ROW_BLOCK
    group = 2 if n_blocks % 2 == 0 else 1

    def blocks_body(it, carry):
        loaded, scores = [], []
        for j in range(group):
            i = it * group + j
            u0 = u0s[0]
            case = case_of_block[0]
            for bi in range(1, len(u0s)):
                u0 = jnp.where(i == bi, u0s[bi], u0)
                case = jnp.where(i == bi, case_of_block[bi], case)
            q_rows = pl.ds(pl.multiple_of(i * n_q, n_q), n_q)
            q_i = q_ref[0, q_rows, :]
            start = pl.multiple_of(u0 * GRID_W, GRID_W)
            k_loc = k_ref[0, pl.ds(start, n_loc), :]
            loaded.append((q_rows, v_ref[0, pl.ds(start, n_loc), :]))
            q_both = jnp.concatenate([jnp.where(hm, q_i, jnp.zeros_like(q_i)) for hm in head_mask], axis=0)
            s_loc = _dot_nt(q_both, k_loc)
            s_ctx = _dot_nt(q_both, kc)
            for hi in range(len(head_mask)):
                hr = slice(hi * n_q, (hi + 1) * n_q)
                scores.append((s_loc[hr] + bias_ref[hi, case], s_ctx[hr]))
        probs = []
        for s_loc, s_ctx in scores:
            m = jnp.maximum(jnp.max(s_loc, axis=-1, keepdims=True), jnp.max(s_ctx, axis=-1, keepdims=True))
            p_loc = jnp.exp(s_loc - m)
            p_ctx = jnp.exp(s_ctx - m)
            den = jnp.sum(p_loc, axis=-1, keepdims=True) + jnp.sum(p_ctx, axis=-1, keepdims=True)
            probs.append((p_loc.astype(BF16), p_ctx.astype(BF16), den))
        n_hd = len(head_mask)
        for j, (q_rows, v_loc) in enumerate(loaded):
            mine = probs[j * n_hd:(j + 1) * n_hd]
            o_both = (jnp.dot(jnp.concatenate([p[0] for p in mine], axis=0), v_loc, preferred_element_type=F32)
                      + jnp.dot(jnp.concatenate([p[1] for p in mine], axis=0), vc, preferred_element_type=F32))
            out = None
            for hi, hm in enumerate(head_mask):
                o = o_both[hi * n_q:(hi + 1) * n_q] / mine[hi][2]
                out = o if out is None else jnp.where(hm, o, out)
            o_ref[0, q_rows, :] = out.astype(o_ref.dtype)
        return carry

    lax.fori_loop(0, n_blocks // group, blocks_body, 0)


def _na_block_plan(rows):
    kr = min(WIN_ROWS, rows)
    ku = min(kr + NA_ROW_BLOCK - 1, rows)
    u0s, patterns, case_of_block = [], [], []
    for i in range(rows // NA_ROW_BLOCK):
        u0 = int(np.clip(NA_ROW_BLOCK * i - kr // 2, 0, rows - ku))
        rel = []
        for a in range(NA_ROW_BLOCK):
            r = NA_ROW_BLOCK * i + a
            r0 = int(np.clip(r - kr // 2, 0, rows - kr))
            assert u0 <= r0 and r0 + kr <= u0 + ku
            rel.append((r0 - u0, u0 - r))
        u0s.append(u0)
        if tuple(rel) not in patterns:
            patterns.append(tuple(rel))
        case_of_block.append(patterns.index(tuple(rel)))
    return kr, ku, u0s, patterns, case_of_block


def _na_bias_table(rpb, rows):
    kr, ku, _, patterns, _ = _na_block_plan(rows)
    q = np.arange(GRID_W)
    kcol = np.arange(GRID_W)
    ws = np.clip(q - WIN_COLS // 2, 0, GRID_W - WIN_COLS)
    in_win = (kcol[None, :] >= ws[:, None]) & (kcol[None, :] < ws[:, None] + WIN_COLS)
    dc = np.clip(kcol[None, :] - q[:, None] + WIN_COLS - 1, 0, 2 * WIN_COLS - 2)
    u = np.arange(ku)
    n_dr = 2 * WIN_ROWS - 1
    dr = np.full((len(patterns), NA_ROW_BLOCK, ku), n_dr, np.int32)
    for c, rel in enumerate(patterns):
        for a, (r0_rel, u0_minus_r) in enumerate(rel):
            row_ok = (u >= r0_rel) & (u < r0_rel + kr)
            dr[c, a] = np.where(row_ok, u0_minus_r + u + WIN_ROWS - 1, n_dr)
    h = rpb.shape[0]
    tiles = jnp.where(in_win[None, None], rpb.astype(F32)[:, :, dc], MASK_NEG)
    tiles = jnp.concatenate([tiles, jnp.full((h, 1, GRID_W, GRID_W), MASK_NEG, F32)], axis=1)
    tbl = jnp.take(tiles, jnp.asarray(dr.reshape(-1)), axis=1)
    tbl = tbl.reshape(h, len(patterns), NA_ROW_BLOCK, ku, GRID_W, GRID_W)
    return tbl.transpose(0, 1, 2, 4, 3, 5).reshape(h, len(patterns), NA_ROW_BLOCK * GRID_W, ku * GRID_W)


def _na_attention(qkv, bias, t_lat):
    b, tt, w3 = qkv.shape
    na_w = w3 // 3
    ncol = na_w // LANES
    rows = t_lat // GRID_W
    bias_block = (2,) + bias.shape[1:]
    kern = functools.partial(_na_kernel, t_lat=t_lat, rows=rows)
    return pl.pallas_call(
        kern,
        grid=(b, ncol),
        in_specs=[pl.BlockSpec((1, tt, LANES), lambda i, p: (i, 0, p)),
                  pl.BlockSpec((1, tt, LANES), lambda i, p: (i, 0, ncol + p)),
                  pl.BlockSpec((1, tt, LANES), lambda i, p: (i, 0, 2 * ncol + p)),
                  pl.BlockSpec(bias_block, lambda i, p: (p, 0, 0, 0))],
        out_specs=pl.BlockSpec((1, tt, LANES), lambda i, p: (i, 0, p)),
        out_shape=jax.ShapeDtypeStruct((b, tt, na_w), BF16),
        compiler_params=_cparams(("parallel", "parallel")),
        name="na_attention",
    )(qkv, qkv, qkv, bias)


def _scan_block(i, nt, rev):
    if rev:
        return jnp.where(i == 0, nt - 1, nt - 1 - i)
    return jnp.where(i == 0, nt - 1, i - 1)


class _HgrnDir:
    def __init__(self, rev, q_ref, f_ref, v_ref, lb_ref, o_ref, st_ref):
        self.rev, self.q_ref, self.f_ref, self.v_ref, self.o_ref, self.st_ref = rev, q_ref, f_ref, v_ref, o_ref, st_ref
        self.n_heads = st_ref.shape[0]
        self.lb = lb_ref[0:1, :]
        self.one_m_lb = lb_ref[1:2, :]
        self.n_chunks = q_ref.shape[1] // CHUNK
        self.order = list(range(self.n_chunks - 1, -1, -1)) if rev else list(range(self.n_chunks))
        self.heads = [slice(h * HG_HEAD_DIM, (h + 1) * HG_HEAD_DIM) for h in range(self.n_heads)]
        self.states = [st_ref[h] for h in range(self.n_heads)]
        self.gated = {}
        self.tri = _tri_matrix(CHUNK, rev)
        row = lax.broadcasted_iota(jnp.int32, (CHUNK, CHUNK), 0)
        col = lax.broadcasted_iota(jnp.int32, (CHUNK, CHUNK), 1)
        self.causal = (col >= row) if rev else (col <= row)

    def gate(self, cc):
        lb, one_m_lb, q_ref, f_ref = self.lb, self.one_m_lb, self.q_ref, self.f_ref
        rs = slice(cc * CHUNK, (cc + 1) * CHUNK)
        x = f_ref[0, rs, :]
        e = jnp.exp(-jnp.abs(x))
        r = 1.0 / (1.0 + e)
        x_pos = x >= 0.0
        f = jnp.where(x_pos, 1.0 + lb * e, lb + e) * r
        log_f = jnp.where(f > 0.0, jnp.log(f), x)
        k_all = one_m_lb * jnp.where(x_pos, e, 1.0) * r
        return _silu(q_ref[0, rs, :]), k_all, log_f

    def decays(self, cc, gates):
        rev = self.rev
        q_all, k_all, log_f = gates
        n_sub = CHUNK // SUB
        zero_row = jnp.zeros((1, log_f.shape[1]), F32)
        scan_blocks = list(range(n_sub - 1, -1, -1)) if rev else list(range(n_sub))
        rng = None
        g_all = _tri_cumsum(self.tri, log_f)
        c = [g_all[j * SUB:j * SUB + 1, :] if rev else g_all[(j + 1) * SUB - 1:(j + 1) * SUB, :]
             for j in range(n_sub)]
        c_prev = [zero_row] * n_sub
        for before, after in zip(scan_blocks[:-1], scan_blocks[1:]):
            c_prev[after] = c[before]
        for j in range(n_sub):
            d = c_prev[j] - c[j]
            rng = d if rng is None else jnp.maximum(rng, d)
        self.gated[cc] = (q_all, k_all, g_all, c, c_prev)
        return rng

    def slow_diag(self):
        rev = self.rev
        row = lax.broadcasted_iota(jnp.int32, (CHUNK, CHUNK), 0)
        col = lax.broadcasted_iota(jnp.int32, (CHUNK, CHUNK), 1)
        rowv = lax.broadcasted_iota(jnp.int32, (CHUNK, 1), 0)
        pos = rowv % SUB
        accs = []
        for cc in self.order:
            q_all, k_all, g_all = self.gated[cc][:3]
            for hs in self.heads:
                q, k, g = q_all[:, hs], k_all[:, hs], g_all[:, hs]
                acc = jnp.zeros((CHUNK, CHUNK), F32)
                for dlt in range(SUB):
                    shift = (CHUNK - dlt) % CHUNK if rev else dlt
                    k_d = pltpu.roll(k, shift, 0) if shift else k
                    g_d = pltpu.roll(g, shift, 0) if shift else g
                    valid = (pos + dlt <= SUB - 1) if rev else (pos >= dlt)
                    e = jnp.exp(jnp.where(valid, g - g_d, 0.0))
                    val = jnp.sum(q * k_d * e, axis=-1, keepdims=True)
                    partner = (row + dlt) if rev else (row - dlt)
                    acc = acc + jnp.where((col == partner) & valid, val, 0.0)
                accs.append(acc)
        return jnp.stack(accs, axis=0)

    def prepare(self, ci, safe):
        rev = self.rev
        n_sub = CHUNK // SUB
        cc = self.order[ci]
        q_all, k_all, g_all, c, c_prev = self.gated[cc]
        width = g_all.shape[1]
        scan_blocks = list(range(n_sub - 1, -1, -1)) if rev else list(range(n_sub))
        c_final = c[scan_blocks[-1]]
        blk = lambda x, j: x[j * SUB:(j + 1) * SUB]

        def per_block(fn):
            return jnp.concatenate([fn(j) for j in range(n_sub)], axis=0)

        q_b = per_block(lambda j: blk(q_all, j) * jnp.exp(blk(g_all, j) - c_prev[j]))
        k_end = per_block(lambda j: blk(k_all, j) * jnp.exp(c[j] - blk(g_all, j)))
        k_comb = []
        for sb in range(n_sub):
            def scaled(j, sb=sb):
                if j == sb:
                    return blk(k_end, j) * jnp.where(safe, jnp.exp(jnp.minimum(c_prev[sb] - c[sb], SAFE_RANGE)), 0.0)
                if scan_blocks.index(j) < scan_blocks.index(sb):
                    return blk(k_end, j) * jnp.exp(c_prev[sb] - c[j])
                return jnp.zeros((SUB, width), F32)
            k_comb.append(per_block(scaled))
        return dict(
            rows=slice(cc * CHUNK, (cc + 1) * CHUNK), q_b=q_b, k_comb=k_comb,
            q_g=per_block(lambda j: blk(q_b, j) * jnp.exp(c_prev[j])),
            k_last=per_block(lambda j: blk(k_end, j) * jnp.exp(c_final - c[j])),
            e_last=jnp.exp(c_final), v=self.v_ref[0, slice(cc * CHUNK, (cc + 1) * CHUNK), :], outs=[])

    def scores(self, h, p, diag_block):
        n_sub = CHUNK // SUB
        hs = self.heads[h]
        blocks = [_dot_nt(p["q_b"][sb * SUB:(sb + 1) * SUB, hs], p["k_comb"][sb][:, hs]) for sb in range(n_sub)]
        return jnp.where(self.causal, jnp.concatenate(blocks, axis=0), 0.0) + diag_block

    def values(self, h, p, a):
        hs = self.heads[h]
        st = self.states[h]
        p["outs"].append(_dot(a, p["v"][:, hs]) + _dot_nt(p["q_g"][:, hs], st))
        self.states[h] = st * p["e_last"][:, hs] + lax.dot_general(
            p["v"][:, hs].astype(BF16), p["k_last"][:, hs].astype(BF16), (((0,), (0,)), ((), ())),
            preferred_element_type=F32)

    def store(self, p):
        self.o_ref[0, p["rows"], :] = jnp.concatenate(p["outs"], axis=-1).astype(self.o_ref.dtype)

    def finish(self):
        for h in range(self.n_heads):
            self.st_ref[h] = self.states[h]


def _hgrn_kernel(qf_ref, ff_ref, vf_ref, lbf_ref, qb_ref, fb_ref, vb_ref, lbb_ref, of_ref, ob_ref, sf_ref, sb_ref):
    @pl.when(pl.program_id(1) == 0)
    def _():
        sf_ref[...] = jnp.zeros_like(sf_ref)
        sb_ref[...] = jnp.zeros_like(sb_ref)

    dirs = [_HgrnDir(False, qf_ref, ff_ref, vf_ref, lbf_ref, of_ref, sf_ref),
            _HgrnDir(True, qb_ref, fb_ref, vb_ref, lbb_ref, ob_ref, sb_ref)]
    n_chunks, n_heads = dirs[0].n_chunks, dirs[0].n_heads
    rng = None
    stage1 = [(d, d.order[ci]) for ci in range(n_chunks) for d in dirs]
    gates = [d.gate(cc) for d, cc in stage1]
    for (d, cc), gt in zip(stage1, gates):
        r = d.decays(cc, gt)
        rng = r if rng is None else jnp.maximum(rng, r)
    safe = jnp.max(rng) <= SAFE_RANGE
    per_dir = n_chunks * n_heads
    diag = lax.cond(safe, lambda: jnp.zeros((2 * per_dir, CHUNK, CHUNK), F32),
                    lambda: jnp.concatenate([d.slow_diag() for d in dirs], axis=0))
    for ci in range(n_chunks):
        preps = [d.prepare(ci, safe) for d in dirs]
        chains = [(h, di, d) for h in range(n_heads) for di, d in enumerate(dirs)]
        att = [d.scores(h, preps[di], diag[di * per_dir + ci * n_heads + h]) for h, di, d in chains]
        for (h, di, d), a in zip(chains, att):
            d.values(h, preps[di], a)
        for di, d in enumerate(dirs):
            d.store(preps[di])
    for d in dirs:
        d.finish()


def _hgrn_scan(hraw, lb_rows_f, lb_rows_b):
    b, tt, n = hraw.shape
    w = lb_rows_f.shape[1]
    nt = tt // TM
    n_heads = w // HG_HEAD_DIM

    def dir_specs(rev):
        blk = lambda i: _scan_block(i, nt, rev)
        f_col = 2 if rev else 1
        return [pl.BlockSpec((1, TM, w), lambda bi, i: (bi, blk(i), 0)),
                pl.BlockSpec((1, TM, w), lambda bi, i: (bi, blk(i), f_col)),
                pl.BlockSpec((1, TM, w), lambda bi, i: (bi, blk(i), 3)),
                pl.BlockSpec((8, w), lambda bi, i: (0, 0))]

    out_spec = lambda rev: pl.BlockSpec((1, TM, w), lambda bi, i: (bi, _scan_block(i, nt, rev), 0))
    return pl.pallas_call(
        _hgrn_kernel,
        grid=(b, nt),
        in_specs=dir_specs(False) + dir_specs(True),
        out_specs=[out_spec(False), out_spec(True)],
        out_shape=[jax.ShapeDtypeStruct((b, tt, w), BF16)] * 2,
        scratch_shapes=[pltpu.VMEM((n_heads, HG_HEAD_DIM, HG_HEAD_DIM), F32)] * 2,
        compiler_params=_cparams(("parallel", "arbitrary")),
        name="hgrn_scan",
    )(hraw, hraw, hraw, lb_rows_f, hraw, hraw, hraw, lb_rows_b)


def _outproj0_kernel(na_ref, of_ref, ob_ref, gate_ref, x_ref, c_ref, m_ref, gain_ref, w_ref, o_ref, *, n_lat_tiles):
    na_w = na_ref.shape[2]
    for bi in range(BB):
        acc = jnp.dot(na_ref[bi], w_ref[0:na_w, :], preferred_element_type=F32)
        o = of_ref[bi].astype(F32) + ob_ref[bi].astype(F32)
        gate = _silu(gate_ref[bi])
        ys = []
        for h in range(o.shape[1] // HG_HEAD_DIM):
            hs = slice(h * HG_HEAD_DIM, (h + 1) * HG_HEAD_DIM)
            oh = o[:, hs]
            ms = jnp.mean(oh * oh, axis=-1, keepdims=True)
            ys.append(oh * lax.rsqrt(ms + EPS) * gain_ref[:, hs] * gate[:, hs])
        acc = acc + jnp.dot(jnp.concatenate(ys, axis=-1).astype(BF16), w_ref[na_w:, :], preferred_element_type=F32)
        o_ref[bi] = _stream0_rows(x_ref, c_ref, bi, n_lat_tiles) + m_ref[bi, 2:3, :] * acc


def _outproj0(o_na, o_f, o_b, hraw, x, ctx, mod, gain, w):
    b, t_lat, d = x.shape
    tt = t_lat + ctx.shape[1]
    na_w = o_na.shape[2]
    hw = o_f.shape[2]
    n_lat_tiles = t_lat // TM
    return pl.pallas_call(
        functools.partial(_outproj0_kernel, n_lat_tiles=n_lat_tiles),
        grid=(b // BB, n_lat_tiles + 1),
        in_specs=[_tile_spec(na_w), _tile_spec(hw), _tile_spec(hw),
                  pl.BlockSpec((BB, TM, hw), lambda i, t: (i, t, 4))] + _stream0_specs(d, n_lat_tiles) + [
            _mod_spec(d, n_lat_tiles), _const_spec((1, hw)), _const_spec(w.shape)],
        out_specs=_tile_spec(d),
        out_shape=jax.ShapeDtypeStruct((b, tt, d), F32),
        compiler_params=_cparams(("parallel", "parallel")),
        name="outproj0",
    )(o_na, o_f, o_b, hraw, x, ctx, mod, gain, w)


HALO = 8
TM_EXT = TM + 2 * HALO


def _halo_specs(width, tt):
    r8 = TM // HALO
    last = tt // HALO - 1
    return [pl.BlockSpec((BB, HALO, width), lambda i, t: (i, jnp.maximum(t * r8 - 1, 0), 0)),
            pl.BlockSpec((BB, HALO, width), lambda i, t: (i, jnp.minimum((t + 1) * r8, last), 0))]


def _modulated_with_halo(h_ref, hp_ref, hn_ref, gain, m_ref, shift_row, scale_row):
    us, exts = [], []
    for bi in range(BB):
        shift = m_ref[bi, shift_row:shift_row + 1, :]
        scale = m_ref[bi, scale_row:scale_row + 1, :]
        u = _rms_mod(h_ref[bi], gain, shift, scale)
        us.append(u.astype(BF16))
        exts += [_rms_mod(hp_ref[bi], gain, shift, scale), u, _rms_mod(hn_ref[bi], gain, shift, scale)]
    return jnp.concatenate(us, axis=0), jnp.concatenate(exts, axis=0).astype(BF16)


def _conv3_ext(a_ext, cw, cb, n_lat_tiles):
    n = a_ext.shape[0] - 2 * HALO
    t = pl.program_id(1)
    has_prev = jnp.logical_and(t != 0, t != n_lat_tiles)
    has_next = jnp.logical_and(t != n_lat_tiles - 1, t != n_lat_tiles)
    a = a_ext[HALO:HALO + n]
    prev_row = jnp.where(has_prev, a_ext[HALO - 1:HALO], 0.0)
    next_row = jnp.where(has_next, a_ext[HALO + n:HALO + n + 1], 0.0)
    row = lax.broadcasted_iota(jnp.int32, (n, 1), 0)
    up = jnp.where(row == 0, prev_row, pltpu.roll(a, 1, 0))
    dn = jnp.where(row == n - 1, next_row, pltpu.roll(a, n - 1, 0))
    return cw[0:1, :] * up + cw[1:2, :] * a + cw[2:3, :] * dn + cb


def _ffn_up_kernel(h_ref, hp_ref, hn_ref, m_ref, gain_ref, w_ref, cw_ref, cb_ref, mid_ref, *, n_lat_tiles):
    u, u_ext = _modulated_with_halo(h_ref, hp_ref, hn_ref, gain_ref[...], m_ref, 3, 4)
    dff = mid_ref.shape[2]
    for lo, hi in _col_chunks(dff, 768):
        a_ext = jnp.dot(u_ext, w_ref[:, lo:hi], preferred_element_type=F32)
        v = jnp.dot(u, w_ref[:, dff + lo:dff + hi], preferred_element_type=F32)
        for bi in range(BB):
            c = _conv3_ext(a_ext[bi * TM_EXT:(bi + 1) * TM_EXT], cw_ref[:, lo:hi], cb_ref[:, lo:hi], n_lat_tiles)
            mid_ref[bi, :, lo:hi] = (_gelu_tanh(c) * v[bi * TM:(bi + 1) * TM]).astype(mid_ref.dtype)


def _ffn_up(hh, mod, gain, w, conv_w, conv_b, n_tiles, n_lat_tiles):
    b, tt_in, d = hh.shape
    dff = w.shape[1] // 2
    return pl.pallas_call(
        functools.partial(_ffn_up_kernel, n_lat_tiles=n_lat_tiles),
        grid=(b // BB, n_tiles),
        in_specs=[_tile_spec(d)] + _halo_specs(d, tt_in) + [
            _mod_spec(d, n_lat_tiles), _const_spec((1, d)), _const_spec(w.shape),
            _const_spec((3, dff)), _const_spec((1, dff))],
        out_specs=_tile_spec(dff),
        out_shape=jax.ShapeDtypeStruct((b, n_tiles * TM, dff), BF16),
        compiler_params=_cparams(("parallel", "parallel")),
        name="ffn_up",
    )(hh, hh, hh, mod, gain, w, conv_w, conv_b)


def _ffn_down_kernel(mid_ref, w_ref, h_ref, m_ref, o_ref):
    y = jnp.dot(jnp.concatenate([mid_ref[bi] for bi in range(BB)], axis=0), w_ref[...],
                preferred_element_type=F32)
    for bi in range(BB):
        o_ref[bi] = h_ref[bi] + m_ref[bi, 5:6, :] * y[bi * TM:(bi + 1) * TM]


def _ffn_down(mid, w, hh, mod, n_lat_tiles):
    b, tt, dff = mid.shape
    d = hh.shape[2]
    return pl.pallas_call(
        _ffn_down_kernel,
        grid=(b // BB, tt // TM),
        in_specs=[_tile_spec(dff), _const_spec(w.shape), _tile_spec(d), _mod_spec(d, n_lat_tiles)],
        out_specs=_tile_spec(d),
        out_shape=jax.ShapeDtypeStruct((b, tt, d), F32),
        compiler_params=_cparams(("parallel", "parallel")),
        name="ffn_down",
    )(mid, w, hh, mod)


def _inproj1_kernel(h_ref, hp_ref, hn_ref, m_ref, gain_ref, w_ref, cw_ref, cb_ref, dtb_ref, a_ref,
                    z_ref, xs_ref, bc_ref, dtp_ref, *, n_lat_tiles):
    u, u_ext = _modulated_with_halo(h_ref, hp_ref, hn_ref, gain_ref[...], m_ref, 0, 1)
    nz = z_ref.shape[2]
    nxs = xs_ref.shape[2]
    nx = nxs + bc_ref.shape[2]
    step = 512
    for lo, hi in _col_chunks(nz, step):
        z = jnp.dot(u, w_ref[:, lo:hi], preferred_element_type=F32)
        for bi in range(BB):
            z_ref[bi, :, lo:hi] = z[bi * TM:(bi + 1) * TM].astype(z_ref.dtype)
    for lo, hi in _col_chunks(nx, step):
        xbc_ext = jnp.dot(u_ext, w_ref[:, nz + lo:nz + hi], preferred_element_type=F32)
        for bi in range(BB):
            c = _silu(_conv3_ext(xbc_ext[bi * TM_EXT:(bi + 1) * TM_EXT], cw_ref[:, lo:hi], cb_ref[:, lo:hi],
                                 n_lat_tiles))
            if hi <= nxs:
                xs_ref[bi, :, lo:hi] = c.astype(xs_ref.dtype)
            else:
                bc_ref[bi, :, lo - nxs:hi - nxs] = c.astype(bc_ref.dtype)
    raw = jnp.dot(u, w_ref[:, nz + nx:], preferred_element_type=F32)
    dt = _softplus(raw + dtb_ref[...])
    la = dt * a_ref[...]
    for bi in range(BB):
        dtp_ref[bi, :, 0:2 * LANES] = dt[bi * TM:(bi + 1) * TM]
        dtp_ref[bi, :, 2 * LANES:4 * LANES] = la[bi * TM:(bi + 1) * TM]


def _inproj1(hh, mod, gain, w, conv_w, conv_b, dt_bias, a_neg, nz, nxs):
    b, tt, d = hh.shape
    nt = tt // TM
    n_lat_tiles = nt - 1
    nx = conv_w.shape[1]
    assert nxs % 512 == 0
    tile = _tile_spec
    return pl.pallas_call(
        functools.partial(_inproj1_kernel, n_lat_tiles=n_lat_tiles),
        grid=(b // BB, nt),
        in_specs=[tile(d)] + _halo_specs(d, tt) + [
            _mod_spec(d, n_lat_tiles), _const_spec((1, d)), _const_spec(w.shape),
            _const_spec((3, nx)), _const_spec((1, nx)),
            _const_spec((1, 2 * LANES)), _const_spec((1, 2 * LANES))],
        out_specs=[tile(nz), tile(nxs), tile(nx - nxs), tile(4 * LANES)],
        out_shape=[jax.ShapeDtypeStruct((b, tt, nz), BF16),
                   jax.ShapeDtypeStruct((b, tt, nxs), BF16),
                   jax.ShapeDtypeStruct((b, tt, nx - nxs), BF16),
                   jax.ShapeDtypeStruct((b, tt, 4 * LANES), F32)],
        compiler_params=_cparams(("parallel", "parallel")),
        name="inproj1",
    )(hh, hh, hh, mod, gain, w, conv_w, conv_b, dt_bias, a_neg)


class _SsdDir:
    def __init__(self, rev, x_ref, bc_ref, dt_ref, la_ref, e_ref, o_ref, s_ref, skip_ref=None):
        self.rev, self.x_ref, self.bc_ref, self.dt_ref, self.la_ref = rev, x_ref, bc_ref, dt_ref, la_ref
        self.e_ref, self.o_ref, self.s_ref, self.skip_ref = e_ref, o_ref, s_ref, skip_ref
        self.n_groups = s_ref.shape[0]
        self.gw = s_ref.shape[2]
        self.n_heads = x_ref.shape[2] // SSD_HEAD_DIM
        self.n_chunks = x_ref.shape[1] // CHUNK
        self.states = [s_ref[g] for g in range(self.n_groups)]
        self.tri = _tri_matrix(CHUNK, rev)
        rowc = lax.broadcasted_iota(jnp.int32, (CHUNK, self.gw), 0)
        pos = lax.broadcasted_iota(jnp.int32, (CHUNK, self.gw), 1) % SSD_HEAD_DIM
        self.on_diag = pos == rowc
        self.causal = (pos >= rowc) if rev else (pos <= rowc)

    def chunk(self, k):
        return self.n_chunks - 1 - k if self.rev else k

    def _stacked(self, v):
        lane = lax.broadcasted_iota(jnp.int32, (1, LANES), 1)
        p1, p2, p3 = _split3(v)
        return jnp.where(lane < self.n_heads, p1, jnp.where(lane < 2 * self.n_heads, p2,
                                                            jnp.where(lane < 3 * self.n_heads, p3, jnp.zeros_like(p1))))

    def scalars(self, cc):
        rs = slice(cc * CHUNK, (cc + 1) * CHUNK)
        return (self._stacked(_tri_cumsum(self.tri, self.la_ref[0, rs, :])), self._stacked(self.dt_ref[0, rs, :]))

    def expand(self, g, scalars):
        cum_st, dt_st = scalars
        gs = slice(g * self.gw, (g + 1) * self.gw)
        return (jnp.dot(cum_st, self.e_ref[:, gs], preferred_element_type=F32),
                jnp.dot(dt_st, self.e_ref[:, gs], preferred_element_type=F32))

    def prep(self, cc, g, expanded):
        cum, dt = expanded
        rs = slice(cc * CHUNK, (cc + 1) * CHUNK)
        gs = slice(g * self.gw, (g + 1) * self.gw)
        last = 0 if self.rev else CHUNK - 1
        cum_row = jnp.sum(jnp.where(self.on_diag, cum, 0.0), axis=0, keepdims=True)
        cum_last = cum[last:last + 1, :]
        x = self.x_ref[0, rs, gs].astype(F32)
        x_dt = x * dt
        return dict(
            skip=None if self.skip_ref is None else x * self.skip_ref[:, gs],
            xw=(x_dt * jnp.exp(cum_last - cum)).astype(BF16),
            x_dt=x_dt.astype(BF16),
            decay=jnp.exp(jnp.where(self.causal, cum - cum_row, MASK_NEG)),
            e_cum=jnp.exp(cum),
            e_last=jnp.exp(cum_last))

    def first_matmuls(self, cc, g, p):
        rs = slice(cc * CHUNK, (cc + 1) * CHUNK)
        b_g = self.bc_ref[0, rs, g * SSD_STATE:(g + 1) * SSD_STATE]
        c_g = self.bc_ref[0, rs, (self.n_groups + g) * SSD_STATE:(self.n_groups + g + 1) * SSD_STATE]
        b_rep = jnp.concatenate([b_g] * (self.gw // CHUNK), axis=0)
        return dict(cb=_dot_nt(c_g, b_rep), read=_dot(c_g, self.states[g]),
                    update=lax.dot_general(b_g, p["xw"], (((0,), (0,)), ((), ())), preferred_element_type=F32))

    def intra(self, g, p, first):
        blk = 256
        blk_head = lax.broadcasted_iota(jnp.int32, (1, blk), 1) // SSD_HEAD_DIM
        m_g = (p["decay"] * first["cb"]).astype(BF16)
        y_parts = []
        for j in range(self.gw // blk):
            x4 = p["x_dt"][:, j * blk:(j + 1) * blk]
            x_bd = jnp.concatenate(
                [jnp.where(blk_head == hh, x4, jnp.zeros_like(x4)) for hh in range(blk // SSD_HEAD_DIM)], axis=0)
            y_parts.append(jnp.dot(m_g[:, j * blk:(j + 1) * blk], x_bd, preferred_element_type=F32))
        return jnp.concatenate(y_parts, axis=-1)

    def combine(self, cc, g, p, first, y_intra):
        rs = slice(cc * CHUNK, (cc + 1) * CHUNK)
        gs = slice(g * self.gw, (g + 1) * self.gw)
        y = y_intra + first["read"] * p["e_cum"]
        if p["skip"] is not None:
            y = y + p["skip"]
        self.o_ref[0, rs, gs] = y.astype(self.o_ref.dtype)
        self.states[g] = self.states[g] * p["e_last"] + first["update"]

    def finish(self):
        for g in range(self.n_groups):
            self.s_ref[g] = self.states[g]


def _ssd_kernel(xf_ref, bcf_ref, dtf_ref, laf_ref, xb_ref, bcb_ref, dtb_ref, lab_ref, e_ref, skip_ref,
                of_ref, ob_ref, sf_ref, sb_ref):
    @pl.when(pl.program_id(1) == 0)
    def _():
        sf_ref[...] = jnp.zeros_like(sf_ref)
        sb_ref[...] = jnp.zeros_like(sb_ref)

    fwd = _SsdDir(False, xf_ref, bcf_ref, dtf_ref, laf_ref, e_ref, of_ref, sf_ref, skip_ref)
    bwd = _SsdDir(True, xb_ref, bcb_ref, dtb_ref, lab_ref, e_ref, ob_ref, sb_ref)
    groups = range(fwd.n_groups)
    steps = [(d, k) for k in range(fwd.n_chunks) for d in (fwd, bwd)]
    d0, k0 = steps[0]
    sc = d0.scalars(d0.chunk(k0))
    ready = [d0.prep(d0.chunk(k0), g, d0.expand(g, sc)) for g in groups]
    for idx, (d, k) in enumerate(steps):
        nxt = steps[idx + 1] if idx + 1 < len(steps) else None
        if nxt is not None:
            nd, nk = nxt
            nsc = nd.scalars(nd.chunk(nk))
        cc = d.chunk(k)
        first = [d.first_matmuls(cc, g, ready[g]) for g in groups]
        following = []
        for g in groups:
            y_intra = d.intra(g, ready[g], first[g])
            if nxt is not None:
                following.append(nd.prep(nd.chunk(nk), g, nd.expand(g, nsc)))
            d.combine(cc, g, ready[g], first[g], y_intra)
        ready = following
    fwd.finish()
    bwd.finish()


def _ssd_scan(xs, bc, dtp, expand_mat, d_skip):
    b, tt, width = xs.shape
    nt = tt // TM
    nbc = bc.shape[2]
    gw = width // SSD_GROUPS

    def dir_specs(rev):
        blk = lambda i: _scan_block(i, nt, rev)
        d_col = 1 if rev else 0
        return [pl.BlockSpec((1, TM, width), lambda bi, i: (bi, blk(i), 0)),
                pl.BlockSpec((1, TM, nbc), lambda bi, i: (bi, blk(i), 0)),
                pl.BlockSpec((1, TM, LANES), lambda bi, i: (bi, blk(i), d_col)),
                pl.BlockSpec((1, TM, LANES), lambda bi, i: (bi, blk(i), 2 + d_col))]

    out_spec = lambda rev: pl.BlockSpec((1, TM, width), lambda bi, i: (bi, _scan_block(i, nt, rev), 0))
    return pl.pallas_call(
        _ssd_kernel,
        grid=(b, nt),
        in_specs=dir_specs(False) + dir_specs(True) + [pl.BlockSpec(expand_mat.shape, lambda bi, i: (0, 0)),
                                                       pl.BlockSpec((1, width), lambda bi, i: (0, 0))],
        out_specs=[out_spec(False), out_spec(True)],
        out_shape=[jax.ShapeDtypeStruct((b, tt, width), BF16)] * 2,
        scratch_shapes=[pltpu.VMEM((SSD_GROUPS, SSD_STATE, gw), F32)] * 2,
        compiler_params=_cparams(("parallel", "arbitrary")),
        name="ssd_scan",
    )(xs, bc, dtp, dtp, xs, bc, dtp, dtp, expand_mat, d_skip)


def _outproj1_kernel(yf_ref, yb_ref, z_ref, h_ref, m_ref, gain_ref, w_ref, o_ref):
    width = z_ref.shape[2]
    gw = width // SSD_GROUPS
    for bi in range(BB):
        acc = None
        for g in range(SSD_GROUPS):
            gs = slice(g * gw, (g + 1) * gw)
            yg = (yf_ref[bi, :, gs].astype(F32) + yb_ref[bi, :, gs].astype(F32)) * _silu(z_ref[bi, :, gs].astype(F32))
            ms = jnp.mean(yg * yg, axis=-1, keepdims=True)
            part = jnp.dot((yg * lax.rsqrt(ms + EPS) * gain_ref[:, gs]).astype(BF16), w_ref[gs, :],
                           preferred_element_type=F32)
            acc = part if acc is None else acc + part
        o_ref[bi] = h_ref[bi] + m_ref[bi, 2:3, :] * acc


def _outproj1(y_f, y_b, z, hh, mod, gain, w, n_lat_tiles):
    b, _, d = hh.shape
    width = z.shape[2]
    tile = _tile_spec
    in_specs = [tile(width, 3), tile(width, 3), tile(width, 3), tile(d, 3),
                _mod_spec(d, n_lat_tiles), _const_spec((1, width)), _const_spec(w.shape)]

    def outer(*refs):
        pltpu.emit_pipeline(_outproj1_kernel, grid=(b // BB, n_lat_tiles), in_specs=in_specs,
                            out_specs=[tile(d)])(*refs)

    any_spec = pl.BlockSpec(memory_space=pl.ANY)
    return pl.pallas_call(
        outer,
        in_specs=[any_spec] * 7,
        out_specs=any_spec,
        out_shape=jax.ShapeDtypeStruct((b, n_lat_tiles * TM, d), F32),
        compiler_params=pltpu.CompilerParams(vmem_limit_bytes=VMEM_LIMIT),
        name="outproj1",
    )(y_f, y_b, z, hh, mod, gain, w)


def _mod_rows(mods_l, b, d):
    six = mods_l.reshape(mods_l.shape[0], 6, d)
    lat = six[:b]
    ctx = jnp.broadcast_to(six[b:b + 1], (b, 6, d))
    both = jnp.stack([lat, ctx], axis=1)
    return jnp.pad(both, ((0, 0), (0, 0), (0, 2), (0, 0)))


def kernel(x, c, ctx, c_ctx, w_mod, b_mod, norm_mix, norm_ffn, ffn_w_up, ffn_conv_w, ffn_conv_b, ffn_w_down,
           hy_w_in, hy_w_out, na_q_gain, na_k_gain, na_rpb, hg_out_gain, hg_lb_fwd, hg_lb_bwd, ssd_w_in,
           ssd_conv_w, ssd_conv_b, ssd_dt_bias_fwd, ssd_dt_bias_bwd, ssd_a_log_fwd, ssd_a_log_bwd, ssd_d,
           ssd_norm_gain, ssd_w_out):
    b, t_lat, d = x.shape
    l_ctx = ctx.shape[1]
    assert l_ctx == TM and t_lat % TM == 0 and w_mod.shape[0] == 2 and b % BB == 0
    n_lat_tiles = t_lat // TM
    nt = n_lat_tiles + 1

    rows = ((b + 1 + 7) // 8) * 8
    cond = jnp.concatenate([c, c_ctx[None], jnp.zeros((rows - b - 1, d), F32)], axis=0)
    mods = _modulation(cond, w_mod, b_mod)
    mod0 = _mod_rows(mods[0], b, d)
    mod1 = _mod_rows(mods[1], b, d)
    row = lambda v: v.reshape(1, -1).astype(F32)

    hg_w = hg_lb_fwd.shape[1]
    na_w = hy_w_out.shape[1] - hg_w
    n_na_heads = na_w // NA_HEAD_DIM
    qg = row(jnp.tile(na_q_gain[0], n_na_heads)) * (NA_HEAD_DIM ** -0.5)
    kg = row(jnp.tile(na_k_gain[0], n_na_heads))
    qkv, hraw = _inproj0(x, ctx, mod0, row(norm_mix[0]), hy_w_in[0].astype(BF16), qg, kg)
    bias = _na_bias_table(na_rpb[0], t_lat // GRID_W)
    o_na = _na_attention(qkv, bias, t_lat)

    def lb_rows(lb_param):
        lb = jnp.cumsum(jax.nn.softmax(lb_param.astype(F32), axis=0), axis=0)[0]
        return jnp.pad(jnp.stack([lb, 1.0 - lb], axis=0), ((0, 6), (0, 0)))

    o_f, o_b = _hgrn_scan(hraw, lb_rows(hg_lb_fwd), lb_rows(hg_lb_bwd))
    hg_gain = row(jnp.tile(hg_out_gain[0], hg_w // HG_HEAD_DIM))
    h1 = _outproj0(o_na, o_f, o_b, hraw, x, ctx, mod0, hg_gain, hy_w_out[0].astype(BF16))

    mid0 = _ffn_up(h1, mod0, row(norm_ffn[0]), ffn_w_up[0].astype(BF16), ffn_conv_w[0], row(ffn_conv_b[0]),
                   nt, n_lat_tiles)
    h2 = _ffn_down(mid0, ffn_w_down[0].astype(BF16), h1, mod0, n_lat_tiles)

    n_heads = ssd_d.shape[1]
    inner = ssd_w_out.shape[1]
    nxbc = ssd_conv_w.shape[2]
    w1 = ssd_w_in[0]
    rep = LANES // n_heads
    w_dtf = jnp.tile(w1[:, inner + nxbc:inner + nxbc + n_heads], (1, rep))
    w_dtb = jnp.tile(w1[:, inner + nxbc + n_heads:], (1, rep))
    w1p = jnp.concatenate([w1[:, :inner + nxbc], w_dtf, w_dtb], axis=1).astype(BF16)
    dt_bias = row(jnp.concatenate([jnp.tile(ssd_dt_bias_fwd[0], rep), jnp.tile(ssd_dt_bias_bwd[0], rep)]))
    a_neg = row(jnp.concatenate([jnp.tile(-jnp.exp(ssd_a_log_fwd[0].astype(F32)), rep),
                                 jnp.tile(-jnp.exp(ssd_a_log_bwd[0].astype(F32)), rep)]))
    z, xs, bc, dtp = _inproj1(h2, mod1, row(norm_mix[1]), w1p, ssd_conv_w[0], row(ssd_conv_b[0]), dt_bias,
                              a_neg, inner, inner)

    lane = np.arange(LANES)[:, None]
    colh = (np.arange(inner) // SSD_HEAD_DIM)[None, :]
    expand_mat = jnp.asarray(((lane % n_heads == colh) & (lane < 3 * n_heads)).astype(np.float32), dtype=BF16)
    d_skip = row(jnp.repeat(ssd_d[0], SSD_HEAD_DIM))
    y_f, y_b = _ssd_scan(xs, bc, dtp, expand_mat, d_skip)
    h3 = _outproj1(y_f, y_b, z, h2, mod1, row(ssd_norm_gain[0]), ssd_w_out[0].astype(BF16), n_lat_tiles)

    mid1 = _ffn_up(h3, mod1, row(norm_ffn[1]), ffn_w_up[1].astype(BF16), ffn_conv_w[1], row(ffn_conv_b[1]),
                   n_lat_tiles, n_lat_tiles)
    return _ffn_down(mid1, ffn_w_down[1].astype(BF16), h3, mod1, n_lat_tiles)
```
